```python
import math
import jax, jax.numpy as jnp
from jax import lax
import numpy as np

D_MODEL = 1024
BATCH = 8
SEQ = 4096
DEPTH = 2

GRID_W = 64
CTX_LEN = 256
MIX_WIDTH = 2 * D_MODEL
SSD_WIDTH = MIX_WIDTH // 2
SSD_HEAD_DIM = 64
SSD_HEADS = SSD_WIDTH // SSD_HEAD_DIM
SSD_GROUPS = 2
SSD_STATE = 128
SSD_CONV = 5
SSD_CONV_DIM = SSD_WIDTH + 2 * SSD_GROUPS * SSD_STATE
MLA_WIDTH = MIX_WIDTH // 4
MLA_V_DIM = 64
MLA_HEADS = MLA_WIDTH // MLA_V_DIM
MLA_NOPE_DIM = 64
MLA_ROPE_DIM = 32
MLA_Q_LORA = 384
MLA_KV_LORA = 256
MLA_SCALE = (MLA_NOPE_DIM + MLA_ROPE_DIM) ** -0.5
RET_WIDTH = MIX_WIDTH // 4
RET_HEADS = 4
RET_V_DIM = RET_WIDTH // RET_HEADS
RET_QK_DIM = RET_V_DIM // 2
RET_QK_WIDTH = RET_HEADS * RET_QK_DIM

CHUNK = 128
Q_BLOCK = 128
ROPE_THETA = 10000.0
ALPHA = (2 * DEPTH) ** 0.25
BETA = (8 * DEPTH) ** -0.25
LN_EPS = 1e-5
RMS_EPS = 1e-6

IN_WIDTHS = (SSD_WIDTH, SSD_CONV_DIM, SSD_HEADS,
             MLA_Q_LORA, MLA_KV_LORA, MLA_ROPE_DIM, MLA_WIDTH,
             RET_QK_WIDTH, RET_QK_WIDTH, RET_WIDTH, RET_WIDTH)
IN_WIDTH = sum(IN_WIDTHS)

kernel_name = "hymba_ssd_mla_retention_prefix_trunk"


def in_splits():
    return [int(s) for s in np.cumsum(IN_WIDTHS)[:-1]]


def layer_norm(x, g, b):
    xf = x.astype(jnp.float32)
    mu = jnp.mean(xf, -1, keepdims=True)
    var = jnp.mean(jnp.square(xf - mu), -1, keepdims=True)
    return ((xf - mu) * lax.rsqrt(var + LN_EPS)).astype(x.dtype) * g + b


def head_norm(x):
    xf = x.astype(jnp.float32)
    mu = jnp.mean(xf, -1, keepdims=True)
    var = jnp.mean(jnp.square(xf - mu), -1, keepdims=True)
    return ((xf - mu) * lax.rsqrt(var + LN_EPS)).astype(x.dtype)


def rms_norm(x, g):
    xf = x.astype(jnp.float32)
    return (xf * lax.rsqrt(jnp.mean(xf * xf, -1, keepdims=True) + RMS_EPS)).astype(x.dtype) * g


def axial_rope(rows, rot_dim):
    t = jnp.arange(rows * GRID_W)
    row = (t // GRID_W).astype(jnp.float32)
    col = (t % GRID_W).astype(jnp.float32)
    n_freq = rot_dim // 4
    inv = ROPE_THETA ** (-jnp.arange(n_freq, dtype=jnp.float32) / n_freq)
    ang = jnp.concatenate([row[:, None] * inv, col[:, None] * inv], -1)
    return jnp.cos(ang), jnp.sin(ang)


def apply_rope(u, cos, sin):
    half = u.shape[-1] // 2
    u1, u2 = u[..., :half], u[..., half:]
    cos = cos.astype(u.dtype)
    sin = sin.astype(u.dtype)
    return jnp.concatenate([u1 * cos - u2 * sin, u2 * cos + u1 * sin], -1)


def dwconv_centred(u, w, b):
    pad = w.shape[0] // 2
    out = lax.conv_general_dilated(u, w[:, None, :], window_strides=(1,), padding=[(pad, pad)],
                                   dimension_numbers=('NWC', 'WIO', 'NWC'),
                                   feature_group_count=u.shape[-1])
    return out + b


def chunked_scan(q, k, v, log_a, h0):
    b, n, g, d_state = q.shape
    hg, p = v.shape[-2:]
    nc = n // CHUNK
    qc = q.reshape(b, nc, CHUNK, g, d_state)
    kc = k.reshape(b, nc, CHUNK, g, d_state)
    vc = v.reshape(b, nc, CHUNK, g, hg, p)
    cum = jnp.cumsum(log_a.astype(jnp.float32).reshape(b, nc, CHUNK, g, hg), axis=2)
    lower = jnp.tril(jnp.ones((CHUNK, CHUNK), dtype=bool))[:, :, None, None]
    seg = cum[:, :, :, None] - cum[:, :, None, :]
    decay = jnp.exp(jnp.where(lower, seg, -jnp.inf))
    scores = jnp.einsum('bcign,bcjgn->bcijg', qc, kc)
    y_intra = jnp.einsum('bcijgh,bcjghp->bcighp', scores[..., None] * decay, vc)
    to_end = jnp.exp(cum[:, :, -1:] - cum)
    states = jnp.einsum('bcjgn,bcjghp->bcghnp', kc, vc * to_end[..., None])
    chunk_decay = jnp.exp(cum[:, :, -1])

    def step(h, inp):
        s, dcy = inp
        return dcy[..., None, None] * h + s, h

    h_last, h_in = lax.scan(step, h0, (jnp.moveaxis(states, 1, 0), jnp.moveaxis(chunk_decay, 1, 0)))
    h_in = jnp.moveaxis(h_in, 0, 1)
    y_inter = jnp.einsum('bcign,bcghnp->bcighp', qc, h_in) * jnp.exp(cum)[..., None]
    y = (y_intra + y_inter).reshape(b, n, g, hg, p)
    return y.astype(v.dtype), h_last


def directional_scan(q_c, k_c, v_c, la_c, q_l, k_l, v_l, la_l, reverse):
    f = (lambda a: jnp.flip(a, 1)) if reverse else (lambda a: a)
    b, _, g, n = q_c.shape
    hg, p = v_c.shape[-2:]
    h0 = jnp.zeros((b, g, hg, n, p), jnp.float32)
    y_c, h_c = chunked_scan(f(q_c), f(k_c), f(v_c), f(la_c), h0)
    y_l, _ = chunked_scan(f(q_l), f(k_l), f(v_l), f(la_l), h_c)
    return f(y_c), f(y_l)


def ssd_prep(xbc, conv_w, conv_b):
    u = jax.nn.silu(dwconv_centred(xbc, conv_w, conv_b))
    b, n, _ = u.shape
    hg = SSD_HEADS // SSD_GROUPS
    xs = u[..., :SSD_WIDTH].reshape(b, n, SSD_GROUPS, hg, SSD_HEAD_DIM)
    bm = u[..., SSD_WIDTH:SSD_WIDTH + SSD_GROUPS * SSD_STATE].reshape(b, n, SSD_GROUPS, SSD_STATE)
    cm = u[..., SSD_WIDTH + SSD_GROUPS * SSD_STATE:].reshape(b, n, SSD_GROUPS, SSD_STATE)
    return xs, bm, cm


def ssd_mixer(p_c, p_l, conv_w, conv_b, a_log_f, a_log_b, dt_bias_f, dt_bias_b, d_skip, norm_w, with_ctx):
    z_c, xbc_c, dt_c = p_c
    z_l, xbc_l, dt_l = p_l
    xs_c, b_c, c_c = ssd_prep(xbc_c, conv_w, conv_b)
    xs_l, b_l, c_l = ssd_prep(xbc_l, conv_w, conv_b)
    hg = SSD_HEADS // SSD_GROUPS
    d = d_skip.reshape(SSD_GROUPS, hg)[..., None]
    y_c = xs_c * d
    y_l = xs_l * d
    for a_log, dt_bias, rev in ((a_log_f, dt_bias_f, False), (a_log_b, dt_bias_b, True)):
        a = -jnp.exp(a_log.astype(jnp.float32)).reshape(SSD_GROUPS, hg)
        dtc = jax.nn.softplus((dt_c + dt_bias).astype(jnp.float32)).reshape(*dt_c.shape[:2], SSD_GROUPS, hg)
        dtl = jax.nn.softplus((dt_l + dt_bias).astype(jnp.float32)).reshape(*dt_l.shape[:2], SSD_GROUPS, hg)
        yc, yl = directional_scan(c_c, b_c, (xs_c * dtc[..., None]).astype(xs_c.dtype), dtc * a,
                                  c_l, b_l, (xs_l * dtl[..., None]).astype(xs_l.dtype), dtl * a, rev)
        y_c = y_c + yc
        y_l = y_l + yl
    b, n = z_l.shape[:2]
    out_l = rms_norm(y_l.reshape(b, n, SSD_WIDTH).astype(z_l.dtype) * jax.nn.silu(z_l), norm_w)
    out_c = None
    if with_ctx:
        out_c = rms_norm(y_c.reshape(b, z_c.shape[1], SSD_WIDTH).astype(z_c.dtype) * jax.nn.silu(z_c), norm_w)
    return out_l, out_c


def mla_attend(q_nope, q_rope, k_nope, k_rope, v):
    s = (jnp.einsum('bqhd,bkhd->bhqk', q_nope, k_nope)
         + jnp.einsum('bqhr,bkr->bhqk', q_rope, k_rope)).astype(jnp.float32) * MLA_SCALE
    pr = jax.nn.softmax(s, axis=-1).astype(v.dtype)
    return jnp.einsum('bhqk,bkhd->bqhd', pr, v)


def mla_mixer(p_c, p_l, q_norm, w_uq, kv_norm, w_ukv, cos, sin, with_ctx):
    def project(cq, ckv, kr, rope):
        b, n, _ = cq.shape
        q = (rms_norm(cq, q_norm) @ w_uq).reshape(b, n, MLA_HEADS, MLA_NOPE_DIM + MLA_ROPE_DIM)
        kv = (rms_norm(ckv, kv_norm) @ w_ukv).reshape(b, n, MLA_HEADS, MLA_NOPE_DIM + MLA_V_DIM)
        q_nope, q_rope = q[..., :MLA_NOPE_DIM], q[..., MLA_NOPE_DIM:]
        k_nope, v = kv[..., :MLA_NOPE_DIM], kv[..., MLA_NOPE_DIM:]
        if rope:
            q_rope = apply_rope(q_rope, cos[:, None], sin[:, None])
            kr = apply_rope(kr, cos, sin)
        return q_nope, q_rope, k_nope, kr, v

    cq_c, ckv_c, kr_c, gate_c = p_c
    cq_l, ckv_l, kr_l, gate_l = p_l
    qn_c, qr_c, kn_c, krr_c, v_c = project(cq_c, ckv_c, kr_c, False)
    qn_l, qr_l, kn_l, krr_l, v_l = project(cq_l, ckv_l, kr_l, True)
    kn_all = jnp.concatenate([kn_c, kn_l], 1)
    kr_all = jnp.concatenate([krr_c, krr_l], 1)
    v_all = jnp.concatenate([v_c, v_l], 1)
    b, n = qn_l.shape[:2]
    nb = n // Q_BLOCK
    blk = lambda a: jnp.moveaxis(a.reshape(b, nb, Q_BLOCK, *a.shape[2:]), 1, 0)
    o = lax.map(lambda qs: mla_attend(qs[0], qs[1], kn_all, kr_all, v_all), (blk(qn_l), blk(qr_l)))
    out_l = jnp.moveaxis(o, 0, 1).reshape(b, n, MLA_WIDTH) * jax.nn.silu(gate_l)
    out_c = None
    if with_ctx:
        o_c = mla_attend(qn_c, qr_c, kn_c, krr_c, v_c)
        out_c = o_c.reshape(b, qn_c.shape[1], MLA_WIDTH) * jax.nn.silu(gate_c)
    return out_l, out_c


def retention_mixer(p_c, p_l, log_rate_f, log_rate_b, cos, sin, with_ctx):
    def heads(q, k, v, rope):
        b, n, _ = q.shape
        q = q.reshape(b, n, RET_HEADS, RET_QK_DIM)
        k = k.reshape(b, n, RET_HEADS, RET_QK_DIM)
        if rope:
            q = apply_rope(q, cos[:, None], sin[:, None])
            k = apply_rope(k, cos[:, None], sin[:, None])
        k = k * RET_QK_DIM ** -0.5
        v = v.reshape(b, n, RET_HEADS, 1, RET_V_DIM)
        return q, k, v

    q_c, k_c, v_c, gate_c = p_c
    q_l, k_l, v_l, gate_l = p_l
    qc, kc, vc = heads(q_c, k_c, v_c, False)
    ql, kl, vl = heads(q_l, k_l, v_l, True)
    b, n = ql.shape[:2]
    lc = qc.shape[1]
    ys_c, ys_l = [], []
    for log_rate, rev in ((log_rate_f, False), (log_rate_b, True)):
        la = -jnp.exp(log_rate.astype(jnp.float32))[:, None]
        yc, yl = directional_scan(qc, kc, vc, jnp.broadcast_to(la, (b, lc, RET_HEADS, 1)),
                                  ql, kl, vl, jnp.broadcast_to(la, (b, n, RET_HEADS, 1)), rev)
        ys_c.append(yc)
        ys_l.append(yl)
    y_l = head_norm((ys_l[0] + ys_l[1]).reshape(b, n, RET_HEADS, RET_V_DIM)).astype(gate_l.dtype)
    out_l = y_l.reshape(b, n, RET_WIDTH) * jax.nn.silu(gate_l)
    out_c = None
    if with_ctx:
        y_c = head_norm((ys_c[0] + ys_c[1]).reshape(b, lc, RET_HEADS, RET_V_DIM)).astype(gate_c.dtype)
        out_c = y_c.reshape(b, lc, RET_WIDTH) * jax.nn.silu(gate_c)
    return out_l, out_c


def trunk_layer(x, cx, mod_l, mod_c, w_in, ssd_conv_w, ssd_conv_b, ssd_a_log_f, ssd_a_log_b,
                ssd_dt_bias_f, ssd_dt_bias_b, ssd_d, ssd_norm_w, mla_q_norm, mla_w_uq, mla_kv_norm,
                mla_w_ukv, ret_log_rate_f, ret_log_rate_b, w_out, ln_g, ln_b,
                cos_mla, sin_mla, cos_ret, sin_ret, with_ctx):
    sh_l, sc_l, g_l = jnp.split(mod_l[:, None, :], 3, axis=-1)
    sh_c, sc_c, g_c = jnp.split(mod_c, 3, axis=-1)
    h_l = x * (1 + sc_l) + sh_l
    h_c = cx * (1 + sc_c) + sh_c
    parts_l = jnp.split(h_l @ w_in, in_splits(), axis=-1)
    parts_c = jnp.split(h_c @ w_in, in_splits(), axis=-1)
    ssd_l, ssd_c = ssd_mixer(parts_c[0:3], parts_l[0:3], ssd_conv_w, ssd_conv_b, ssd_a_log_f, ssd_a_log_b,
                             ssd_dt_bias_f, ssd_dt_bias_b, ssd_d, ssd_norm_w, with_ctx)
    mla_l, mla_c = mla_mixer(parts_c[3:7], parts_l[3:7], mla_q_norm, mla_w_uq, mla_kv_norm, mla_w_ukv,
                             cos_mla, sin_mla, with_ctx)
    ret_l, ret_c = retention_mixer(parts_c[7:11], parts_l[7:11], ret_log_rate_f, ret_log_rate_b,
                                   cos_ret, sin_ret, with_ctx)
    out_l = jnp.concatenate([ssd_l, mla_l, ret_l], -1) @ w_out
    x_new = layer_norm(ALPHA * x + g_l * out_l, ln_g, ln_b)
    cx_new = None
    if with_ctx:
        out_c = jnp.concatenate([ssd_c, mla_c, ret_c], -1) @ w_out
        cx_new = layer_norm(ALPHA * cx + g_c * out_c, ln_g, ln_b)
    return x_new, cx_new


def setup_inputs(seed: int = 0) -> dict:
    key = jax.random.key(seed)
    ks = jax.random.split(key, 32)
    f32 = jnp.float32
    nrm = lambda k, shape, std: std * jax.random.normal(k, shape, f32)
    x = nrm(ks[0], (BATCH, SEQ, D_MODEL), 1.0)
    c = nrm(ks[1], (BATCH, D_MODEL), 1.0)
    ctx = nrm(ks[2], (BATCH, CTX_LEN, D_MODEL), 1.0)
    c_ctx = nrm(ks[3], (D_MODEL,), 1.0)
    w_ada = nrm(ks[4], (DEPTH, D_MODEL, 3 * D_MODEL), D_MODEL ** -0.5)
    b_ada = nrm(ks[5], (DEPTH, 3 * D_MODEL), 0.02)
    w_in = nrm(ks[6], (DEPTH, D_MODEL, IN_WIDTH), D_MODEL ** -0.5)
    ssd_conv_w = nrm(ks[7], (DEPTH, SSD_CONV, SSD_CONV_DIM), SSD_CONV ** -0.5)
    ssd_conv_b = nrm(ks[8], (DEPTH, SSD_CONV_DIM), 0.02)
    ssd_a_log_f = jnp.log(jax.random.uniform(ks[9], (DEPTH, SSD_HEADS), f32, 1.0, 16.0))
    ssd_a_log_b = jnp.log(jax.random.uniform(ks[10], (DEPTH, SSD_HEADS), f32, 1.0, 16.0))
    dt_f = jnp.exp(jax.random.uniform(ks[11], (DEPTH, SSD_HEADS), f32, math.log(1e-3), math.log(1e-1)))
    dt_b = jnp.exp(jax.random.uniform(ks[12], (DEPTH, SSD_HEADS), f32, math.log(1e-3), math.log(1e-1)))
    ssd_dt_bias_f = dt_f + jnp.log(-jnp.expm1(-dt_f))
    ssd_dt_bias_b = dt_b + jnp.log(-jnp.expm1(-dt_b))
    ssd_d = 1.0 + nrm(ks[13], (DEPTH, SSD_HEADS), 0.1)
    ssd_norm_w = 1.0 + nrm(ks[14], (DEPTH, SSD_WIDTH), 0.02)
    mla_q_norm = 1.0 + nrm(ks[15], (DEPTH, MLA_Q_LORA), 0.02)
    mla_w_uq = nrm(ks[16], (DEPTH, MLA_Q_LORA, MLA_HEADS * (MLA_NOPE_DIM + MLA_ROPE_DIM)), MLA_Q_LORA ** -0.5)
    mla_kv_norm = 1.0 + nrm(ks[17], (DEPTH, MLA_KV_LORA), 0.02)
    mla_w_ukv = nrm(ks[18], (DEPTH, MLA_KV_LORA, MLA_HEADS * (MLA_NOPE_DIM + MLA_V_DIM)), MLA_KV_LORA ** -0.5)
    base_rate = jnp.log(-jnp.log1p(-(2.0 ** -(5.0 + jnp.arange(RET_HEADS, dtype=f32)))))
    ret_log_rate_f = base_rate + nrm(ks[19], (DEPTH, RET_HEADS), 0.05)
    ret_log_rate_b = base_rate + nrm(ks[20], (DEPTH, RET_HEADS), 0.05)
    w_out = nrm(ks[21], (DEPTH, MIX_WIDTH, D_MODEL), BETA * MIX_WIDTH ** -0.5)
    ln_g = 1.0 + nrm(ks[22], (DEPTH, D_MODEL), 0.02)
    ln_b = nrm(ks[23], (DEPTH, D_MODEL), 0.02)
    return {"x": x, "c": c, "ctx": ctx, "c_ctx": c_ctx, "w_ada": w_ada, "b_ada": b_ada, "w_in": w_in,
            "ssd_conv_w": ssd_conv_w, "ssd_conv_b": ssd_conv_b, "ssd_a_log_f": ssd_a_log_f,
            "ssd_a_log_b": ssd_a_log_b, "ssd_dt_bias_f": ssd_dt_bias_f, "ssd_dt_bias_b": ssd_dt_bias_b,
            "ssd_d": ssd_d, "ssd_norm_w": ssd_norm_w, "mla_q_norm": mla_q_norm, "mla_w_uq": mla_w_uq,
            "mla_kv_norm": mla_kv_norm, "mla_w_ukv": mla_w_ukv, "ret_log_rate_f": ret_log_rate_f,
            "ret_log_rate_b": ret_log_rate_b, "w_out": w_out, "ln_g": ln_g, "ln_b": ln_b}


def reference(x, c, ctx, c_ctx, w_ada, b_ada, w_in, ssd_conv_w, ssd_conv_b, ssd_a_log_f, ssd_a_log_b,
              ssd_dt_bias_f, ssd_dt_bias_b, ssd_d, ssd_norm_w, mla_q_norm, mla_w_uq, mla_kv_norm, mla_w_ukv,
              ret_log_rate_f, ret_log_rate_b, w_out, ln_g, ln_b):
    rows = x.shape[1] // GRID_W
    cos_mla, sin_mla = axial_rope(rows, MLA_ROPE_DIM)
    cos_ret, sin_ret = axial_rope(rows, RET_QK_DIM)
    sc = jax.nn.silu(c)
    sc_ctx = jax.nn.silu(c_ctx)
    cx = ctx
    for i in range(DEPTH):
        mod_l = sc @ w_ada[i] + b_ada[i]
        mod_c = sc_ctx @ w_ada[i] + b_ada[i]
        x, cx = trunk_layer(x, cx, mod_l, mod_c, w_in[i], ssd_conv_w[i], ssd_conv_b[i], ssd_a_log_f[i],
                            ssd_a_log_b[i], ssd_dt_bias_f[i], ssd_dt_bias_b[i], ssd_d[i], ssd_norm_w[i],
                            mla_q_norm[i], mla_w_uq[i], mla_kv_norm[i], mla_w_ukv[i], ret_log_rate_f[i],
                            ret_log_rate_b[i], w_out[i], ln_g[i], ln_b[i],
                            cos_mla, sin_mla, cos_ret, sin_ret, i < DEPTH - 1)
    return x
```

```python
import functools
import math

import jax
import jax.numpy as jnp
import numpy as np
from jax import lax
from jax.experimental import pallas as pl
from jax.experimental.pallas import tpu as pltpu

F32 = jnp.float32
BF16 = jnp.bfloat16

LANES = 128
SUBLANES = 8
VMEM_LIMIT_BYTES = 56 * 1024 * 1024

GRID_W = 64
D_MODEL = 1024
SSD_WIDTH = 1024
SSD_HEAD_DIM = 64
SSD_HEADS = 16
SSD_GROUPS = 2
SSD_STATE = 128
SSD_CONV = 5
SSD_CONV_DIM = SSD_WIDTH + 2 * SSD_GROUPS * SSD_STATE
MLA_WIDTH = 512
MLA_V_DIM = 64
MLA_HEADS = 8
MLA_NOPE_DIM = 64
MLA_ROPE_DIM = 32
MLA_Q_LORA = 384
MLA_KV_LORA = 256
MLA_SCALE = (MLA_NOPE_DIM + MLA_ROPE_DIM) ** -0.5
RET_WIDTH = 512
RET_HEADS = 4
RET_V_DIM = 128
RET_QK_DIM = 64
RET_QK_WIDTH = RET_HEADS * RET_QK_DIM
CHUNK = 128
ROPE_THETA = 10000.0
LN_EPS = 1e-5
RMS_EPS = 1e-6
IN_WIDTHS = (SSD_WIDTH, SSD_CONV_DIM, SSD_HEADS, MLA_Q_LORA, MLA_KV_LORA, MLA_ROPE_DIM, MLA_WIDTH,
             RET_QK_WIDTH, RET_QK_WIDTH, RET_WIDTH, RET_WIDTH)

TOKEN_TILE = 256
MLA_HEAD_SLOT = 128
MLA_ROPE_LANE = MLA_NOPE_DIM
NEG_BIG = -1e30


def _params(*sem):
    return pltpu.CompilerParams(dimension_semantics=sem, vmem_limit_bytes=VMEM_LIMIT_BYTES)


def _silu(v):
    return v * jax.nn.sigmoid(v)


def _softplus(v):
    return jnp.maximum(v, 0.0) + jnp.log1p(jnp.exp(-jnp.abs(v)))


def _dot(a, b):
    return jnp.dot(a, b, preferred_element_type=F32)


def _dot_exact(a, b):
    return jnp.dot(a, b, preferred_element_type=F32, precision=lax.Precision.HIGHEST)


def _dot_nt(a, b):
    return lax.dot_general(a, b, (((1,), (1,)), ((), ())), preferred_element_type=F32)


def _ada_kernel(c_ref, w_ref, b_ref, o_ref):
    o_ref[...] = _dot_exact(_silu(c_ref[...]), w_ref[...]) + b_ref[...]


def _ada(cc, w, b):
    rows, d = cc.shape
    n = w.shape[1]
    tn = 1024
    return pl.pallas_call(
        _ada_kernel,
        out_shape=jax.ShapeDtypeStruct((rows, n), F32),
        grid=(n // tn,),
        in_specs=[pl.BlockSpec((rows, d), lambda j: (0, 0)),
                  pl.BlockSpec((d, tn), lambda j: (0, j)),
                  pl.BlockSpec((1, tn), lambda j: (0, j))],
        out_specs=pl.BlockSpec((rows, tn), lambda j: (0, j)),
        compiler_params=_params("arbitrary"),
        name="ada_mod",
    )(cc, w, b.reshape(1, n))


PROJ_WIDTHS = (SSD_WIDTH, SSD_CONV_DIM, LANES, MLA_Q_LORA, MLA_KV_LORA, MLA_WIDTH,
               RET_QK_WIDTH, RET_QK_WIDTH, RET_WIDTH, RET_WIDTH)
DT_LANE = 0
KROPE_LANE = MLA_ROPE_LANE


def _relayout_w_in(w_in):
    splits = [int(s) for s in np.cumsum(IN_WIDTHS)[:-1]]
    z, xbc, dt, cq, ckv, kr, mg, rq, rk, rv, rg = jnp.split(w_in, splits, axis=-1)
    d = w_in.shape[0]
    small = jnp.concatenate([dt, jnp.zeros((d, KROPE_LANE - SSD_HEADS), w_in.dtype), kr,
                             jnp.zeros((d, LANES - KROPE_LANE - MLA_ROPE_DIM), w_in.dtype)], axis=-1)
    return jnp.concatenate([z, xbc, small, cq, ckv, mg, rq, rk, rv, rg], axis=-1).astype(BF16)


def _inproj_kernel(x_ref, mod_ref, w_ref, *out_refs):
    x = x_ref[0]
    mod = mod_ref[0, 0]
    sh = mod[:, :D_MODEL]
    sc = mod[:, D_MODEL:2 * D_MODEL]
    h = (x * (1.0 + sc) + sh).astype(BF16)
    off = 0
    for ref, width in zip(out_refs, PROJ_WIDTHS):
        ref[0] = _dot(h, w_ref[:, off:off + width])
        off += width


def _inproj(xs, mod_all, w):
    bsz, s, d = xs.shape
    nt = s // TOKEN_TILE
    n_ctx_tiles = 1
    tot = sum(PROJ_WIDTHS)
    out_shape = [jax.ShapeDtypeStruct((bsz, s, wd), F32) for wd in PROJ_WIDTHS]
    out_specs = [pl.BlockSpec((1, TOKEN_TILE, wd), lambda b, t: (b, t, 0)) for wd in PROJ_WIDTHS]
    return pl.pallas_call(
        _inproj_kernel,
        out_shape=out_shape,
        grid=(bsz, nt),
        in_specs=[pl.BlockSpec((1, TOKEN_TILE, d), lambda b, t: (b, t, 0)),
                  pl.BlockSpec((1, 1, 1, 3 * d), lambda b, t: (b, jnp.minimum(t, n_ctx_tiles), 0, 0)),
                  pl.BlockSpec((d, tot), lambda b, t: (0, 0))],
        out_specs=out_specs,
        compiler_params=_params("parallel", "arbitrary"),
        name="in_proj",
    )(xs, mod_all, w)


def _chunk_index(d, c, nc_ctx, nc):
    bwd = jnp.where(c < nc_ctx, nc_ctx - 1 - c, nc + nc_ctx - 1 - c)
    return jnp.where(d == 0, c, bwd)


def _tri_masks(d):
    ii = lax.broadcasted_iota(jnp.int32, (CHUNK, CHUNK), 0)
    jj = lax.broadcasted_iota(jnp.int32, (CHUNK, CHUNK), 1)
    sgn = 1 - 2 * d
    mask = (ii - jj) * sgn >= 0
    mask_t = (jj - ii) * sgn >= 0
    return mask, mask_t


def _ssd_kernel(xbc_ref, prev_ref, next_ref, dt_ref, dtt_ref, z_ref, convw_ref, convb_ref,
                alog_ref, dtb_ref, alogt_ref, dtbt_ref, dwide_ref, normw_ref,
                out_ref, yf_ref, h_ref, *, nc_ctx, nc):
    d = pl.program_id(1)
    c = pl.program_id(2)
    cidx = _chunk_index(d, c, nc_ctx, nc)

    @pl.when(c == 0)
    def _():
        h_ref[...] = jnp.zeros_like(h_ref)

    first = jnp.logical_or(cidx == 0, cidx == nc_ctx)
    last = jnp.logical_or(cidx == nc_ctx - 1, cidx == nc - 1)
    prev = prev_ref[0] * jnp.where(first, 0.0, 1.0)
    nxt = next_ref[0] * jnp.where(last, 0.0, 1.0)
    ext = jnp.concatenate([prev, xbc_ref[0], nxt], axis=0)
    pad = SSD_CONV // 2
    acc = convb_ref[...]
    for k in range(SSD_CONV):
        lo = SUBLANES - pad + k
        acc = acc + convw_ref[k:k + 1, :] * ext[lo:lo + CHUNK, :]
    u = _silu(acc)
    xs = u[:, :SSD_WIDTH]
    xs_bf = xs.astype(BF16)

    lane = lax.broadcasted_iota(jnp.int32, (CHUNK, LANES), 1)
    dt_col = _softplus(dt_ref[0] + dtb_ref[0])
    la_col = jnp.where(lane < SSD_HEADS, dt_col * -jnp.exp(alog_ref[0]), 0.0)
    dt_row = _softplus(dtt_ref[0] + dtbt_ref[0])
    la_row = dt_row * -jnp.exp(alogt_ref[0])
    mask, mask_t = _tri_masks(d)
    tri = jnp.where(mask, 1.0, 0.0)
    tri_t = jnp.where(mask_t, 1.0, 0.0)
    cum_col = _dot_exact(tri, la_col)
    cum_row = _dot_exact(la_row, tri_t)
    tot_row = jnp.sum(la_row, axis=1, keepdims=True)
    w_row = dt_row * jnp.exp(tot_row - cum_row)
    etot_row = jnp.exp(tot_row)

    half = lane < SSD_HEAD_DIM
    heads_per_group = SSD_HEADS // SSD_GROUPS
    ys = []
    gmats = {}
    for k in range(SSD_HEADS // 2):
        g = (2 * k) // heads_per_group
        if g not in gmats:
            b_g = u[:, SSD_WIDTH + g * SSD_STATE:SSD_WIDTH + (g + 1) * SSD_STATE]
            c_g = u[:, SSD_WIDTH + (SSD_GROUPS + g) * SSD_STATE:SSD_WIDTH + (SSD_GROUPS + g + 1) * SSD_STATE]
            gmats[g] = (_dot_nt(c_g.astype(BF16), b_g.astype(BF16)), c_g, b_g.T)
        scores, c_g, bt_g = gmats[g]
        xs_pair = xs_bf[:, k * LANES:(k + 1) * LANES]
        h_pair = h_ref[k]
        rhs = jnp.concatenate([xs_pair, h_pair.astype(BF16)], axis=0)
        y_heads, h_heads, etots = [], [], []
        for hd in (2 * k, 2 * k + 1):
            cb = jnp.broadcast_to(cum_col[:, hd:hd + 1], (CHUNK, CHUNK))
            seg = cb - cum_row[hd:hd + 1, :]
            dec = jnp.exp(jnp.where(mask, seg, NEG_BIG))
            m = scores * dec * dt_row[hd:hd + 1, :]
            lhs = jnp.concatenate([m, c_g * jnp.exp(cb)], axis=1).astype(BF16)
            y_heads.append(_dot(lhs, rhs))
            btw = (bt_g * w_row[hd:hd + 1, :]).astype(BF16)
            h_heads.append(_dot(btw, xs_pair))
            etots.append(jnp.broadcast_to(etot_row[hd:hd + 1, :], (1, LANES)))
        ys.append(jnp.where(half, y_heads[0], y_heads[1]))
        etot = jnp.where(half[:1], etots[0], etots[1])
        h_ref[k] = etot * h_pair + jnp.where(half, h_heads[0], h_heads[1])
    y = jnp.concatenate(ys, axis=1)

    @pl.when(d == 0)
    def _():
        yf_ref[cidx] = y

    @pl.when(d == 1)
    def _():
        tot = yf_ref[cidx] + y + xs * dwide_ref[...]
        gated = tot * _silu(z_ref[0])
        ms = jnp.mean(gated * gated, axis=-1, keepdims=True)
        out_ref[0] = gated * lax.rsqrt(ms + RMS_EPS) * normw_ref[...]


def _ssd(z, xbc, small, dtt, conv_w, conv_b, alog, dtb, alogt, dtbt, dwide, normw, nc_ctx):
    bsz, s, _ = z.shape
    nc = s // CHUNK
    first_bwd = nc_ctx - 1
    rows8 = CHUNK // SUBLANES
    nblk8 = s // SUBLANES

    def cix(d, c):
        return _chunk_index(d, c, nc_ctx, nc)

    def late(d, c):
        return jnp.where(d == 0, first_bwd, cix(d, c))

    kern = functools.partial(_ssd_kernel, nc_ctx=nc_ctx, nc=nc)
    cd = SSD_CONV_DIM
    return pl.pallas_call(
        kern,
        out_shape=jax.ShapeDtypeStruct((bsz, s, SSD_WIDTH), F32),
        grid=(bsz, 2, nc),
        in_specs=[
            pl.BlockSpec((1, CHUNK, cd), lambda b, d, c: (b, cix(d, c), 0)),
            pl.BlockSpec((1, SUBLANES, cd), lambda b, d, c: (b, jnp.maximum(cix(d, c) * rows8 - 1, 0), 0)),
            pl.BlockSpec((1, SUBLANES, cd),
                         lambda b, d, c: (b, jnp.minimum((cix(d, c) + 1) * rows8, nblk8 - 1), 0)),
            pl.BlockSpec((1, CHUNK, LANES), lambda b, d, c: (b, cix(d, c), 0)),
            pl.BlockSpec((1, SSD_HEADS, CHUNK), lambda b, d, c: (b, 0, cix(d, c))),
            pl.BlockSpec((1, CHUNK, SSD_WIDTH), lambda b, d, c: (b, late(d, c), 0)),
            pl.BlockSpec((SSD_CONV, cd), lambda b, d, c: (0, 0)),
            pl.BlockSpec((1, cd), lambda b, d, c: (0, 0)),
            pl.BlockSpec((1, 1, LANES), lambda b, d, c: (d, 0, 0)),
            pl.BlockSpec((1, 1, LANES), lambda b, d, c: (d, 0, 0)),
            pl.BlockSpec((1, SSD_HEADS, CHUNK), lambda b, d, c: (d, 0, 0)),
            pl.BlockSpec((1, SSD_HEADS, CHUNK), lambda b, d, c: (d, 0, 0)),
            pl.BlockSpec((1, SSD_WIDTH), lambda b, d, c: (0, 0)),
            pl.BlockSpec((1, SSD_WIDTH), lambda b, d, c: (0, 0)),
        ],
        out_specs=pl.BlockSpec((1, CHUNK, SSD_WIDTH), lambda b, d, c: (b, late(d, c), 0)),
        scratch_shapes=[pltpu.VMEM((nc, CHUNK, SSD_WIDTH), F32),
                        pltpu.VMEM((SSD_HEADS // 2, SSD_STATE, LANES), F32)],
        compiler_params=_params("parallel", "arbitrary", "arbitrary"),
        name="ssd_scan",
    )(xbc, xbc, xbc, small, dtt, z, conv_w, conv_b, alog, dtb, alogt, dtbt, dwide, normw)


def _rope3(v, cos_t, sin_a, sin_b, shift):
    width = v.shape[-1]
    return v * cos_t + pltpu.roll(v, width - shift, 1) * sin_a + pltpu.roll(v, shift, 1) * sin_b


def _ret_kernel(q_ref, k_ref, v_ref, g_ref, cos_ref, sa_ref, sb_ref, rate_ref,
                out_ref, yf_ref, h_ref, *, nc_ctx, nc):
    d = pl.program_id(1)
    c = pl.program_id(2)
    cidx = _chunk_index(d, c, nc_ctx, nc)

    @pl.when(c == 0)
    def _():
        h_ref[...] = jnp.zeros_like(h_ref)

    half = RET_QK_DIM // 2
    q = _rope3(q_ref[0], cos_ref[...], sa_ref[...], sb_ref[...], half)
    kk = _rope3(k_ref[0], cos_ref[...], sa_ref[...], sb_ref[...], half) * (RET_QK_DIM ** -0.5)
    v = v_ref[0]
    kt = kk.T
    mask, _ = _tri_masks(d)
    ii = lax.broadcasted_iota(jnp.int32, (CHUNK, CHUNK), 0)
    jj = lax.broadcasted_iota(jnp.int32, (CHUNK, CHUNK), 1)
    dist = jnp.abs(ii - jj).astype(F32)
    pos_i = lax.broadcasted_iota(jnp.int32, (CHUNK, RET_QK_DIM), 0)
    steps_col = jnp.where(d == 0, pos_i + 1, CHUNK - pos_i).astype(F32)
    pos_j = lax.broadcasted_iota(jnp.int32, (1, CHUNK), 1)
    left_row = jnp.where(d == 0, CHUNK - 1 - pos_j, pos_j).astype(F32)
    ys = []
    for hd in range(RET_HEADS):
        la = -jnp.exp(rate_ref[0, :, hd:hd + 1])
        q_h = q[:, hd * RET_QK_DIM:(hd + 1) * RET_QK_DIM]
        k_h = kk[:, hd * RET_QK_DIM:(hd + 1) * RET_QK_DIM]
        v_h = v[:, hd * RET_V_DIM:(hd + 1) * RET_V_DIM].astype(BF16)
        h_h = h_ref[hd]
        dec = jnp.exp(jnp.where(mask, dist * la, NEG_BIG))
        sc = _dot_nt(q_h.astype(BF16), k_h.astype(BF16)) * dec
        y_h = _dot(sc.astype(BF16), v_h)
        y_h = y_h + _dot((q_h * jnp.exp(steps_col * la)).astype(BF16), h_h.astype(BF16))
        ktw = (kt[hd * RET_QK_DIM:(hd + 1) * RET_QK_DIM, :] * jnp.exp(left_row * la)).astype(BF16)
        h_ref[hd] = jnp.exp(CHUNK * la) * h_h + _dot(ktw, v_h)
        ys.append(y_h)
    y = jnp.concatenate(ys, axis=1)

    @pl.when(d == 0)
    def _():
        yf_ref[cidx] = y

    @pl.when(d == 1)
    def _():
        tot = yf_ref[cidx] + y
        gate = _silu(g_ref[0])
        outs = []
        for hd in range(RET_HEADS):
            t_h = tot[:, hd * RET_V_DIM:(hd + 1) * RET_V_DIM]
            mu = jnp.mean(t_h, axis=-1, keepdims=True)
            var = jnp.mean(jnp.square(t_h - mu), axis=-1, keepdims=True)
            outs.append((t_h - mu) * lax.rsqrt(var + LN_EPS))
        out_ref[0] = jnp.concatenate(outs, axis=1) * gate


def _ret(rq, rk, rv, rg, cos_t, sin_a, sin_b, rates, nc_ctx):
    bsz, s, _ = rq.shape
    nc = s // CHUNK
    first_bwd = nc_ctx - 1

    def cix(d, c):
        return _chunk_index(d, c, nc_ctx, nc)

    def late(d, c):
        return jnp.where(d == 0, first_bwd, cix(d, c))

    kern = functools.partial(_ret_kernel, nc_ctx=nc_ctx, nc=nc)
    return pl.pallas_call(
        kern,
        out_shape=jax.ShapeDtypeStruct((bsz, s, RET_WIDTH), F32),
        grid=(bsz, 2, nc),
        in_specs=[
            pl.BlockSpec((1, CHUNK, RET_QK_WIDTH), lambda b, d, c: (b, cix(d, c), 0)),
            pl.BlockSpec((1, CHUNK, RET_QK_WIDTH), lambda b, d, c: (b, cix(d, c), 0)),
            pl.BlockSpec((1, CHUNK, RET_WIDTH), lambda b, d, c: (b, cix(d, c), 0)),
            pl.BlockSpec((1, CHUNK, RET_WIDTH), lambda b, d, c: (b, late(d, c), 0)),
            pl.BlockSpec((CHUNK, RET_QK_WIDTH), lambda b, d, c: (cix(d, c), 0)),
            pl.BlockSpec((CHUNK, RET_QK_WIDTH), lambda b, d, c: (cix(d, c), 0)),
            pl.BlockSpec((CHUNK, RET_QK_WIDTH), lambda b, d, c: (cix(d, c), 0)),
            pl.BlockSpec((1, 1, LANES), lambda b, d, c: (d, 0, 0)),
        ],
        out_specs=pl.BlockSpec((1, CHUNK, RET_WIDTH), lambda b, d, c: (b, late(d, c), 0)),
        scratch_shapes=[pltpu.VMEM((nc, CHUNK, RET_WIDTH), F32),
                        pltpu.VMEM((RET_HEADS, RET_QK_DIM, RET_V_DIM), F32)],
        compiler_params=_params("parallel", "arbitrary", "arbitrary"),
        name="ret_scan",
    )(rq, rk, rv, rg, cos_t, sin_a, sin_b, rates)


def _rms(v, g):
    return v * lax.rsqrt(jnp.mean(v * v, axis=-1, keepdims=True) + RMS_EPS) * g


def _mla_proj_kernel(cq_ref, ckv_ref, small_ref, cos_ref, sa_ref, sb_ref, qn_ref, kvn_ref,
                     wq_ref, wk_ref, wv_ref, q_ref, k_ref, v_ref):
    cqn = _rms(cq_ref[0], qn_ref[...]).astype(BF16)
    ckvn = _rms(ckv_ref[0], kvn_ref[...]).astype(BF16)
    q = _dot(cqn, wq_ref[...])
    kn = _dot(ckvn, wk_ref[...])
    v_ref[0] = _dot(ckvn, wv_ref[...]).astype(BF16)
    cos_t, sin_a, sin_b = cos_ref[...], sa_ref[...], sb_ref[...]
    lane = lax.broadcasted_iota(jnp.int32, small_ref.shape[1:], 1)
    in_rope = jnp.logical_and(lane >= KROPE_LANE, lane < KROPE_LANE + MLA_ROPE_DIM)
    kr = _rope3(jnp.where(in_rope, small_ref[0], 0.0), cos_t, sin_a, sin_b, MLA_ROPE_DIM // 2)
    for hd in range(MLA_HEADS):
        sl = slice(hd * MLA_HEAD_SLOT, (hd + 1) * MLA_HEAD_SLOT)
        q_ref[0, :, sl] = (_rope3(q[:, sl], cos_t, sin_a, sin_b, MLA_ROPE_DIM // 2) * MLA_SCALE).astype(BF16)
        k_ref[0, :, sl] = (kn[:, sl] + kr).astype(BF16)


def _mla_proj(cq, ckv, small, cos_t, sin_a, sin_b, qn, kvn, wq, wk, wv):
    bsz, s, _ = cq.shape
    nt = s // TOKEN_TILE
    hw = MLA_HEADS * MLA_HEAD_SLOT
    tok = lambda wd: pl.BlockSpec((1, TOKEN_TILE, wd), lambda b, t: (b, t, 0))
    tab = pl.BlockSpec((TOKEN_TILE, LANES), lambda b, t: (t, 0))
    full = lambda a: pl.BlockSpec(a.shape, lambda b, t: (0,) * a.ndim)
    return pl.pallas_call(
        _mla_proj_kernel,
        out_shape=[jax.ShapeDtypeStruct((bsz, s, hw), BF16),
                   jax.ShapeDtypeStruct((bsz, s, hw), BF16),
                   jax.ShapeDtypeStruct((bsz, s, MLA_WIDTH), BF16)],
        grid=(bsz, nt),
        in_specs=[tok(MLA_Q_LORA), tok(MLA_KV_LORA), tok(LANES), tab, tab, tab,
                  full(qn), full(kvn), full(wq), full(wk), full(wv)],
        out_specs=[tok(hw), tok(hw), tok(MLA_WIDTH)],
        compiler_params=_params("parallel", "arbitrary"),
        name="mla_proj",
    )(cq, ckv, small, cos_t, sin_a, sin_b, qn, kvn, wq, wk, wv)


def _mla_attn_kernel(q_ref, k_ref, v_ref, g_ref, o_ref):
    v = v_ref[0]
    outs = []
    for hd in range(2):
        sl = slice(hd * MLA_HEAD_SLOT, (hd + 1) * MLA_HEAD_SLOT)
        s = _dot_nt(q_ref[0, :, sl], k_ref[0, :, sl])
        m = jnp.max(s, axis=-1, keepdims=True)
        p = jnp.exp(s - m)
        l = jnp.sum(p, axis=-1, keepdims=True)
        outs.append(_dot(p.astype(BF16), v) / l)
    lane = lax.broadcasted_iota(jnp.int32, outs[0].shape, 1)
    o = jnp.where(lane < MLA_V_DIM, outs[0], outs[1])
    o_ref[0] = o * _silu(g_ref[0])


def _mla_attn(q, k, v, gate, q_lo, q_hi, n_keys, out_rows, out_lo):
    bsz = q.shape[0]
    tq = TOKEN_TILE
    pair_w = 2 * MLA_HEAD_SLOT
    return pl.pallas_call(
        _mla_attn_kernel,
        out_shape=jax.ShapeDtypeStruct((bsz, out_rows, MLA_WIDTH), F32),
        grid=(bsz, MLA_HEADS // 2, q_hi - q_lo),
        in_specs=[pl.BlockSpec((1, tq, pair_w), lambda b, j, t: (b, t + q_lo, j)),
                  pl.BlockSpec((1, n_keys, pair_w), lambda b, j, t: (b, 0, j)),
                  pl.BlockSpec((1, n_keys, LANES), lambda b, j, t: (b, 0, j)),
                  pl.BlockSpec((1, tq, LANES), lambda b, j, t: (b, t + q_lo, j))],
        out_specs=pl.BlockSpec((1, tq, LANES), lambda b, j, t: (b, t + q_lo - out_lo, j)),
        compiler_params=_params("parallel", "arbitrary", "arbitrary"),
        name="mla_attn",
    )(q, k, v, gate)


def _outproj_kernel(ssd_ref, mla_ref, ret_ref, x_ref, mod_ref, w_ref, g_ref, b_ref, o_ref, *, alpha):
    acc = _dot(ssd_ref[0].astype(BF16), w_ref[:SSD_WIDTH, :])
    acc = acc + _dot(mla_ref[0].astype(BF16), w_ref[SSD_WIDTH:SSD_WIDTH + MLA_WIDTH, :])
    acc = acc + _dot(ret_ref[0].astype(BF16), w_ref[SSD_WIDTH + MLA_WIDTH:, :])
    gate = mod_ref[0, 0][:, 2 * D_MODEL:]
    r = alpha * x_ref[0] + gate * acc
    mu = jnp.mean(r, axis=-1, keepdims=True)
    var = jnp.mean(jnp.square(r - mu), axis=-1, keepdims=True)
    o_ref[0] = (r - mu) * lax.rsqrt(var + LN_EPS) * g_ref[...] + b_ref[...]


def _outproj(ssd_o, mla_o, mla_lo, ret_o, xs, mod_all, w, ln_g, ln_b, t_lo, alpha):
    bsz, s, d = xs.shape
    nt = s // TOKEN_TILE
    n_ctx_tiles = 1
    tok = lambda wd: pl.BlockSpec((1, TOKEN_TILE, wd), lambda b, t: (b, t + t_lo, 0))
    kern = functools.partial(_outproj_kernel, alpha=alpha)
    return pl.pallas_call(
        kern,
        out_shape=jax.ShapeDtypeStruct((bsz, s - t_lo * TOKEN_TILE, d), F32),
        grid=(bsz, nt - t_lo),
        in_specs=[tok(SSD_WIDTH),
                  pl.BlockSpec((1, TOKEN_TILE, MLA_WIDTH), lambda b, t: (b, t + t_lo - mla_lo, 0)),
                  tok(RET_WIDTH), tok(d),
                  pl.BlockSpec((1, 1, 1, 3 * d), lambda b, t: (b, jnp.minimum(t + t_lo, n_ctx_tiles), 0, 0)),
                  pl.BlockSpec(w.shape, lambda b, t: (0, 0)),
                  pl.BlockSpec((1, d), lambda b, t: (0, 0)),
                  pl.BlockSpec((1, d), lambda b, t: (0, 0))],
        out_specs=pl.BlockSpec((1, TOKEN_TILE, d), lambda b, t: (b, t, 0)),
        compiler_params=_params("parallel", "arbitrary"),
        name="out_proj",
    )(ssd_o, mla_o, ret_o, xs, mod_all, w, ln_g.reshape(1, d), ln_b.reshape(1, d))


def _rope_tables(n_ctx, n_lat, rot_dim, width, lane0, period):
    t = jnp.arange(n_lat)
    row = (t // GRID_W).astype(F32)
    col = (t % GRID_W).astype(F32)
    n_freq = rot_dim // 4
    inv = ROPE_THETA ** (-jnp.arange(n_freq, dtype=F32) / n_freq)
    ang = jnp.concatenate([row[:, None] * inv, col[:, None] * inv], -1)
    cos, sin = jnp.cos(ang), jnp.sin(ang)
    half = rot_dim // 2
    lane = np.arange(width) % period - lane0
    in_rot = (lane >= 0) & (lane < rot_dim)
    idx = np.where(in_rot, lane % half, 0)
    first = in_rot & (lane < half)
    second = in_rot & (lane >= half)
    cos_t = jnp.where(in_rot[None, :], cos[:, idx], 1.0)
    sin_a = jnp.where(first[None, :], -sin[:, idx], 0.0)
    sin_b = jnp.where(second[None, :], sin[:, idx], 0.0)
    ctx_pad = lambda a, v: jnp.concatenate([jnp.full((n_ctx, width), v, F32), a], axis=0)
    return ctx_pad(cos_t, 1.0), ctx_pad(sin_a, 0.0), ctx_pad(sin_b, 0.0)


def _mla_weights(w_uq, w_ukv):
    qd = MLA_NOPE_DIM + MLA_ROPE_DIM
    wq = w_uq.reshape(MLA_Q_LORA, MLA_HEADS, qd)
    wq = jnp.pad(wq, ((0, 0), (0, 0), (0, MLA_HEAD_SLOT - qd))).reshape(MLA_Q_LORA, MLA_HEADS * MLA_HEAD_SLOT)
    wkv = w_ukv.reshape(MLA_KV_LORA, MLA_HEADS, MLA_NOPE_DIM + MLA_V_DIM)
    wk = jnp.pad(wkv[..., :MLA_NOPE_DIM], ((0, 0), (0, 0), (0, MLA_HEAD_SLOT - MLA_NOPE_DIM)))
    wk = wk.reshape(MLA_KV_LORA, MLA_HEADS * MLA_HEAD_SLOT)
    wv = wkv[..., MLA_NOPE_DIM:].reshape(MLA_KV_LORA, MLA_WIDTH)
    return wq.astype(BF16), wk.astype(BF16), wv.astype(BF16)


def _lane_pad(v):
    return jnp.pad(v, (0, LANES - v.shape[0])).reshape(1, LANES)


def kernel(x, c, ctx, c_ctx, w_ada, b_ada, w_in, ssd_conv_w, ssd_conv_b, ssd_a_log_f, ssd_a_log_b,
           ssd_dt_bias_f, ssd_dt_bias_b, ssd_d, ssd_norm_w, mla_q_norm, mla_w_uq, mla_kv_norm, mla_w_ukv,
           ret_log_rate_f, ret_log_rate_b, w_out, ln_g, ln_b):
    bsz, n_lat, d = x.shape
    n_ctx = ctx.shape[1]
    depth = w_in.shape[0]
    assert d == D_MODEL and n_ctx == TOKEN_TILE and n_lat % TOKEN_TILE == 0
    alpha = (2 * depth) ** 0.25
    s = n_ctx + n_lat
    nc_ctx = n_ctx // CHUNK
    ctx_tiles = n_ctx // TOKEN_TILE
    nt = s // TOKEN_TILE

    cos_m, sa_m, sb_m = _rope_tables(n_ctx, n_lat, MLA_ROPE_DIM, LANES, MLA_ROPE_LANE, LANES)
    cos_r, sa_r, sb_r = _rope_tables(n_ctx, n_lat, RET_QK_DIM, RET_QK_WIDTH, 0, RET_QK_DIM)

    rows = 16
    cc = jnp.zeros((rows, d), F32).at[:bsz].set(c).at[bsz].set(c_ctx)
    xs = jnp.concatenate([ctx, x], axis=1)

    for i in range(depth):
        mod = _ada(cc, w_ada[i], b_ada[i])
        mod_all = jnp.stack([jnp.broadcast_to(mod[bsz], (bsz, 3 * d)), mod[:bsz]], axis=1)
        mod_all = mod_all.reshape(bsz, 2, 1, 3 * d)
        z, xbc, small, cq, ckv, mg, rq, rk, rv, rg = _inproj(xs, mod_all, _relayout_w_in(w_in[i]))

        dtt = jnp.swapaxes(small[:, :, DT_LANE:DT_LANE + SSD_HEADS], 1, 2)
        both = lambda f, b: jnp.stack([f, b], axis=0)
        alog = both(_lane_pad(ssd_a_log_f[i]), _lane_pad(ssd_a_log_b[i]))
        dtb = both(_lane_pad(ssd_dt_bias_f[i]), _lane_pad(ssd_dt_bias_b[i]))
        head_rows = lambda v: jnp.broadcast_to(v[:, None], (SSD_HEADS, CHUNK))
        alogt = both(head_rows(ssd_a_log_f[i]), head_rows(ssd_a_log_b[i]))
        dtbt = both(head_rows(ssd_dt_bias_f[i]), head_rows(ssd_dt_bias_b[i]))
        dwide = jnp.repeat(ssd_d[i], SSD_HEAD_DIM).reshape(1, SSD_WIDTH)
        ssd_o = _ssd(z, xbc, small, dtt, ssd_conv_w[i], ssd_conv_b[i].reshape(1, -1), alog, dtb, alogt, dtbt,
                     dwide, ssd_norm_w[i].reshape(1, -1), nc_ctx)

        wq, wk, wv = _mla_weights(mla_w_uq[i], mla_w_ukv[i])
        q, k, v = _mla_proj(cq, ckv, small, cos_m, sa_m, sb_m, mla_q_norm[i].reshape(1, -1),
                            mla_kv_norm[i].reshape(1, -1), wq, wk, wv)
        last = i == depth - 1
        mla_l = _mla_attn(q, k, v, mg, ctx_tiles, nt, s, n_lat, ctx_tiles)
        if last:
            mla_o, mla_lo = mla_l, ctx_tiles
        else:
            mla_c = _mla_attn(q, k, v, mg, 0, ctx_tiles, n_ctx, n_ctx, 0)
            mla_o, mla_lo = jnp.concatenate([mla_c, mla_l], axis=1), 0

        rates = both(_lane_pad(ret_log_rate_f[i]), _lane_pad(ret_log_rate_b[i]))
        ret_o = _ret(rq, rk, rv, rg, cos_r, sa_r, sb_r, rates, nc_ctx)

        xs = _outproj(ssd_o, mla_o, mla_lo, ret_o, xs, mod_all, w_out[i].astype(BF16), ln_g[i], ln_b[i],
                      ctx_tiles if last else 0, alpha)
    return xs
```

```python
import functools
import math

import jax
import jax.numpy as jnp
import numpy as np
from jax import lax
from jax.experimental import pallas as pl
from jax.experimental.pallas import tpu as pltpu

F32 = jnp.float32
BF16 = jnp.bfloat16

LANES = 128
SUBLANES = 8
VMEM_LIMIT_BYTES = 56 * 1024 * 1024

GRID_W = 64
D_MODEL = 1024
SSD_WIDTH = 1024
SSD_HEAD_DIM = 64
SSD_HEADS = 16
SSD_GROUPS = 2
SSD_STATE = 128
SSD_CONV = 5
SSD_CONV_DIM = SSD_WIDTH + 2 * SSD_GROUPS * SSD_STATE
MLA_WIDTH = 512
MLA_V_DIM = 64
MLA_HEADS = 8
MLA_NOPE_DIM = 64
MLA_ROPE_DIM = 32
MLA_Q_LORA = 384
MLA_KV_LORA = 256
MLA_SCALE = (MLA_NOPE_DIM + MLA_ROPE_DIM) ** -0.5
RET_WIDTH = 512
RET_HEADS = 4
RET_V_DIM = 128
RET_QK_DIM = 64
RET_QK_WIDTH = RET_HEADS * RET_QK_DIM
CHUNK = 128
ROPE_THETA = 10000.0
LN_EPS = 1e-5
RMS_EPS = 1e-6
IN_WIDTHS = (SSD_WIDTH, SSD_CONV_DIM, SSD_HEADS, MLA_Q_LORA, MLA_KV_LORA, MLA_ROPE_DIM, MLA_WIDTH,
             RET_QK_WIDTH, RET_QK_WIDTH, RET_WIDTH, RET_WIDTH)

TOKEN_TILE = 256
MLA_HEAD_SLOT = 128
MLA_ROPE_LANE = MLA_NOPE_DIM
MLA_PAIRS = MLA_HEADS // 2
MLA_V_SLOT = 256
MLA_Q_SCALE = MLA_SCALE * math.log2(math.e)
NEG_BIG = -1e30


def _params(*sem):
    return pltpu.CompilerParams(dimension_semantics=sem, vmem_limit_bytes=VMEM_LIMIT_BYTES)


def _silu(v):
    return v * jax.nn.sigmoid(v)


def _softplus(v):
    return jnp.maximum(v, 0.0) + jnp.log1p(jnp.exp(-jnp.abs(v)))


def _dot(a, b):
    return jnp.dot(a, b, preferred_element_type=F32)


def _dot_exact(a, b):
    return jnp.dot(a, b, preferred_element_type=F32, precision=lax.Precision.HIGHEST)


def _dot_nt(a, b):
    return lax.dot_general(a, b, (((1,), (1,)), ((), ())), preferred_element_type=F32)


def _ada_kernel(c_ref, w_ref, b_ref, o_ref):
    o_ref[...] = _dot_exact(_silu(c_ref[...]), w_ref[...]) + b_ref[...]


def _ada(cc, w, b):
    rows, d = cc.shape
    n = w.shape[1]
    tn = 1024
    return pl.pallas_call(
        _ada_kernel,
        out_shape=jax.ShapeDtypeStruct((rows, n), F32),
        grid=(n // tn,),
        in_specs=[pl.BlockSpec((rows, d), lambda j: (0, 0)),
                  pl.BlockSpec((d, tn), lambda j: (0, j)),
                  pl.BlockSpec((1, tn), lambda j: (0, j))],
        out_specs=pl.BlockSpec((rows, tn), lambda j: (0, j)),
        compiler_params=_params("arbitrary"),
        name="ada_mod",
    )(cc, w, b.reshape(1, n))


PROJ_WIDTHS = (SSD_WIDTH, SSD_CONV_DIM, LANES, MLA_Q_LORA, MLA_KV_LORA, MLA_WIDTH,
               RET_QK_WIDTH, RET_QK_WIDTH, RET_WIDTH, RET_WIDTH)
DT_LANE = 0
KROPE_LANE = MLA_ROPE_LANE


def _relayout_w_in(w_in):
    splits = [int(s) for s in np.cumsum(IN_WIDTHS)[:-1]]
    z, xbc, dt, cq, ckv, kr, mg, rq, rk, rv, rg = jnp.split(w_in, splits, axis=-1)
    d = w_in.shape[0]
    small = jnp.concatenate([dt, jnp.zeros((d, KROPE_LANE - SSD_HEADS), w_in.dtype), kr,
                             jnp.zeros((d, LANES - KROPE_LANE - MLA_ROPE_DIM), w_in.dtype)], axis=-1)
    return jnp.concatenate([z, xbc, small, cq, ckv, mg, rq, rk, rv, rg], axis=-1).astype(BF16)


def _stream_specs(stream, t_lo, ctx_tiles):
    d = stream[0].shape[-1]
    if len(stream) == 1:
        return [pl.BlockSpec((1, TOKEN_TILE, d), lambda b, t: (b, t + t_lo, 0))]
    return [pl.BlockSpec((1, TOKEN_TILE, d), lambda b, t: (b, jnp.minimum(t + t_lo, ctx_tiles - 1), 0)),
            pl.BlockSpec((1, TOKEN_TILE, d), lambda b, t: (b, jnp.maximum(t + t_lo - ctx_tiles, 0), 0))]


def _stream_tile(refs, t, ctx_tiles):
    if len(refs) == 1:
        return refs[0][0]
    return jnp.where(t < ctx_tiles, refs[0][0], refs[1][0])


def _inproj_kernel(*refs, n_stream, ctx_tiles):
    x_refs, (mod_ref, w_ref), out_refs = refs[:n_stream], refs[n_stream:n_stream + 2], refs[n_stream + 2:]
    x = _stream_tile(x_refs, pl.program_id(1), ctx_tiles)
    mod = mod_ref[0, 0]
    sh = mod[:, :D_MODEL]
    sc = mod[:, D_MODEL:2 * D_MODEL]
    h = (x * (1.0 + sc) + sh).astype(BF16)
    off = 0
    for ref, width in zip(out_refs, PROJ_WIDTHS):
        ref[0] = _dot(h, w_ref[:, off:off + width])
        off += width


def _inproj(stream, mod_all, w, ctx_tiles):
    bsz, d = stream[0].shape[0], stream[0].shape[-1]
    s = sum(a.shape[1] for a in stream)
    nt = s // TOKEN_TILE
    tot = sum(PROJ_WIDTHS)
    out_shape = [jax.ShapeDtypeStruct((bsz, s, wd), F32) for wd in PROJ_WIDTHS]
    out_specs = [pl.BlockSpec((1, TOKEN_TILE, wd), lambda b, t: (b, t, 0)) for wd in PROJ_WIDTHS]
    kern = functools.partial(_inproj_kernel, n_stream=len(stream), ctx_tiles=ctx_tiles)
    return pl.pallas_call(
        kern,
        out_shape=out_shape,
        grid=(bsz, nt),
        in_specs=_stream_specs(stream, 0, ctx_tiles) + [
            pl.BlockSpec((1, 1, 1, 3 * d), lambda b, t: (b, jnp.minimum(t, ctx_tiles) // ctx_tiles, 0, 0)),
            pl.BlockSpec((d, tot), lambda b, t: (0, 0))],
        out_specs=out_specs,
        compiler_params=_params("parallel", "arbitrary"),
        name="in_proj",
    )(*stream, mod_all, w)


def _chunk_index(d, c, nc_ctx, nc):
    bwd = jnp.where(c < nc_ctx, nc_ctx - 1 - c, nc + nc_ctx - 1 - c)
    return jnp.where(d == 0, c, bwd)


def _tri_masks(d):
    ii = lax.broadcasted_iota(jnp.int32, (CHUNK, CHUNK), 0)
    jj = lax.broadcasted_iota(jnp.int32, (CHUNK, CHUNK), 1)
    sgn = 1 - 2 * d
    mask = (ii - jj) * sgn >= 0
    mask_t = (jj - ii) * sgn >= 0
    return mask, mask_t


def _rope3(v, cos_t, sin_a, sin_b, shift):
    width = v.shape[-1]
    return v * cos_t + pltpu.roll(v, width - shift, 1) * sin_a + pltpu.roll(v, shift, 1) * sin_b


ROW_CUM, ROW_DT, ROW_WEND, ROW_ETOT = 0, 16, 32, 48
ROW_PACK = 64


def _mixer_prep_kernel(xbc_ref, prev_ref, next_ref, small_ref, rq_ref, rk_ref, rv_ref,
                       convw_ref, convb_ref, alogt_ref, dtbt_ref,
                       cos_ref, sa_ref, sb_ref,
                       xs_ref, xst_ref, bc_ref, colp_ref, rowp_ref, q_ref, kbf_ref, kt_ref, v_ref,
                       *, nc_ctx, nc):
    c = pl.program_id(1)
    first = jnp.logical_or(c == 0, c == nc_ctx)
    last = jnp.logical_or(c == nc_ctx - 1, c == nc - 1)
    prev = prev_ref[0] * jnp.where(first, 0.0, 1.0)
    nxt = next_ref[0] * jnp.where(last, 0.0, 1.0)
    ext = jnp.concatenate([prev, xbc_ref[0], nxt], axis=0)
    pad = SSD_CONV // 2
    acc = convb_ref[...]
    for k in range(SSD_CONV):
        lo = SUBLANES - pad + k
        acc = acc + convw_ref[k:k + 1, :] * ext[lo:lo + CHUNK, :]
    u = _silu(acc)
    xs = u[:, :SSD_WIDTH]
    xs_ref[0] = xs.astype(BF16)
    xst_ref[0] = xs.T.astype(BF16)
    bc_ref[0] = u[:, SSD_WIDTH:]

    dt_raw_t = small_ref[0].T[DT_LANE:DT_LANE + SSD_HEADS, :]
    pad_rows = jnp.zeros((CHUNK - SSD_HEADS, CHUNK), F32)
    for d in range(2):
        _, mask_t = _tri_masks(d)
        tri_t = jnp.where(mask_t, 1.0, 0.0)
        dt_row = _softplus(dt_raw_t + dtbt_ref[d])
        la_row = dt_row * -jnp.exp(alogt_ref[d])
        cum_row = _dot_exact(la_row, tri_t)
        colp_ref[0, d] = jnp.concatenate([cum_row, pad_rows], axis=0).T
        tot_row = jnp.sum(la_row, axis=1, keepdims=True)
        base = d * ROW_PACK
        rowp_ref[0, 0, base + ROW_CUM:base + ROW_CUM + SSD_HEADS, :] = cum_row
        rowp_ref[0, 0, base + ROW_DT:base + ROW_DT + SSD_HEADS, :] = dt_row
        rowp_ref[0, 0, base + ROW_WEND:base + ROW_WEND + SSD_HEADS, :] = dt_row * jnp.exp(tot_row - cum_row)
        rowp_ref[0, 0, base + ROW_ETOT:base + ROW_ETOT + SSD_HEADS, :] = jnp.broadcast_to(
            jnp.exp(tot_row), (SSD_HEADS, CHUNK))

    half = RET_QK_DIM // 2
    q_ref[0] = _rope3(rq_ref[0], cos_ref[...], sa_ref[...], sb_ref[...], half)
    kk = _rope3(rk_ref[0], cos_ref[...], sa_ref[...], sb_ref[...], half) * (RET_QK_DIM ** -0.5)
    kbf_ref[0] = kk.astype(BF16)
    kt_ref[0] = kk.T
    v_ref[0] = rv_ref[0].astype(BF16)


def _mixer_prep(xbc, small, rq, rk, rv, conv_w, conv_b, alogt, dtbt, cos_t, sin_a, sin_b, nc_ctx):
    bsz, s, cd = xbc.shape
    nc = s // CHUNK
    rows8 = CHUNK // SUBLANES
    nblk8 = s // SUBLANES
    tokc = lambda wd: pl.BlockSpec((1, CHUNK, wd), lambda b, c: (b, c, 0))
    full = lambda a: pl.BlockSpec(a.shape, lambda b, c: (0,) * a.ndim)
    tab = pl.BlockSpec((CHUNK, RET_QK_WIDTH), lambda b, c: (c, 0))
    kern = functools.partial(_mixer_prep_kernel, nc_ctx=nc_ctx, nc=nc)
    return pl.pallas_call(
        kern,
        out_shape=[jax.ShapeDtypeStruct((bsz, s, SSD_WIDTH), BF16),
                   jax.ShapeDtypeStruct((bsz, SSD_WIDTH, s), BF16),
                   jax.ShapeDtypeStruct((bsz, s, cd - SSD_WIDTH), F32),
                   jax.ShapeDtypeStruct((bsz, 2, s, LANES), F32),
                   jax.ShapeDtypeStruct((bsz, nc, 2 * ROW_PACK, CHUNK), F32),
                   jax.ShapeDtypeStruct((bsz, s, RET_QK_WIDTH), F32),
                   jax.ShapeDtypeStruct((bsz, s, RET_QK_WIDTH), BF16),
                   jax.ShapeDtypeStruct((bsz, RET_QK_WIDTH, s), F32),
                   jax.ShapeDtypeStruct((bsz, s, RET_WIDTH), BF16)],
        grid=(bsz, nc),
        in_specs=[tokc(cd),
                  pl.BlockSpec((1, SUBLANES, cd), lambda b, c: (b, jnp.maximum(c * rows8 - 1, 0), 0)),
                  pl.BlockSpec((1, SUBLANES, cd), lambda b, c: (b, jnp.minimum((c + 1) * rows8, nblk8 - 1), 0)),
                  tokc(LANES), tokc(RET_QK_WIDTH), tokc(RET_QK_WIDTH), tokc(RET_WIDTH),
                  full(conv_w), full(conv_b), full(alogt), full(dtbt),
                  tab, tab, tab],
        out_specs=[tokc(SSD_WIDTH),
                   pl.BlockSpec((1, SSD_WIDTH, CHUNK), lambda b, c: (b, 0, c)),
                   tokc(cd - SSD_WIDTH),
                   pl.BlockSpec((1, 2, CHUNK, LANES), lambda b, c: (b, 0, c, 0)),
                   pl.BlockSpec((1, 1, 2 * ROW_PACK, CHUNK), lambda b, c: (b, c, 0, 0)),
                   tokc(RET_QK_WIDTH), tokc(RET_QK_WIDTH),
                   pl.BlockSpec((1, RET_QK_WIDTH, CHUNK), lambda b, c: (b, 0, c)),
                   tokc(RET_WIDTH)],
        compiler_params=_params("parallel", "arbitrary"),
        name="mixer_prep",
    )(xbc, xbc, xbc, small, rq, rk, rv, conv_w, conv_b, alogt, dtbt, cos_t, sin_a, sin_b)


def _ret_consts_kernel(rate_ref, dec_ref, ecum_ref, wend_ref, etot_ref):
    d = pl.program_id(0)
    mask, _ = _tri_masks(d)
    ii = lax.broadcasted_iota(jnp.int32, (CHUNK, CHUNK), 0)
    jj = lax.broadcasted_iota(jnp.int32, (CHUNK, CHUNK), 1)
    dist = jnp.abs(ii - jj).astype(F32)
    pos_i = lax.broadcasted_iota(jnp.int32, (CHUNK, RET_QK_DIM), 0)
    steps_col = jnp.where(d == 0, pos_i + 1, CHUNK - pos_i).astype(F32)
    pos_j = lax.broadcasted_iota(jnp.int32, (RET_QK_DIM, CHUNK), 1)
    left_row = jnp.where(d == 0, CHUNK - 1 - pos_j, pos_j).astype(F32)
    for hd in range(RET_HEADS):
        la = -jnp.exp(rate_ref[0, :, hd:hd + 1])
        dec_ref[0, hd] = jnp.exp(jnp.where(mask, dist * la, NEG_BIG))
        ecum_ref[0, :, hd * RET_QK_DIM:(hd + 1) * RET_QK_DIM] = jnp.exp(steps_col * la)
        wend_ref[0, hd * RET_QK_DIM:(hd + 1) * RET_QK_DIM, :] = jnp.exp(left_row * la)
        etot_ref[0, hd] = jnp.broadcast_to(jnp.exp(CHUNK * la), (1, LANES))


def _ret_consts(rates):
    return pl.pallas_call(
        _ret_consts_kernel,
        out_shape=[jax.ShapeDtypeStruct((2, RET_HEADS, CHUNK, CHUNK), F32),
                   jax.ShapeDtypeStruct((2, CHUNK, RET_QK_WIDTH), F32),
                   jax.ShapeDtypeStruct((2, RET_QK_WIDTH, CHUNK), F32),
                   jax.ShapeDtypeStruct((2, RET_HEADS, 1, LANES), F32)],
        grid=(2,),
        in_specs=[pl.BlockSpec((1, 1, LANES), lambda d: (d, 0, 0))],
        out_specs=[pl.BlockSpec((1, RET_HEADS, CHUNK, CHUNK), lambda d: (d, 0, 0, 0)),
                   pl.BlockSpec((1, CHUNK, RET_QK_WIDTH), lambda d: (d, 0, 0)),
                   pl.BlockSpec((1, RET_QK_WIDTH, CHUNK), lambda d: (d, 0, 0)),
                   pl.BlockSpec((1, RET_HEADS, 1, LANES), lambda d: (d, 0, 0, 0))],
        compiler_params=_params("arbitrary"),
        name="ret_consts",
    )(rates)


def _mixer_scan_kernel(xs_ref, xst_ref, bc_ref, colp_ref, rowp_ref, z_ref, dwide_ref, normw_ref,
                       q_ref, kbf_ref, kt_ref, v_ref, g_ref, dec_ref, ecum_ref, wend_ref, etot_ref,
                       ssd_out_ref, ret_out_ref, yf_ref, yr_ref, ht_ref, hr_ref, *, nc_ctx, nc):
    d = pl.program_id(1)
    c = pl.program_id(2)
    cidx = _chunk_index(d, c, nc_ctx, nc)

    @pl.when(c == 0)
    def _():
        ht_ref[...] = jnp.zeros_like(ht_ref)
        hr_ref[...] = jnp.zeros_like(hr_ref)

    mask, _ = _tri_masks(d)

    q = q_ref[0]
    q_bf = q.astype(BF16)
    k_bf = kbf_ref[0]
    qe_bf = (q * ecum_ref[0]).astype(BF16)
    ktw_bf = (kt_ref[0] * wend_ref[0]).astype(BF16)
    v_bf = v_ref[0]
    yrs = []
    ret_lhs = {}

    def ret_scores(hd):
        qs = slice(hd * RET_QK_DIM, (hd + 1) * RET_QK_DIM)
        sc = _dot_nt(q_bf[:, qs], k_bf[:, qs]) * dec_ref[0, hd]
        ret_lhs[hd] = jnp.concatenate([sc.astype(BF16), qe_bf[:, qs]], axis=1)

    def ret_update(hd):
        qs = slice(hd * RET_QK_DIM, (hd + 1) * RET_QK_DIM)
        v_h = v_bf[:, hd * RET_V_DIM:(hd + 1) * RET_V_DIM]
        h_h = hr_ref[hd]
        rhs = jnp.concatenate([v_h, h_h.astype(BF16)], axis=0)
        yrs.append(_dot(ret_lhs[hd], rhs))
        hr_ref[hd] = etot_ref[0, hd] * h_h + _dot(ktw_bf[qs, :], v_h)

    xs_bf = xs_ref[0]
    bc = bc_ref[0]
    cum_col = colp_ref[0, 0]
    rowp = rowp_ref[0, 0]
    cum_row = rowp[ROW_CUM:ROW_CUM + SSD_HEADS]
    dt_row = rowp[ROW_DT:ROW_DT + SSD_HEADS]
    wend_row = rowp[ROW_WEND:ROW_WEND + SSD_HEADS]
    etot_rowp = rowp[ROW_ETOT:ROW_ETOT + SSD_HEADS]
    lane = lax.broadcasted_iota(jnp.int32, (CHUNK, LANES), 1)
    sub = lax.broadcasted_iota(jnp.int32, (CHUNK, LANES), 0)
    lane_lo = lane < SSD_HEAD_DIM
    sub_lo = sub < SSD_HEAD_DIM
    heads_per_group = SSD_HEADS // SSD_GROUPS
    ys = []
    gmats = {}
    for k in range(SSD_HEADS // 2):
        g = (2 * k) // heads_per_group
        if g not in gmats:
            b_g = bc[:, g * SSD_STATE:(g + 1) * SSD_STATE].astype(BF16)
            c_g = bc[:, (SSD_GROUPS + g) * SSD_STATE:(SSD_GROUPS + g + 1) * SSD_STATE]
            gmats[g] = (_dot_nt(c_g.astype(BF16), b_g), c_g, b_g)
        scores, c_g, b_g = gmats[g]
        ha, hb = 2 * k, 2 * k + 1
        xs_pair = xs_bf[:, k * LANES:(k + 1) * LANES]
        ht_pair = ht_ref[k]
        rhs = jnp.concatenate([xs_pair, ht_pair.T.astype(BF16)], axis=0)
        y_heads = []
        for hd in (ha, hb):
            cb = jnp.broadcast_to(cum_col[:, hd:hd + 1], (CHUNK, CHUNK))
            seg = cb - cum_row[hd:hd + 1, :]
            dec = jnp.exp(jnp.where(mask, seg, NEG_BIG))
            m = scores * dec * dt_row[hd:hd + 1, :]
            lhs = jnp.concatenate([m, c_g * jnp.exp(cb)], axis=1).astype(BF16)
            y_heads.append(_dot(lhs, rhs))
        ys.append(jnp.where(lane_lo, y_heads[0], y_heads[1]))
        w_pair = jnp.where(sub_lo, wend_row[ha:ha + 1, :], wend_row[hb:hb + 1, :])
        e_pair = jnp.where(sub_lo, etot_rowp[ha:ha + 1, :], etot_rowp[hb:hb + 1, :])
        xw = (xst_ref[0, k * LANES:(k + 1) * LANES, :].astype(F32) * w_pair).astype(BF16)
        ht_ref[k] = e_pair * ht_pair + _dot(xw, b_g)
        if k % 2 == 0:
            ret_scores(k // 2)
        else:
            ret_update(k // 2)
    y = jnp.concatenate(ys, axis=1)
    yr = jnp.concatenate(yrs, axis=1)

    @pl.when(d == 0)
    def _():
        yf_ref[cidx] = y
        yr_ref[cidx] = yr

    @pl.when(d == 1)
    def _():
        tot = yf_ref[cidx] + y + xs_bf.astype(F32) * dwide_ref[...]
        gated = tot * _silu(z_ref[0])
        ms = jnp.mean(gated * gated, axis=-1, keepdims=True)
        ssd_out_ref[0] = gated * lax.rsqrt(ms + RMS_EPS) * normw_ref[...]
        rtot = yr_ref[cidx] + yr
        outs = []
        for hd in range(RET_HEADS):
            t_h = rtot[:, hd * RET_V_DIM:(hd + 1) * RET_V_DIM]
            mu = jnp.mean(t_h, axis=-1, keepdims=True)
            var = jnp.mean(jnp.square(t_h - mu), axis=-1, keepdims=True)
            outs.append((t_h - mu) * lax.rsqrt(var + LN_EPS))
        ret_out_ref[0] = jnp.concatenate(outs, axis=1) * _silu(g_ref[0])


def _mixer_scan(prep, z, rg, dwide, normw, rconsts, nc_ctx):
    xs_bf, xst_bf, bc, colp, rowp, rq, rk_bf, rkt, rv_bf = prep
    dec, ecum, wend, etot = rconsts
    bsz, s, _ = z.shape
    nc = s // CHUNK
    first_bwd = nc_ctx - 1

    def cix(d, c):
        return _chunk_index(d, c, nc_ctx, nc)

    def late(d, c):
        return jnp.where(d == 0, first_bwd, cix(d, c))

    tokc = lambda wd: pl.BlockSpec((1, CHUNK, wd), lambda b, d, c: (b, cix(d, c), 0))
    tokl = lambda wd: pl.BlockSpec((1, CHUNK, wd), lambda b, d, c: (b, late(d, c), 0))
    chan = lambda wd: pl.BlockSpec((1, wd, CHUNK), lambda b, d, c: (b, 0, cix(d, c)))
    row1 = lambda wd: pl.BlockSpec((1, wd), lambda b, d, c: (0, 0))
    kern = functools.partial(_mixer_scan_kernel, nc_ctx=nc_ctx, nc=nc)
    return pl.pallas_call(
        kern,
        out_shape=[jax.ShapeDtypeStruct((bsz, s, SSD_WIDTH), F32),
                   jax.ShapeDtypeStruct((bsz, s, RET_WIDTH), F32)],
        grid=(bsz, 2, nc),
        in_specs=[tokc(SSD_WIDTH), chan(SSD_WIDTH), tokc(bc.shape[-1]),
                  pl.BlockSpec((1, 1, CHUNK, LANES), lambda b, d, c: (b, d, cix(d, c), 0)),
                  pl.BlockSpec((1, 1, ROW_PACK, CHUNK), lambda b, d, c: (b, cix(d, c), d, 0)),
                  tokl(SSD_WIDTH), row1(SSD_WIDTH), row1(SSD_WIDTH),
                  tokc(RET_QK_WIDTH), tokc(RET_QK_WIDTH), chan(RET_QK_WIDTH), tokc(RET_WIDTH), tokl(RET_WIDTH),
                  pl.BlockSpec((1, RET_HEADS, CHUNK, CHUNK), lambda b, d, c: (d, 0, 0, 0)),
                  pl.BlockSpec((1, CHUNK, RET_QK_WIDTH), lambda b, d, c: (d, 0, 0)),
                  pl.BlockSpec((1, RET_QK_WIDTH, CHUNK), lambda b, d, c: (d, 0, 0)),
                  pl.BlockSpec((1, RET_HEADS, 1, LANES), lambda b, d, c: (d, 0, 0, 0))],
        out_specs=[tokl(SSD_WIDTH), tokl(RET_WIDTH)],
        scratch_shapes=[pltpu.VMEM((nc, CHUNK, SSD_WIDTH), F32),
                        pltpu.VMEM((nc, CHUNK, RET_WIDTH), F32),
                        pltpu.VMEM((SSD_HEADS // 2, LANES, SSD_STATE), F32),
                        pltpu.VMEM((RET_HEADS, RET_QK_DIM, RET_V_DIM), F32)],
        compiler_params=_params("parallel", "arbitrary", "arbitrary"),
        name="mixer_scan",
    )(xs_bf, xst_bf, bc, colp, rowp, z, dwide, normw, rq, rk_bf, rkt, rv_bf, rg, dec, ecum, wend, etot)


def _rms(v, g):
    return v * lax.rsqrt(jnp.mean(v * v, axis=-1, keepdims=True) + RMS_EPS) * g


def _mla_proj_kernel(cq_ref, ckv_ref, small_ref, cos_ref, sa_ref, sb_ref, qn_ref, kvn_ref,
                     wq_ref, wk_ref, wv_ref, q_ref, k_ref, v_ref):
    cqn = _rms(cq_ref[0], qn_ref[...]).astype(BF16)
    ckvn = _rms(ckv_ref[0], kvn_ref[...]).astype(BF16)
    q = _dot(cqn, wq_ref[...])
    kn = _dot(ckvn, wk_ref[...])
    vlane = lax.broadcasted_iota(jnp.int32, (1, MLA_PAIRS * MLA_V_SLOT), 1)
    ones = jnp.where(vlane % MLA_V_SLOT >= 2 * MLA_V_DIM, 1.0, 0.0)
    v_ref[0] = (_dot(ckvn, wv_ref[...]) + ones).astype(BF16)
    cos_t, sin_a, sin_b = cos_ref[...], sa_ref[...], sb_ref[...]
    lane = lax.broadcasted_iota(jnp.int32, small_ref.shape[1:], 1)
    in_rope = jnp.logical_and(lane >= KROPE_LANE, lane < KROPE_LANE + MLA_ROPE_DIM)
    kr = _rope3(jnp.where(in_rope, small_ref[0], 0.0), cos_t, sin_a, sin_b, MLA_ROPE_DIM // 2)
    for hd in range(MLA_HEADS):
        sl = slice(hd * MLA_HEAD_SLOT, (hd + 1) * MLA_HEAD_SLOT)
        q_ref[0, :, sl] = (_rope3(q[:, sl], cos_t, sin_a, sin_b, MLA_ROPE_DIM // 2) * MLA_Q_SCALE).astype(BF16)
        k_ref[0, :, sl] = (kn[:, sl] + kr).astype(BF16)


def _mla_proj(cq, ckv, small, cos_t, sin_a, sin_b, qn, kvn, wq, wk, wv):
    bsz, s, _ = cq.shape
    nt = s // TOKEN_TILE
    hw = MLA_HEADS * MLA_HEAD_SLOT
    vw = MLA_PAIRS * MLA_V_SLOT
    tok = lambda wd: pl.BlockSpec((1, TOKEN_TILE, wd), lambda b, t: (b, t, 0))
    tab = pl.BlockSpec((TOKEN_TILE, LANES), lambda b, t: (t, 0))
    full = lambda a: pl.BlockSpec(a.shape, lambda b, t: (0,) * a.ndim)
    return pl.pallas_call(
        _mla_proj_kernel,
        out_shape=[jax.ShapeDtypeStruct((bsz, s, hw), BF16),
                   jax.ShapeDtypeStruct((bsz, s, hw), BF16),
                   jax.ShapeDtypeStruct((bsz, s, vw), BF16)],
        grid=(bsz, nt),
        in_specs=[tok(MLA_Q_LORA), tok(MLA_KV_LORA), tok(LANES), tab, tab, tab,
                  full(qn), full(kvn), full(wq), full(wk), full(wv)],
        out_specs=[tok(hw), tok(hw), tok(vw)],
        compiler_params=_params("parallel", "arbitrary"),
        name="mla_proj",
    )(cq, ckv, small, cos_t, sin_a, sin_b, qn, kvn, wq, wk, wv)


ATTN_KEY_CHUNK = 256


def _attn_step(q_ref, k_ref, v_ref, g_ref, o_ref, slot_a, slot_b, n_keys):
    heads = range(2)
    hsl = [slice(hd * MLA_HEAD_SLOT, (hd + 1) * MLA_HEAD_SLOT) for hd in heads]
    if slot_a is not None:
        q = [q_ref[0, :, hsl[hd]] for hd in heads]
        mcol = [None, None]
    if slot_b is not None:
        mrow = [slot_b[1][hd] for hd in heads]
        pv = [None, None]
    for c in range(n_keys // ATTN_KEY_CHUNK):
        ks = slice(c * ATTN_KEY_CHUNK, (c + 1) * ATTN_KEY_CHUNK)
        for hd in heads:
            if slot_a is not None:
                s = _dot_nt(q[hd], k_ref[0, ks, hsl[hd]])
                slot_a[0][hd, :, ks] = s
                cm = s[:, :LANES]
                for w in range(1, ATTN_KEY_CHUNK // LANES):
                    cm = jnp.maximum(cm, s[:, w * LANES:(w + 1) * LANES])
                mcol[hd] = cm if c == 0 else jnp.maximum(mcol[hd], cm)
            if slot_b is not None:
                p = jnp.exp2(slot_b[0][hd, :, ks] - mrow[hd]).astype(BF16)
                part = _dot(p, v_ref[0, ks, :])
                pv[hd] = part if c == 0 else pv[hd] + part
    if slot_a is not None:
        for hd in heads:
            slot_a[1][hd] = jnp.max(mcol[hd], axis=-1, keepdims=True)
    if slot_b is not None:
        outs = [pv[hd][:, :LANES] / pv[hd][:, LANES:] for hd in heads]
        lane = lax.broadcasted_iota(jnp.int32, outs[0].shape, 1)
        o = jnp.where(lane < MLA_V_DIM, outs[0], outs[1])
        o_ref[0] = o * _silu(g_ref[0])


def _mla_attn_ctx_kernel(q_ref, k_ref, v_ref, g_ref, o_ref, s_ref, m_ref):
    n_keys = k_ref.shape[1]
    _attn_step(q_ref, k_ref, v_ref, g_ref, o_ref, (s_ref, m_ref), None, n_keys)
    _attn_step(q_ref, k_ref, v_ref, g_ref, o_ref, None, (s_ref, m_ref), n_keys)


def _mla_attn_ctx(q, k, v, gate, ctx_tiles):
    bsz, s, _ = q.shape
    tq = TOKEN_TILE
    n_ctx = ctx_tiles * tq
    pair_w = 2 * MLA_HEAD_SLOT
    return pl.pallas_call(
        _mla_attn_ctx_kernel,
        out_shape=jax.ShapeDtypeStruct((bsz, s, MLA_WIDTH), F32),
        grid=(bsz, MLA_PAIRS, ctx_tiles),
        in_specs=[pl.BlockSpec((1, tq, pair_w), lambda b, j, t: (b, t, j)),
                  pl.BlockSpec((1, n_ctx, pair_w), lambda b, j, t: (b, 0, j)),
                  pl.BlockSpec((1, n_ctx, MLA_V_SLOT), lambda b, j, t: (b, 0, j)),
                  pl.BlockSpec((1, tq, LANES), lambda b, j, t: (b, t, j))],
        out_specs=pl.BlockSpec((1, tq, LANES), lambda b, j, t: (b, t, j)),
        scratch_shapes=[pltpu.VMEM((2, tq, n_ctx), F32), pltpu.VMEM((2, tq, 1), F32)],
        compiler_params=_params("parallel", "arbitrary", "arbitrary"),
        name="mla_attn_ctx",
    )(q, k, v, gate)


def _mla_attn_lat_kernel(*refs, n_tiles, aliased):
    if aliased:
        refs = refs[1:]
    q_ref, k_ref, v_ref, g_ref, o_ref, s0, m0, s1, m1 = refs
    n_keys = k_ref.shape[1]
    slots = ((s0, m0), (s1, m1))
    n = pl.program_id(0)

    @pl.when(n == 0)
    def _():
        _attn_step(q_ref, k_ref, v_ref, g_ref, o_ref, slots[0], None, n_keys)

    for par in range(2):
        @pl.when(jnp.logical_and(jnp.logical_and(n > 0, n < n_tiles), n % 2 == par))
        def _():
            _attn_step(q_ref, k_ref, v_ref, g_ref, o_ref, slots[par], slots[1 - par], n_keys)

    @pl.when(n == n_tiles)
    def _():
        _attn_step(q_ref, k_ref, v_ref, g_ref, o_ref, None, slots[(n_tiles - 1) % 2], n_keys)


def _mla_attn_lat(q, k, v, gate, ctx_tiles, ctx_out=None):
    bsz, s, _ = q.shape
    tq = TOKEN_TILE
    nq = s // tq - ctx_tiles
    n_tiles = bsz * MLA_PAIRS * nq
    pair_w = 2 * MLA_HEAD_SLOT
    out_lo = 0 if ctx_out is not None else ctx_tiles

    def decode(i):
        return i // (MLA_PAIRS * nq), (i // nq) % MLA_PAIRS, i % nq + ctx_tiles

    def ahead(n):
        return decode(jnp.minimum(n, n_tiles - 1))

    def behind(n):
        return decode(jnp.maximum(n - 1, 0))

    def q_map(n):
        b, j, t = ahead(n)
        return b, t, j

    def k_map(n):
        b, j, _ = ahead(n)
        return b, 0, j

    def v_map(n):
        b, j, _ = behind(n)
        return b, 0, j

    def g_map(n):
        b, j, t = behind(n)
        return b, t, j

    def o_map(n):
        b, j, t = behind(n)
        return b, t - out_lo, j

    in_specs = [pl.BlockSpec((1, tq, pair_w), q_map),
                pl.BlockSpec((1, s, pair_w), k_map),
                pl.BlockSpec((1, s, MLA_V_SLOT), v_map),
                pl.BlockSpec((1, tq, LANES), g_map)]
    args = [q, k, v, gate]
    aliases = {}
    if ctx_out is not None:
        in_specs = [pl.BlockSpec(memory_space=pl.ANY)] + in_specs
        args = [ctx_out] + args
        aliases = {0: 0}
    kern = functools.partial(_mla_attn_lat_kernel, n_tiles=n_tiles, aliased=ctx_out is not None)
    return pl.pallas_call(
        kern,
        out_shape=jax.ShapeDtypeStruct((bsz, s - out_lo * tq, MLA_WIDTH), F32),
        grid=(n_tiles + 1,),
        in_specs=in_specs,
        out_specs=pl.BlockSpec((1, tq, LANES), o_map),
        scratch_shapes=[pltpu.VMEM((2, tq, s), F32), pltpu.VMEM((2, tq, 1), F32),
                        pltpu.VMEM((2, tq, s), F32), pltpu.VMEM((2, tq, 1), F32)],
        input_output_aliases=aliases,
        compiler_params=_params("arbitrary"),
        name="mla_attn",
    )(*args)


def _outproj_kernel(*refs, n_stream, ctx_tiles, t_lo, alpha):
    ssd_ref, mla_ref, ret_ref = refs[:3]
    x_refs = refs[3:3 + n_stream]
    mod_ref, w_ref, g_ref, b_ref, o_ref = refs[3 + n_stream:]
    acc = _dot(ssd_ref[0].astype(BF16), w_ref[:SSD_WIDTH, :])
    acc = acc + _dot(mla_ref[0].astype(BF16), w_ref[SSD_WIDTH:SSD_WIDTH + MLA_WIDTH, :])
    acc = acc + _dot(ret_ref[0].astype(BF16), w_ref[SSD_WIDTH + MLA_WIDTH:, :])
    gate = mod_ref[0, 0][:, 2 * D_MODEL:]
    x = _stream_tile(x_refs, pl.program_id(1) + t_lo, ctx_tiles)
    r = alpha * x + gate * acc
    mu = jnp.mean(r, axis=-1, keepdims=True)
    var = jnp.mean(jnp.square(r - mu), axis=-1, keepdims=True)
    o_ref[0] = (r - mu) * lax.rsqrt(var + LN_EPS) * g_ref[...] + b_ref[...]


def _outproj(ssd_o, mla_o, mla_lo, ret_o, stream, mod_all, w, ln_g, ln_b, t_lo, ctx_tiles, alpha):
    bsz, s, _ = ssd_o.shape
    d = stream[0].shape[-1]
    nt = s // TOKEN_TILE
    tok = lambda wd: pl.BlockSpec((1, TOKEN_TILE, wd), lambda b, t: (b, t + t_lo, 0))
    kern = functools.partial(_outproj_kernel, n_stream=len(stream), ctx_tiles=ctx_tiles, t_lo=t_lo, alpha=alpha)
    return pl.pallas_call(
        kern,
        out_shape=jax.ShapeDtypeStruct((bsz, s - t_lo * TOKEN_TILE, d), F32),
        grid=(bsz, nt - t_lo),
        in_specs=[tok(SSD_WIDTH),
                  pl.BlockSpec((1, TOKEN_TILE, MLA_WIDTH), lambda b, t: (b, t + t_lo - mla_lo, 0)),
                  tok(RET_WIDTH)] + _stream_specs(stream, t_lo, ctx_tiles) + [
                  pl.BlockSpec((1, 1, 1, 3 * d),
                               lambda b, t: (b, jnp.minimum(t + t_lo, ctx_tiles) // ctx_tiles, 0, 0)),
                  pl.BlockSpec(w.shape, lambda b, t: (0, 0)),
                  pl.BlockSpec((1, d), lambda b, t: (0, 0)),
                  pl.BlockSpec((1, d), lambda b, t: (0, 0))],
        out_specs=pl.BlockSpec((1, TOKEN_TILE, d), lambda b, t: (b, t, 0)),
        compiler_params=_params("parallel", "arbitrary"),
        name="out_proj",
    )(ssd_o, mla_o, ret_o, *stream, mod_all, w, ln_g.reshape(1, d), ln_b.reshape(1, d))


def _rope_tables(n_ctx, n_lat, rot_dim, width, lane0, period):
    t = np.arange(n_lat)
    row = (t // GRID_W).astype(np.float32)
    col = (t % GRID_W).astype(np.float32)
    n_freq = rot_dim // 4
    inv = (ROPE_THETA ** (-np.arange(n_freq, dtype=np.float32) / n_freq)).astype(np.float32)
    ang = np.concatenate([row[:, None] * inv, col[:, None] * inv], -1).astype(np.float32)
    cos, sin = np.cos(ang), np.sin(ang)
    half = rot_dim // 2
    lane = np.arange(width) % period - lane0
    in_rot = (lane >= 0) & (lane < rot_dim)
    idx = np.where(in_rot, lane % half, 0)
    first = in_rot & (lane < half)
    second = in_rot & (lane >= half)
    cos_t = np.where(in_rot[None, :], cos[:, idx], 1.0)
    sin_a = np.where(first[None, :], -sin[:, idx], 0.0)
    sin_b = np.where(second[None, :], sin[:, idx], 0.0)
    ctx_pad = lambda a, v: np.concatenate([np.full((n_ctx, width), v), a], axis=0).astype(np.float32)
    return ctx_pad(cos_t, 1.0), ctx_pad(sin_a, 0.0), ctx_pad(sin_b, 0.0)


def _mla_weights(w_uq, w_ukv):
    qd = MLA_NOPE_DIM + MLA_ROPE_DIM
    wq = w_uq.reshape(MLA_Q_LORA, MLA_HEADS, qd)
    wq = jnp.pad(wq, ((0, 0), (0, 0), (0, MLA_HEAD_SLOT - qd))).reshape(MLA_Q_LORA, MLA_HEADS * MLA_HEAD_SLOT)
    wkv = w_ukv.reshape(MLA_KV_LORA, MLA_HEADS, MLA_NOPE_DIM + MLA_V_DIM)
    wk = jnp.pad(wkv[..., :MLA_NOPE_DIM], ((0, 0), (0, 0), (0, MLA_HEAD_SLOT - MLA_NOPE_DIM)))
    wk = wk.reshape(MLA_KV_LORA, MLA_HEADS * MLA_HEAD_SLOT)
    wv = wkv[..., MLA_NOPE_DIM:].reshape(MLA_KV_LORA, MLA_PAIRS, 2 * MLA_V_DIM)
    wv = jnp.pad(wv, ((0, 0), (0, 0), (0, MLA_V_SLOT - 2 * MLA_V_DIM))).reshape(MLA_KV_LORA, MLA_PAIRS * MLA_V_SLOT)
    return wq.astype(BF16), wk.astype(BF16), wv.astype(BF16)


def _lane_pad(v):
    return jnp.pad(v, (0, LANES - v.shape[0])).reshape(1, LANES)


def kernel(x, c, ctx, c_ctx, w_ada, b_ada, w_in, ssd_conv_w, ssd_conv_b, ssd_a_log_f, ssd_a_log_b,
           ssd_dt_bias_f, ssd_dt_bias_b, ssd_d, ssd_norm_w, mla_q_norm, mla_w_uq, mla_kv_norm, mla_w_ukv,
           ret_log_rate_f, ret_log_rate_b, w_out, ln_g, ln_b):
    bsz, n_lat, d = x.shape
    n_ctx = ctx.shape[1]
    depth = w_in.shape[0]
    assert d == D_MODEL and n_ctx == TOKEN_TILE and n_lat % TOKEN_TILE == 0
    alpha = (2 * depth) ** 0.25
    s = n_ctx + n_lat
    nc_ctx = n_ctx // CHUNK
    ctx_tiles = n_ctx // TOKEN_TILE
    nt = s // TOKEN_TILE

    cos_m, sa_m, sb_m = _rope_tables(n_ctx, n_lat, MLA_ROPE_DIM, LANES, MLA_ROPE_LANE, LANES)
    cos_r, sa_r, sb_r = _rope_tables(n_ctx, n_lat, RET_QK_DIM, RET_QK_WIDTH, 0, RET_QK_DIM)

    rows = 16
    cc = jnp.zeros((rows, d), F32).at[:bsz].set(c).at[bsz].set(c_ctx)
    stream = (ctx, x)

    for i in range(depth):
        last = i == depth - 1
        mod = _ada(cc, w_ada[i], b_ada[i])
        mod_all = jnp.stack([jnp.broadcast_to(mod[bsz], (bsz, 3 * d)), mod[:bsz]], axis=1)
        mod_all = mod_all.reshape(bsz, 2, 1, 3 * d)
        z, xbc, small, cq, ckv, mg, rq, rk, rv, rg = _inproj(stream, mod_all, _relayout_w_in(w_in[i]), ctx_tiles)

        both = lambda f, b: jnp.stack([f, b], axis=0)
        head_rows = lambda v: jnp.broadcast_to(v[:, None], (SSD_HEADS, CHUNK))
        alogt = both(head_rows(ssd_a_log_f[i]), head_rows(ssd_a_log_b[i]))
        dtbt = both(head_rows(ssd_dt_bias_f[i]), head_rows(ssd_dt_bias_b[i]))
        dwide = jnp.repeat(ssd_d[i], SSD_HEAD_DIM).reshape(1, SSD_WIDTH)
        rates = both(_lane_pad(ret_log_rate_f[i]), _lane_pad(ret_log_rate_b[i]))
        prep = _mixer_prep(xbc, small, rq, rk, rv, ssd_conv_w[i], ssd_conv_b[i].reshape(1, -1),
                           alogt, dtbt, cos_r, sa_r, sb_r, nc_ctx)
        ssd_o, ret_o = _mixer_scan(prep, z, rg, dwide, ssd_norm_w[i].reshape(1, -1), _ret_consts(rates), nc_ctx)

        wq, wk, wv = _mla_weights(mla_w_uq[i], mla_w_ukv[i])
        q, k, v = _mla_proj(cq, ckv, small, cos_m, sa_m, sb_m, mla_q_norm[i].reshape(1, -1),
                            mla_kv_norm[i].reshape(1, -1), wq, wk, wv)
        mla_lo = ctx_tiles if last else 0
        mla_o = _mla_attn_lat(q, k, v, mg, ctx_tiles, None if last else _mla_attn_ctx(q, k, v, mg, ctx_tiles))

        out = _outproj(ssd_o, mla_o, mla_lo, ret_o, stream, mod_all, w_out[i].astype(BF16), ln_g[i], ln_b[i],
                       ctx_tiles if last else 0, ctx_tiles, alpha)
        stream = (out,)
    return stream[0]
```

```python
import functools
import math

import jax
import jax.numpy as jnp
import numpy as np
from jax import lax
from jax.experimental import pallas as pl
from jax.experimental.pallas import tpu as pltpu

F32 = jnp.float32
BF16 = jnp.bfloat16

LANES = 128
SUBLANES = 8
VMEM_LIMIT_BYTES = 56 * 1024 * 1024

GRID_W = 64
D_MODEL = 1024
SSD_WIDTH = 1024
SSD_HEAD_DIM = 64
SSD_HEADS = 16
SSD_GROUPS = 2
SSD_STATE = 128
SSD_CONV = 5
SSD_CONV_DIM = SSD_WIDTH + 2 * SSD_GROUPS * SSD_STATE
MLA_WIDTH = 512
MLA_V_DIM = 64
MLA_HEADS = 8
MLA_NOPE_DIM = 64
MLA_ROPE_DIM = 32
MLA_Q_LORA = 384
MLA_KV_LORA = 256
MLA_SCALE = (MLA_NOPE_DIM + MLA_ROPE_DIM) ** -0.5
RET_WIDTH = 512
RET_HEADS = 4
RET_V_DIM = 128
RET_QK_DIM = 64
RET_QK_WIDTH = RET_HEADS * RET_QK_DIM
CHUNK = 128
ROPE_THETA = 10000.0
LN_EPS = 1e-5
RMS_EPS = 1e-6
IN_WIDTHS = (SSD_WIDTH, SSD_CONV_DIM, SSD_HEADS, MLA_Q_LORA, MLA_KV_LORA, MLA_ROPE_DIM, MLA_WIDTH,
             RET_QK_WIDTH, RET_QK_WIDTH, RET_WIDTH, RET_WIDTH)

TOKEN_TILE = 256
MLA_HEAD_SLOT = 128
MLA_ROPE_LANE = MLA_NOPE_DIM
MLA_PAIRS = MLA_HEADS // 2
MLA_V_SLOT = 256
MLA_Q_SCALE = MLA_SCALE * math.log2(math.e)
NEG_BIG = -1e30


def _params(*sem):
    return pltpu.CompilerParams(dimension_semantics=sem, vmem_limit_bytes=VMEM_LIMIT_BYTES)


def _silu(v):
    return v * jax.nn.sigmoid(v)


def _softplus(v):
    return jnp.maximum(v, 0.0) + jnp.log1p(jnp.exp(-jnp.abs(v)))


def _dot(a, b):
    return jnp.dot(a, b, preferred_element_type=F32)


def _dot_exact(a, b):
    return jnp.dot(a, b, preferred_element_type=F32, precision=lax.Precision.HIGHEST)


def _dot_nt(a, b):
    return lax.dot_general(a, b, (((1,), (1,)), ((), ())), preferred_element_type=F32)


def _ada_kernel(c_ref, w_ref, b_ref, o_ref):
    o_ref[...] = _dot_exact(_silu(c_ref[...]), w_ref[...]) + b_ref[...]


def _ada(cc, w, b):
    rows, d = cc.shape
    n = w.shape[1]
    tn = 1024
    return pl.pallas_call(
        _ada_kernel,
        out_shape=jax.ShapeDtypeStruct((rows, n), F32),
        grid=(n // tn,),
        in_specs=[pl.BlockSpec((rows, d), lambda j: (0, 0)),
                  pl.BlockSpec((d, tn), lambda j: (0, j)),
                  pl.BlockSpec((1, tn), lambda j: (0, j))],
        out_specs=pl.BlockSpec((rows, tn), lambda j: (0, j)),
        compiler_params=_params("arbitrary"),
        name="ada_mod",
    )(cc, w, b.reshape(1, n))


def _chunk_index(d, c, nc_ctx, nc):
    bwd = jnp.where(c < nc_ctx, nc_ctx - 1 - c, nc + nc_ctx - 1 - c)
    return jnp.where(d == 0, c, bwd)


def _tri_masks(d):
    ii = lax.broadcasted_iota(jnp.int32, (CHUNK, CHUNK), 0)
    jj = lax.broadcasted_iota(jnp.int32, (CHUNK, CHUNK), 1)
    sgn = 1 - 2 * d
    mask = (ii - jj) * sgn >= 0
    mask_t = (jj - ii) * sgn >= 0
    return mask, mask_t


def _rope3(v, cos_t, sin_a, sin_b, shift):
    width = v.shape[-1]
    return v * cos_t + pltpu.roll(v, width - shift, 1) * sin_a + pltpu.roll(v, shift, 1) * sin_b


PROJ_WIDTHS = (SSD_WIDTH, SSD_CONV_DIM, LANES, MLA_Q_LORA, MLA_KV_LORA, MLA_WIDTH,
               RET_QK_WIDTH, RET_QK_WIDTH, RET_WIDTH, RET_WIDTH)
PROJ_OFFSETS = tuple(int(o) for o in np.cumsum((0,) + PROJ_WIDTHS[:-1]))
DT_LANE = 0
KROPE_LANE = MLA_ROPE_LANE
ROW_CUM, ROW_DT, ROW_WEND, ROW_ETOT = 0, 16, 32, 48
ROW_PACK = 64
TILE_CHUNKS = TOKEN_TILE // CHUNK


def _relayout_w_in(w_in):
    splits = [int(s) for s in np.cumsum(IN_WIDTHS)[:-1]]
    z, xbc, dt, cq, ckv, kr, mg, rq, rk, rv, rg = jnp.split(w_in, splits, axis=-1)
    d = w_in.shape[0]
    small = jnp.concatenate([dt, jnp.zeros((d, KROPE_LANE - SSD_HEADS), w_in.dtype), kr,
                             jnp.zeros((d, LANES - KROPE_LANE - MLA_ROPE_DIM), w_in.dtype)], axis=-1)
    return jnp.concatenate([z, xbc, small, cq, ckv, mg, rq, rk, rv, rg], axis=-1).astype(BF16)


def _stream_specs(stream, t_lo, ctx_tiles):
    d = stream[0].shape[-1]
    if len(stream) == 1:
        return [pl.BlockSpec((1, TOKEN_TILE, d), lambda b, t: (b, t + t_lo, 0))]
    return [pl.BlockSpec((1, TOKEN_TILE, d), lambda b, t: (b, jnp.minimum(t + t_lo, ctx_tiles - 1), 0)),
            pl.BlockSpec((1, TOKEN_TILE, d), lambda b, t: (b, jnp.maximum(t + t_lo - ctx_tiles, 0), 0))]


def _halo_specs(stream, ctx_tiles):
    arr = stream[-1]
    d = arr.shape[-1]
    shift = 0 if len(stream) == 1 else ctx_tiles
    per_tile = TOKEN_TILE // SUBLANES
    nblk = arr.shape[1] // SUBLANES
    prev = lambda b, t: (b, jnp.clip((t - shift) * per_tile - 1, 0, nblk - 1), 0)
    nxt = lambda b, t: (b, jnp.clip((t - shift + 1) * per_tile, 0, nblk - 1), 0)
    return [pl.BlockSpec((1, SUBLANES, d), prev), pl.BlockSpec((1, SUBLANES, d), nxt)]


def _stream_tile(refs, t, ctx_tiles):
    if len(refs) == 1:
        return refs[0][0]
    return jnp.where(t < ctx_tiles, refs[0][0], refs[1][0])


def _inproj_kernel(*refs, n_stream, ctx_tiles, nt):
    x_refs = refs[:n_stream]
    (prev_ref, next_ref, mod_ref, w_ref, convw_ref, convb_ref, alogt_ref, dtbt_ref,
     cos_ref, sa_ref, sb_ref) = refs[n_stream:n_stream + 11]
    (zs_ref, xs_ref, xst_ref, bc_ref, colp_ref, rowp_ref, small_ref, cq_ref, ckv_ref, mgs_ref,
     q_ref, kbf_ref, kt_ref, v_ref, rgs_ref) = refs[n_stream + 11:]
    t = pl.program_id(1)
    mod = mod_ref[0, 0]
    sh = mod[:, :D_MODEL]
    sc = mod[:, D_MODEL:2 * D_MODEL]
    modulate = lambda v: (v * (1.0 + sc) + sh).astype(BF16)
    h = modulate(_stream_tile(x_refs, t, ctx_tiles))
    h_ext = jnp.concatenate([modulate(prev_ref[0]), h, modulate(next_ref[0])], axis=0)

    def proj(g, lo=0, width=None, lhs=None):
        off = PROJ_OFFSETS[g] + lo
        width = PROJ_WIDTHS[g] - lo if width is None else width
        return _dot(h if lhs is None else lhs, w_ref[:, off:off + width])

    first = jnp.logical_or(t == 0, t == ctx_tiles)
    last = jnp.logical_or(t == ctx_tiles - 1, t == nt - 1)
    row = lax.broadcasted_iota(jnp.int32, (TOKEN_TILE + 2 * SUBLANES, 1), 0)
    keep = jnp.where(jnp.logical_or(jnp.logical_and(row < SUBLANES, first),
                                    jnp.logical_and(row >= SUBLANES + TOKEN_TILE, last)), 0.0, 1.0)
    unit_w = 2 * LANES
    ext_parts = {}

    def ext_unit(e):
        ext_parts[e] = proj(1, e * unit_w, unit_w, lhs=h_ext) * keep

    def conv_strip(j, anchor):
        pad = SSD_CONV // 2
        sl = slice(j * LANES, (j + 1) * LANES)
        esl = slice((j % 2) * LANES, (j % 2 + 1) * LANES)
        ext = ext_parts[j // 2]
        acc = convb_ref[:, sl] + anchor
        for k in range(SSD_CONV):
            lo = SUBLANES - pad + k
            acc = acc + convw_ref[k:k + 1, sl] * ext[lo:lo + TOKEN_TILE, esl]
        u = _silu(acc)
        if j < SSD_WIDTH // LANES:
            xs_ref[0, :, sl] = u.astype(BF16)
            xst_ref[0, sl, :] = u.T.astype(BF16)
        else:
            bc_ref[0, :, j * LANES - SSD_WIDTH:(j + 1) * LANES - SSD_WIDTH] = u

    def gate_unit(g, ref, lo, width):
        def run():
            val = proj(g, lo, width)
            ref[0, :, lo:lo + width] = _silu(val)
            return val
        return run

    def plain_unit(g, ref, cast=None):
        def run():
            val = proj(g)
            ref[0] = val if cast is None else val.astype(cast)
            return val
        return run

    def ret_q_unit():
        val = proj(6)
        q_ref[0] = _rope3(val, cos_ref[...], sa_ref[...], sb_ref[...], RET_QK_DIM // 2)
        return val

    def ret_k_unit():
        val = proj(7)
        kk = _rope3(val, cos_ref[...], sa_ref[...], sb_ref[...], RET_QK_DIM // 2) * (RET_QK_DIM ** -0.5)
        kbf_ref[0] = kk.astype(BF16)
        kt_ref[0] = kk.T
        return val

    units = [gate_unit(0, zs_ref, lo, unit_w) for lo in range(0, SSD_WIDTH, unit_w)]
    units += [plain_unit(3, cq_ref), plain_unit(4, ckv_ref)]
    units += [gate_unit(5, mgs_ref, lo, unit_w) for lo in range(0, MLA_WIDTH, unit_w)]
    units += [ret_q_unit, ret_k_unit, plain_unit(8, v_ref, BF16)]
    units += [gate_unit(9, rgs_ref, lo, unit_w) for lo in range(0, RET_WIDTH, unit_w)]
    small = proj(2)
    small_ref[0] = small
    n_strips = SSD_CONV_DIM // LANES
    ext_unit(0)
    anchor = jnp.zeros((1, LANES), F32)
    for j in range(n_strips):
        if j % 2 == 0 and j // 2 + 1 < n_strips // 2:
            ext_unit(j // 2 + 1)
        nxt = units.pop(0)()[0:1, :LANES] * 0.0 if units else anchor
        conv_strip(j, anchor)
        anchor = nxt
    for unit in units:
        unit()

    dt_raw_t = small.T[DT_LANE:DT_LANE + SSD_HEADS, :]
    pad_rows = jnp.zeros((CHUNK - SSD_HEADS, CHUNK), F32)
    for d in range(2):
        _, mask_t = _tri_masks(d)
        tri_t = jnp.where(mask_t, 1.0, 0.0)
        dt_all = _softplus(dt_raw_t + dtbt_ref[d])
        la_all = dt_all * -jnp.exp(alogt_ref[d])
        base = d * ROW_PACK
        for j in range(TILE_CHUNKS):
            cs = slice(j * CHUNK, (j + 1) * CHUNK)
            dt_row, la_row = dt_all[:, cs], la_all[:, cs]
            cum_row = _dot_exact(la_row, tri_t)
            colp_ref[0, d, cs, :] = jnp.concatenate([cum_row, pad_rows], axis=0).T
            tot_row = jnp.sum(la_row, axis=1, keepdims=True)
            rowp_ref[0, j, base + ROW_CUM:base + ROW_CUM + SSD_HEADS, :] = cum_row
            rowp_ref[0, j, base + ROW_DT:base + ROW_DT + SSD_HEADS, :] = dt_row
            rowp_ref[0, j, base + ROW_WEND:base + ROW_WEND + SSD_HEADS, :] = dt_row * jnp.exp(tot_row - cum_row)
            rowp_ref[0, j, base + ROW_ETOT:base + ROW_ETOT + SSD_HEADS, :] = jnp.broadcast_to(
                jnp.exp(tot_row), (SSD_HEADS, CHUNK))


def _inproj(stream, mod_all, w, conv_w, conv_b, alogt, dtbt, cos_t, sin_a, sin_b, ctx_tiles):
    bsz, d = stream[0].shape[0], stream[0].shape[-1]
    s = sum(a.shape[1] for a in stream)
    nt = s // TOKEN_TILE
    nc = s // CHUNK
    tok = lambda wd: pl.BlockSpec((1, TOKEN_TILE, wd), lambda b, t: (b, t, 0))
    chan = lambda wd: pl.BlockSpec((1, wd, TOKEN_TILE), lambda b, t: (b, 0, t))
    full = lambda a: pl.BlockSpec(a.shape, lambda b, t: (0,) * a.ndim)
    tab = pl.BlockSpec((TOKEN_TILE, RET_QK_WIDTH), lambda b, t: (t, 0))
    bcw = SSD_CONV_DIM - SSD_WIDTH
    outs = [((bsz, s, SSD_WIDTH), F32, tok(SSD_WIDTH)),
            ((bsz, s, SSD_WIDTH), BF16, tok(SSD_WIDTH)),
            ((bsz, SSD_WIDTH, s), BF16, chan(SSD_WIDTH)),
            ((bsz, s, bcw), F32, tok(bcw)),
            ((bsz, 2, s, LANES), F32, pl.BlockSpec((1, 2, TOKEN_TILE, LANES), lambda b, t: (b, 0, t, 0))),
            ((bsz, nc, 2 * ROW_PACK, CHUNK), F32,
             pl.BlockSpec((1, TILE_CHUNKS, 2 * ROW_PACK, CHUNK), lambda b, t: (b, t, 0, 0))),
            ((bsz, s, LANES), F32, tok(LANES)),
            ((bsz, s, MLA_Q_LORA), F32, tok(MLA_Q_LORA)),
            ((bsz, s, MLA_KV_LORA), F32, tok(MLA_KV_LORA)),
            ((bsz, s, MLA_WIDTH), F32, tok(MLA_WIDTH)),
            ((bsz, s, RET_QK_WIDTH), F32, tok(RET_QK_WIDTH)),
            ((bsz, s, RET_QK_WIDTH), BF16, tok(RET_QK_WIDTH)),
            ((bsz, RET_QK_WIDTH, s), F32, chan(RET_QK_WIDTH)),
            ((bsz, s, RET_WIDTH), BF16, tok(RET_WIDTH)),
            ((bsz, s, RET_WIDTH), F32, tok(RET_WIDTH))]
    kern = functools.partial(_inproj_kernel, n_stream=len(stream), ctx_tiles=ctx_tiles, nt=nt)
    return pl.pallas_call(
        kern,
        out_shape=[jax.ShapeDtypeStruct(shape, dt) for shape, dt, _ in outs],
        grid=(bsz, nt),
        in_specs=_stream_specs(stream, 0, ctx_tiles) + _halo_specs(stream, ctx_tiles) + [
            pl.BlockSpec((1, 1, 1, 3 * d), lambda b, t: (b, jnp.minimum(t, ctx_tiles) // ctx_tiles, 0, 0)),
            pl.BlockSpec(w.shape, lambda b, t: (0, 0), pipeline_mode=pl.Buffered(1)),
            full(conv_w), full(conv_b), full(alogt), full(dtbt), tab, tab, tab],
        out_specs=[spec for _, _, spec in outs],
        compiler_params=_params("parallel", "arbitrary"),
        name="in_proj",
    )(*stream, stream[-1], stream[-1], mod_all, w, conv_w, conv_b, alogt, dtbt, cos_t, sin_a, sin_b)


def _ret_consts_kernel(rate_ref, dec_ref, ecum_ref, wend_ref, etot_ref):
    d = pl.program_id(0)
    mask, _ = _tri_masks(d)
    ii = lax.broadcasted_iota(jnp.int32, (CHUNK, CHUNK), 0)
    jj = lax.broadcasted_iota(jnp.int32, (CHUNK, CHUNK), 1)
    dist = jnp.abs(ii - jj).astype(F32)
    pos_i = lax.broadcasted_iota(jnp.int32, (CHUNK, RET_QK_DIM), 0)
    steps_col = jnp.where(d == 0, pos_i + 1, CHUNK - pos_i).astype(F32)
    pos_j = lax.broadcasted_iota(jnp.int32, (RET_QK_DIM, CHUNK), 1)
    left_row = jnp.where(d == 0, CHUNK - 1 - pos_j, pos_j).astype(F32)
    for hd in range(RET_HEADS):
        la = -jnp.exp(rate_ref[0, :, hd:hd + 1])
        dec_ref[0, hd] = jnp.exp(jnp.where(mask, dist * la, NEG_BIG))
        ecum_ref[0, :, hd * RET_QK_DIM:(hd + 1) * RET_QK_DIM] = jnp.exp(steps_col * la)
        wend_ref[0, hd * RET_QK_DIM:(hd + 1) * RET_QK_DIM, :] = jnp.exp(left_row * la)
        etot_ref[0, hd] = jnp.broadcast_to(jnp.exp(CHUNK * la), (1, LANES))


def _ret_consts(rates):
    return pl.pallas_call(
        _ret_consts_kernel,
        out_shape=[jax.ShapeDtypeStruct((2, RET_HEADS, CHUNK, CHUNK), F32),
                   jax.ShapeDtypeStruct((2, CHUNK, RET_QK_WIDTH), F32),
                   jax.ShapeDtypeStruct((2, RET_QK_WIDTH, CHUNK), F32),
                   jax.ShapeDtypeStruct((2, RET_HEADS, 1, LANES), F32)],
        grid=(2,),
        in_specs=[pl.BlockSpec((1, 1, LANES), lambda d: (d, 0, 0))],
        out_specs=[pl.BlockSpec((1, RET_HEADS, CHUNK, CHUNK), lambda d: (d, 0, 0, 0)),
                   pl.BlockSpec((1, CHUNK, RET_QK_WIDTH), lambda d: (d, 0, 0)),
                   pl.BlockSpec((1, RET_QK_WIDTH, CHUNK), lambda d: (d, 0, 0)),
                   pl.BlockSpec((1, RET_HEADS, 1, LANES), lambda d: (d, 0, 0, 0))],
        compiler_params=_params("arbitrary"),
        name="ret_consts",
    )(rates)


def _mixer_scan_kernel(xs_ref, xst_ref, bc_ref, colp_ref, rowp_ref, z_ref, dwide_ref, normw_ref,
                       q_ref, kbf_ref, kt_ref, v_ref, g_ref, dec_ref, ecum_ref, wend_ref, etot_ref,
                       ssd_out_ref, ret_out_ref, yf_ref, yr_ref, ht_ref, hr_ref, *, nc_ctx, nc):
    d = pl.program_id(1)
    c = pl.program_id(2)
    cidx = _chunk_index(d, c, nc_ctx, nc)

    @pl.when(c == 0)
    def _():
        ht_ref[...] = jnp.zeros_like(ht_ref)
        hr_ref[...] = jnp.zeros_like(hr_ref)

    mask, _ = _tri_masks(d)

    q = q_ref[0]
    q_bf = q.astype(BF16)
    k_bf = kbf_ref[0]
    qe_bf = (q * ecum_ref[0]).astype(BF16)
    ktw_bf = (kt_ref[0] * wend_ref[0]).astype(BF16)
    v_bf = v_ref[0]
    yrs = []
    ret_lhs = {}

    def ret_scores(hd):
        qs = slice(hd * RET_QK_DIM, (hd + 1) * RET_QK_DIM)
        sc = _dot_nt(q_bf[:, qs], k_bf[:, qs]) * dec_ref[0, hd]
        ret_lhs[hd] = jnp.concatenate([sc.astype(BF16), qe_bf[:, qs]], axis=1)

    def ret_update(hd):
        qs = slice(hd * RET_QK_DIM, (hd + 1) * RET_QK_DIM)
        v_h = v_bf[:, hd * RET_V_DIM:(hd + 1) * RET_V_DIM]
        h_h = hr_ref[hd]
        rhs = jnp.concatenate([v_h, h_h.astype(BF16)], axis=0)
        yrs.append(_dot(ret_lhs[hd], rhs))
        hr_ref[hd] = etot_ref[0, hd] * h_h + _dot(ktw_bf[qs, :], v_h)

    xs_bf = xs_ref[0]
    bc = bc_ref[0]
    cum_col = colp_ref[0, 0]
    rowp = rowp_ref[0, 0]
    cum_row = rowp[ROW_CUM:ROW_CUM + SSD_HEADS]
    dt_row = rowp[ROW_DT:ROW_DT + SSD_HEADS]
    wend_row = rowp[ROW_WEND:ROW_WEND + SSD_HEADS]
    etot_rowp = rowp[ROW_ETOT:ROW_ETOT + SSD_HEADS]
    lane = lax.broadcasted_iota(jnp.int32, (CHUNK, LANES), 1)
    sub = lax.broadcasted_iota(jnp.int32, (CHUNK, LANES), 0)
    lane_lo = lane < SSD_HEAD_DIM
    sub_lo = sub < SSD_HEAD_DIM
    heads_per_group = SSD_HEADS // SSD_GROUPS
    ys = []
    gmats = {}
    for k in range(SSD_HEADS // 2):
        g = (2 * k) // heads_per_group
        if g not in gmats:
            b_g = bc[:, g * SSD_STATE:(g + 1) * SSD_STATE].astype(BF16)
            c_g = bc[:, (SSD_GROUPS + g) * SSD_STATE:(SSD_GROUPS + g + 1) * SSD_STATE]
            gmats[g] = (_dot_nt(c_g.astype(BF16), b_g), c_g, b_g)
        scores, c_g, b_g = gmats[g]
        ha, hb = 2 * k, 2 * k + 1
        xs_pair = xs_bf[:, k * LANES:(k + 1) * LANES]
        ht_pair = ht_ref[k]
        rhs = jnp.concatenate([xs_pair, ht_pair.T.astype(BF16)], axis=0)
        y_heads = []
        for hd in (ha, hb):
            cb = jnp.broadcast_to(cum_col[:, hd:hd + 1], (CHUNK, CHUNK))
            seg = cb - cum_row[hd:hd + 1, :]
            dec = jnp.exp(jnp.where(mask, seg, NEG_BIG))
            m = scores * dec * dt_row[hd:hd + 1, :]
            lhs = jnp.concatenate([m, c_g * jnp.exp(cb)], axis=1).astype(BF16)
            y_heads.append(_dot(lhs, rhs))
        ys.append(jnp.where(lane_lo, y_heads[0], y_heads[1]))
        w_pair = jnp.where(sub_lo, wend_row[ha:ha + 1, :], wend_row[hb:hb + 1, :])
        e_pair = jnp.where(sub_lo, etot_rowp[ha:ha + 1, :], etot_rowp[hb:hb + 1, :])
        xw = (xst_ref[0, k * LANES:(k + 1) * LANES, :].astype(F32) * w_pair).astype(BF16)
        ht_ref[k] = e_pair * ht_pair + _dot(xw, b_g)
        if k % 2 == 0:
            ret_scores(k // 2)
        else:
            ret_update(k // 2)
    y = jnp.concatenate(ys, axis=1)
    yr = jnp.concatenate(yrs, axis=1)

    @pl.when(d == 0)
    def _():
        yf_ref[cidx] = y
        yr_ref[cidx] = yr

    @pl.when(d == 1)
    def _():
        tot = yf_ref[cidx] + y + xs_bf.astype(F32) * dwide_ref[...]
        gated = tot * z_ref[0]
        ms = jnp.mean(gated * gated, axis=-1, keepdims=True)
        ssd_out_ref[0] = gated * lax.rsqrt(ms + RMS_EPS) * normw_ref[...]
        rtot = yr_ref[cidx] + yr
        outs = []
        for hd in range(RET_HEADS):
            t_h = rtot[:, hd * RET_V_DIM:(hd + 1) * RET_V_DIM]
            mu = jnp.mean(t_h, axis=-1, keepdims=True)
            var = jnp.mean(jnp.square(t_h - mu), axis=-1, keepdims=True)
            outs.append((t_h - mu) * lax.rsqrt(var + LN_EPS))
        ret_out_ref[0] = jnp.concatenate(outs, axis=1) * g_ref[0]


def _mixer_scan(prep, z, rg, dwide, normw, rconsts, nc_ctx):
    xs_bf, xst_bf, bc, colp, rowp, rq, rk_bf, rkt, rv_bf = prep
    dec, ecum, wend, etot = rconsts
    bsz, s, _ = z.shape
    nc = s // CHUNK
    first_bwd = nc_ctx - 1

    def cix(d, c):
        return _chunk_index(d, c, nc_ctx, nc)

    def late(d, c):
        return jnp.where(d == 0, first_bwd, cix(d, c))

    tokc = lambda wd: pl.BlockSpec((1, CHUNK, wd), lambda b, d, c: (b, cix(d, c), 0))
    tokl = lambda wd: pl.BlockSpec((1, CHUNK, wd), lambda b, d, c: (b, late(d, c), 0))
    chan = lambda wd: pl.BlockSpec((1, wd, CHUNK), lambda b, d, c: (b, 0, cix(d, c)))
    row1 = lambda wd: pl.BlockSpec((1, wd), lambda b, d, c: (0, 0))
    kern = functools.partial(_mixer_scan_kernel, nc_ctx=nc_ctx, nc=nc)
    return pl.pallas_call(
        kern,
        out_shape=[jax.ShapeDtypeStruct((bsz, s, SSD_WIDTH), F32),
                   jax.ShapeDtypeStruct((bsz, s, RET_WIDTH), F32)],
        grid=(bsz, 2, nc),
        in_specs=[tokc(SSD_WIDTH), chan(SSD_WIDTH), tokc(bc.shape[-1]),
                  pl.BlockSpec((1, 1, CHUNK, LANES), lambda b, d, c: (b, d, cix(d, c), 0)),
                  pl.BlockSpec((1, 1, ROW_PACK, CHUNK), lambda b, d, c: (b, cix(d, c), d, 0)),
                  tokl(SSD_WIDTH), row1(SSD_WIDTH), row1(SSD_WIDTH),
                  tokc(RET_QK_WIDTH), tokc(RET_QK_WIDTH), chan(RET_QK_WIDTH), tokc(RET_WIDTH), tokl(RET_WIDTH),
                  pl.BlockSpec((1, RET_HEADS, CHUNK, CHUNK), lambda b, d, c: (d, 0, 0, 0)),
                  pl.BlockSpec((1, CHUNK, RET_QK_WIDTH), lambda b, d, c: (d, 0, 0)),
                  pl.BlockSpec((1, RET_QK_WIDTH, CHUNK), lambda b, d, c: (d, 0, 0)),
                  pl.BlockSpec((1, RET_HEADS, 1, LANES), lambda b, d, c: (d, 0, 0, 0))],
        out_specs=[tokl(SSD_WIDTH), tokl(RET_WIDTH)],
        scratch_shapes=[pltpu.VMEM((nc, CHUNK, SSD_WIDTH), F32),
                        pltpu.VMEM((nc, CHUNK, RET_WIDTH), F32),
                        pltpu.VMEM((SSD_HEADS // 2, LANES, SSD_STATE), F32),
                        pltpu.VMEM((RET_HEADS, RET_QK_DIM, RET_V_DIM), F32)],
        compiler_params=_params("parallel", "arbitrary", "arbitrary"),
        name="mixer_scan",
    )(xs_bf, xst_bf, bc, colp, rowp, z, dwide, normw, rq, rk_bf, rkt, rv_bf, rg, dec, ecum, wend, etot)


def _rms(v, g):
    return v * lax.rsqrt(jnp.mean(v * v, axis=-1, keepdims=True) + RMS_EPS) * g


def _mla_proj_kernel(cq_ref, ckv_ref, small_ref, cos_ref, sa_ref, sb_ref, qn_ref, kvn_ref,
                     wq_ref, wk_ref, wv_ref, q_ref, k_ref, v_ref):
    cqn = _rms(cq_ref[0], qn_ref[...]).astype(BF16)
    ckvn = _rms(ckv_ref[0], kvn_ref[...]).astype(BF16)
    q = _dot(cqn, wq_ref[...])
    kn = _dot(ckvn, wk_ref[...])
    vlane = lax.broadcasted_iota(jnp.int32, (1, MLA_PAIRS * MLA_V_SLOT), 1)
    ones = jnp.where(vlane % MLA_V_SLOT >= 2 * MLA_V_DIM, 1.0, 0.0)
    v_ref[0] = (_dot(ckvn, wv_ref[...]) + ones).astype(BF16)
    cos_t, sin_a, sin_b = cos_ref[...], sa_ref[...], sb_ref[...]
    lane = lax.broadcasted_iota(jnp.int32, small_ref.shape[1:], 1)
    in_rope = jnp.logical_and(lane >= KROPE_LANE, lane < KROPE_LANE + MLA_ROPE_DIM)
    kr = _rope3(jnp.where(in_rope, small_ref[0], 0.0), cos_t, sin_a, sin_b, MLA_ROPE_DIM // 2)
    for hd in range(MLA_HEADS):
        sl = slice(hd * MLA_HEAD_SLOT, (hd + 1) * MLA_HEAD_SLOT)
        q_ref[0, :, sl] = (_rope3(q[:, sl], cos_t, sin_a, sin_b, MLA_ROPE_DIM // 2) * MLA_Q_SCALE).astype(BF16)
        k_ref[0, :, sl] = (kn[:, sl] + kr).astype(BF16)


def _mla_proj(cq, ckv, small, cos_t, sin_a, sin_b, qn, kvn, wq, wk, wv):
    bsz, s, _ = cq.shape
    nt = s // TOKEN_TILE
    hw = MLA_HEADS * MLA_HEAD_SLOT
    vw = MLA_PAIRS * MLA_V_SLOT
    tok = lambda wd: pl.BlockSpec((1, TOKEN_TILE, wd), lambda b, t: (b, t, 0))
    tab = pl.BlockSpec((TOKEN_TILE, LANES), lambda b, t: (t, 0))
    full = lambda a: pl.BlockSpec(a.shape, lambda b, t: (0,) * a.ndim)
    return pl.pallas_call(
        _mla_proj_kernel,
        out_shape=[jax.ShapeDtypeStruct((bsz, s, hw), BF16),
                   jax.ShapeDtypeStruct((bsz, s, hw), BF16),
                   jax.ShapeDtypeStruct((bsz, s, vw), BF16)],
        grid=(bsz, nt),
        in_specs=[tok(MLA_Q_LORA), tok(MLA_KV_LORA), tok(LANES), tab, tab, tab,
                  full(qn), full(kvn), full(wq), full(wk), full(wv)],
        out_specs=[tok(hw), tok(hw), tok(vw)],
        compiler_params=_params("parallel", "arbitrary"),
        name="mla_proj",
    )(cq, ckv, small, cos_t, sin_a, sin_b, qn, kvn, wq, wk, wv)


ATTN_KEY_CHUNK = 256


def _attn_step(q_ref, k_ref, v_ref, g_ref, o_ref, slot_a, slot_b, n_keys):
    heads = range(2)
    hsl = [slice(hd * MLA_HEAD_SLOT, (hd + 1) * MLA_HEAD_SLOT) for hd in heads]
    if slot_a is not None:
        q = [q_ref[0, :, hsl[hd]] for hd in heads]
        mcol = [None, None]
    if slot_b is not None:
        mrow = [slot_b[1][hd] for hd in heads]
        pv = [None, None]
    for c in range(n_keys // ATTN_KEY_CHUNK):
        ks = slice(c * ATTN_KEY_CHUNK, (c + 1) * ATTN_KEY_CHUNK)
        for hd in heads:
            if slot_a is not None:
                s = _dot_nt(q[hd], k_ref[0, ks, hsl[hd]])
                slot_a[0][hd, :, ks] = s
                cm = s[:, :LANES]
                for w in range(1, ATTN_KEY_CHUNK // LANES):
                    cm = jnp.maximum(cm, s[:, w * LANES:(w + 1) * LANES])
                mcol[hd] = cm if c == 0 else jnp.maximum(mcol[hd], cm)
            if slot_b is not None:
                p = jnp.exp2(slot_b[0][hd, :, ks] - mrow[hd]).astype(BF16)
                part = _dot(p, v_ref[0, ks, :])
                pv[hd] = part if c == 0 else pv[hd] + part
    if slot_a is not None:
        for hd in heads:
            slot_a[1][hd] = jnp.max(mcol[hd], axis=-1, keepdims=True)
    if slot_b is not None:
        outs = [pv[hd][:, :LANES] / pv[hd][:, LANES:] for hd in heads]
        lane = lax.broadcasted_iota(jnp.int32, outs[0].shape, 1)
        o = jnp.where(lane < MLA_V_DIM, outs[0], outs[1])
        o_ref[0] = o * g_ref[0]


def _mla_attn_ctx_kernel(q_ref, k_ref, v_ref, g_ref, o_ref, s_ref, m_ref):
    n_keys = k_ref.shape[1]
    _attn_step(q_ref, k_ref, v_ref, g_ref, o_ref, (s_ref, m_ref), None, n_keys)
    _attn_step(q_ref, k_ref, v_ref, g_ref, o_ref, None, (s_ref, m_ref), n_keys)


def _mla_attn_ctx(q, k, v, gate, ctx_tiles):
    bsz, s, _ = q.shape
    tq = TOKEN_TILE
    n_ctx = ctx_tiles * tq
    pair_w = 2 * MLA_HEAD_SLOT
    return pl.pallas_call(
        _mla_attn_ctx_kernel,
        out_shape=jax.ShapeDtypeStruct((bsz, s, MLA_WIDTH), F32),
        grid=(bsz, MLA_PAIRS, ctx_tiles),
        in_specs=[pl.BlockSpec((1, tq, pair_w), lambda b, j, t: (b, t, j)),
                  pl.BlockSpec((1, n_ctx, pair_w), lambda b, j, t: (b, 0, j)),
                  pl.BlockSpec((1, n_ctx, MLA_V_SLOT), lambda b, j, t: (b, 0, j)),
                  pl.BlockSpec((1, tq, LANES), lambda b, j, t: (b, t, j))],
        out_specs=pl.BlockSpec((1, tq, LANES), lambda b, j, t: (b, t, j)),
        scratch_shapes=[pltpu.VMEM((2, tq, n_ctx), F32), pltpu.VMEM((2, tq, 1), F32)],
        compiler_params=_params("parallel", "arbitrary", "arbitrary"),
        name="mla_attn_ctx",
    )(q, k, v, gate)


def _mla_attn_lat_kernel(*refs, n_tiles, aliased):
    if aliased:
        refs = refs[1:]
    q_ref, k_ref, v_ref, g_ref, o_ref, s0, m0, s1, m1 = refs
    n_keys = k_ref.shape[1]
    slots = ((s0, m0), (s1, m1))
    n = pl.program_id(0)

    @pl.when(n == 0)
    def _():
        _attn_step(q_ref, k_ref, v_ref, g_ref, o_ref, slots[0], None, n_keys)

    for par in range(2):
        @pl.when(jnp.logical_and(jnp.logical_and(n > 0, n < n_tiles), n % 2 == par))
        def _():
            _attn_step(q_ref, k_ref, v_ref, g_ref, o_ref, slots[par], slots[1 - par], n_keys)

    @pl.when(n == n_tiles)
    def _():
        _attn_step(q_ref, k_ref, v_ref, g_ref, o_ref, None, slots[(n_tiles - 1) % 2], n_keys)


def _mla_attn_lat(q, k, v, gate, ctx_tiles, ctx_out=None):
    bsz, s, _ = q.shape
    tq = TOKEN_TILE
    nq = s // tq - ctx_tiles
    n_tiles = bsz * MLA_PAIRS * nq
    pair_w = 2 * MLA_HEAD_SLOT
    out_lo = 0 if ctx_out is not None else ctx_tiles

    def decode(i):
        return i // (MLA_PAIRS * nq), (i // nq) % MLA_PAIRS, i % nq + ctx_tiles

    def ahead(n):
        return decode(jnp.minimum(n, n_tiles - 1))

    def behind(n):
        return decode(jnp.maximum(n - 1, 0))

    def q_map(n):
        b, j, t = ahead(n)
        return b, t, j

    def k_map(n):
        b, j, _ = ahead(n)
        return b, 0, j

    def v_map(n):
        b, j, _ = behind(n)
        return b, 0, j

    def g_map(n):
        b, j, t = behind(n)
        return b, t, j

    def o_map(n):
        b, j, t = behind(n)
        return b, t - out_lo, j

    in_specs = [pl.BlockSpec((1, tq, pair_w), q_map),
                pl.BlockSpec((1, s, pair_w), k_map),
                pl.BlockSpec((1, s, MLA_V_SLOT), v_map),
                pl.BlockSpec((1, tq, LANES), g_map)]
    args = [q, k, v, gate]
    aliases = {}
    if ctx_out is not None:
        in_specs = [pl.BlockSpec(memory_space=pl.ANY)] + in_specs
        args = [ctx_out] + args
        aliases = {0: 0}
    kern = functools.partial(_mla_attn_lat_kernel, n_tiles=n_tiles, aliased=ctx_out is not None)
    return pl.pallas_call(
        kern,
        out_shape=jax.ShapeDtypeStruct((bsz, s - out_lo * tq, MLA_WIDTH), F32),
        grid=(n_tiles + 1,),
        in_specs=in_specs,
        out_specs=pl.BlockSpec((1, tq, LANES), o_map),
        scratch_shapes=[pltpu.VMEM((2, tq, s), F32), pltpu.VMEM((2, tq, 1), F32),
                        pltpu.VMEM((2, tq, s), F32), pltpu.VMEM((2, tq, 1), F32)],
        input_output_aliases=aliases,
        compiler_params=_params("arbitrary"),
        name="mla_attn",
    )(*args)


def _outproj_kernel(*refs, n_stream, ctx_tiles, t_lo, alpha):
    ssd_ref, mla_ref, ret_ref = refs[:3]
    x_refs = refs[3:3 + n_stream]
    mod_ref, w_ref, g_ref, b_ref, o_ref = refs[3 + n_stream:]
    acc = _dot(ssd_ref[0].astype(BF16), w_ref[:SSD_WIDTH, :])
    acc = acc + _dot(mla_ref[0].astype(BF16), w_ref[SSD_WIDTH:SSD_WIDTH + MLA_WIDTH, :])
    acc = acc + _dot(ret_ref[0].astype(BF16), w_ref[SSD_WIDTH + MLA_WIDTH:, :])
    gate = mod_ref[0, 0][:, 2 * D_MODEL:]
    x = _stream_tile(x_refs, pl.program_id(1) + t_lo, ctx_tiles)
    r = alpha * x + gate * acc
    mu = jnp.mean(r, axis=-1, keepdims=True)
    var = jnp.mean(jnp.square(r - mu), axis=-1, keepdims=True)
    o_ref[0] = (r - mu) * lax.rsqrt(var + LN_EPS) * g_ref[...] + b_ref[...]


def _outproj(ssd_o, mla_o, mla_lo, ret_o, stream, mod_all, w, ln_g, ln_b, t_lo, ctx_tiles, alpha):
    bsz, s, _ = ssd_o.shape
    d = stream[0].shape[-1]
    nt = s // TOKEN_TILE
    tok = lambda wd: pl.BlockSpec((1, TOKEN_TILE, wd), lambda b, t: (b, t + t_lo, 0))
    kern = functools.partial(_outproj_kernel, n_stream=len(stream), ctx_tiles=ctx_tiles, t_lo=t_lo, alpha=alpha)
    return pl.pallas_call(
        kern,
        out_shape=jax.ShapeDtypeStruct((bsz, s - t_lo * TOKEN_TILE, d), F32),
        grid=(bsz, nt - t_lo),
        in_specs=[tok(SSD_WIDTH),
                  pl.BlockSpec((1, TOKEN_TILE, MLA_WIDTH), lambda b, t: (b, t + t_lo - mla_lo, 0)),
                  tok(RET_WIDTH)] + _stream_specs(stream, t_lo, ctx_tiles) + [
                  pl.BlockSpec((1, 1, 1, 3 * d),
                               lambda b, t: (b, jnp.minimum(t + t_lo, ctx_tiles) // ctx_tiles, 0, 0)),
                  pl.BlockSpec(w.shape, lambda b, t: (0, 0)),
                  pl.BlockSpec((1, d), lambda b, t: (0, 0)),
                  pl.BlockSpec((1, d), lambda b, t: (0, 0))],
        out_specs=pl.BlockSpec((1, TOKEN_TILE, d), lambda b, t: (b, t, 0)),
        compiler_params=_params("parallel", "arbitrary"),
        name="out_proj",
    )(ssd_o, mla_o, ret_o, *stream, mod_all, w, ln_g.reshape(1, d), ln_b.reshape(1, d))


def _rope_tables(n_ctx, n_lat, rot_dim, width, lane0, period):
    t = np.arange(n_lat)
    row = (t // GRID_W).astype(np.float32)
    col = (t % GRID_W).astype(np.float32)
    n_freq = rot_dim // 4
    inv = (ROPE_THETA ** (-np.arange(n_freq, dtype=np.float32) / n_freq)).astype(np.float32)
    ang = np.concatenate([row[:, None] * inv, col[:, None] * inv], -1).astype(np.float32)
    cos, sin = np.cos(ang), np.sin(ang)
    half = rot_dim // 2
    lane = np.arange(width) % period - lane0
    in_rot = (lane >= 0) & (lane < rot_dim)
    idx = np.where(in_rot, lane % half, 0)
    first = in_rot & (lane < half)
    second = in_rot & (lane >= half)
    cos_t = np.where(in_rot[None, :], cos[:, idx], 1.0)
    sin_a = np.where(first[None, :], -sin[:, idx], 0.0)
    sin_b = np.where(second[None, :], sin[:, idx], 0.0)
    ctx_pad = lambda a, v: np.concatenate([np.full((n_ctx, width), v), a], axis=0).astype(np.float32)
    return ctx_pad(cos_t, 1.0), ctx_pad(sin_a, 0.0), ctx_pad(sin_b, 0.0)


def _mla_weights(w_uq, w_ukv):
    qd = MLA_NOPE_DIM + MLA_ROPE_DIM
    wq = w_uq.reshape(MLA_Q_LORA, MLA_HEADS, qd)
    wq = jnp.pad(wq, ((0, 0), (0, 0), (0, MLA_HEAD_SLOT - qd))).reshape(MLA_Q_LORA, MLA_HEADS * MLA_HEAD_SLOT)
    wkv = w_ukv.reshape(MLA_KV_LORA, MLA_HEADS, MLA_NOPE_DIM + MLA_V_DIM)
    wk = jnp.pad(wkv[..., :MLA_NOPE_DIM], ((0, 0), (0, 0), (0, MLA_HEAD_SLOT - MLA_NOPE_DIM)))
    wk = wk.reshape(MLA_KV_LORA, MLA_HEADS * MLA_HEAD_SLOT)
    wv = wkv[..., MLA_NOPE_DIM:].reshape(MLA_KV_LORA, MLA_PAIRS, 2 * MLA_V_DIM)
    wv = jnp.pad(wv, ((0, 0), (0, 0), (0, MLA_V_SLOT - 2 * MLA_V_DIM))).reshape(MLA_KV_LORA, MLA_PAIRS * MLA_V_SLOT)
    return wq.astype(BF16), wk.astype(BF16), wv.astype(BF16)


def _lane_pad(v):
    return jnp.pad(v, (0, LANES - v.shape[0])).reshape(1, LANES)


def kernel(x, c, ctx, c_ctx, w_ada, b_ada, w_in, ssd_conv_w, ssd_conv_b, ssd_a_log_f, ssd_a_log_b,
           ssd_dt_bias_f, ssd_dt_bias_b, ssd_d, ssd_norm_w, mla_q_norm, mla_w_uq, mla_kv_norm, mla_w_ukv,
           ret_log_rate_f, ret_log_rate_b, w_out, ln_g, ln_b):
    bsz, n_lat, d = x.shape
    n_ctx = ctx.shape[1]
    depth = w_in.shape[0]
    assert d == D_MODEL and n_ctx == TOKEN_TILE and n_lat % TOKEN_TILE == 0
    alpha = (2 * depth) ** 0.25
    s = n_ctx + n_lat
    nc_ctx = n_ctx // CHUNK
    ctx_tiles = n_ctx // TOKEN_TILE
    nt = s // TOKEN_TILE

    cos_m, sa_m, sb_m = _rope_tables(n_ctx, n_lat, MLA_ROPE_DIM, LANES, MLA_ROPE_LANE, LANES)
    cos_r, sa_r, sb_r = _rope_tables(n_ctx, n_lat, RET_QK_DIM, RET_QK_WIDTH, 0, RET_QK_DIM)

    rows = 16
    cc = jnp.zeros((rows, d), F32).at[:bsz].set(c).at[bsz].set(c_ctx)
    stream = (ctx, x)

    for i in range(depth):
        last = i == depth - 1
        mod = _ada(cc, w_ada[i], b_ada[i])
        mod_all = jnp.stack([jnp.broadcast_to(mod[bsz], (bsz, 3 * d)), mod[:bsz]], axis=1)
        mod_all = mod_all.reshape(bsz, 2, 1, 3 * d)
        both = lambda f, b: jnp.stack([f, b], axis=0)
        head_rows = lambda v: jnp.broadcast_to(v[:, None], (SSD_HEADS, TOKEN_TILE))
        alogt = both(head_rows(ssd_a_log_f[i]), head_rows(ssd_a_log_b[i]))
        dtbt = both(head_rows(ssd_dt_bias_f[i]), head_rows(ssd_dt_bias_b[i]))
        (zs, xs_bf, xst_bf, bc, colp, rowp, small, cq, ckv, mgs, rq, rk_bf, rkt, rv_bf, rgs) = _inproj(
            stream, mod_all, _relayout_w_in(w_in[i]), ssd_conv_w[i], ssd_conv_b[i].reshape(1, -1), alogt, dtbt,
            cos_r, sa_r, sb_r, ctx_tiles)

        dwide = jnp.repeat(ssd_d[i], SSD_HEAD_DIM).reshape(1, SSD_WIDTH)
        rates = both(_lane_pad(ret_log_rate_f[i]), _lane_pad(ret_log_rate_b[i]))
        prep = (xs_bf, xst_bf, bc, colp, rowp, rq, rk_bf, rkt, rv_bf)
        ssd_o, ret_o = _mixer_scan(prep, zs, rgs, dwide, ssd_norm_w[i].reshape(1, -1), _ret_consts(rates), nc_ctx)

        wq, wk, wv = _mla_weights(mla_w_uq[i], mla_w_ukv[i])
        q, k, v = _mla_proj(cq, ckv, small, cos_m, sa_m, sb_m, mla_q_norm[i].reshape(1, -1),
                            mla_kv_norm[i].reshape(1, -1), wq, wk, wv)
        mla_lo = ctx_tiles if last else 0
        mla_o = _mla_attn_lat(q, k, v, mgs, ctx_tiles, None if last else _mla_attn_ctx(q, k, v, mgs, ctx_tiles))

        out = _outproj(ssd_o, mla_o, mla_lo, ret_o, stream, mod_all, w_out[i].astype(BF16), ln_g[i], ln_b[i],
                       ctx_tiles if last else 0, ctx_tiles, alpha)
        stream = (out,)
    return stream[0]
```

```python
import functools
import math

import jax
import jax.numpy as jnp
import numpy as np
from jax import lax
from jax.experimental import pallas as pl
from jax.experimental.pallas import tpu as pltpu

F32 = jnp.float32
BF16 = jnp.bfloat16

LANES = 128
SUBLANES = 8
VMEM_LIMIT_BYTES = 56 * 1024 * 1024

GRID_W = 64
D_MODEL = 1024
SSD_WIDTH = 1024
SSD_HEAD_DIM = 64
SSD_HEADS = 16
SSD_GROUPS = 2
SSD_STATE = 128
SSD_CONV = 5
SSD_CONV_DIM = SSD_WIDTH + 2 * SSD_GROUPS * SSD_STATE
MLA_WIDTH = 512
MLA_V_DIM = 64
MLA_HEADS = 8
MLA_NOPE_DIM = 64
MLA_ROPE_DIM = 32
MLA_Q_LORA = 384
MLA_KV_LORA = 256
MLA_SCALE = (MLA_NOPE_DIM + MLA_ROPE_DIM) ** -0.5
RET_WIDTH = 512
RET_HEADS = 4
RET_V_DIM = 128
RET_QK_DIM = 64
RET_QK_WIDTH = RET_HEADS * RET_QK_DIM
CHUNK = 128
ROPE_THETA = 10000.0
LN_EPS = 1e-5
RMS_EPS = 1e-6
IN_WIDTHS = (SSD_WIDTH, SSD_CONV_DIM, SSD_HEADS, MLA_Q_LORA, MLA_KV_LORA, MLA_ROPE_DIM, MLA_WIDTH,
             RET_QK_WIDTH, RET_QK_WIDTH, RET_WIDTH, RET_WIDTH)

TOKEN_TILE = 256
MLA_HEAD_SLOT = 128
MLA_ROPE_LANE = MLA_NOPE_DIM
MLA_PAIRS = MLA_HEADS // 2
MLA_V_SLOT = 256
MLA_Q_SCALE = MLA_SCALE * math.log2(math.e)
NEG_BIG = -1e30


def _params(*sem):
    return pltpu.CompilerParams(dimension_semantics=sem, vmem_limit_bytes=VMEM_LIMIT_BYTES)


def _silu(v):
    return v * jax.nn.sigmoid(v)


def _softplus(v):
    return jnp.maximum(v, 0.0) + jnp.log1p(jnp.exp(-jnp.abs(v)))


def _dot(a, b):
    return jnp.dot(a, b, preferred_element_type=F32)


def _dot_exact(a, b):
    return jnp.dot(a, b, preferred_element_type=F32, precision=lax.Precision.HIGHEST)


def _dot_nt(a, b):
    return lax.dot_general(a, b, (((1,), (1,)), ((), ())), preferred_element_type=F32)


def _ada_kernel(c_ref, w_ref, b_ref, o_ref):
    o_ref[...] = _dot_exact(_silu(c_ref[...]), w_ref[...]) + b_ref[...]


def _ada(cc, w, b):
    rows, d = cc.shape
    n = w.shape[1]
    tn = 1024
    return pl.pallas_call(
        _ada_kernel,
        out_shape=jax.ShapeDtypeStruct((rows, n), F32),
        grid=(n // tn,),
        in_specs=[pl.BlockSpec((rows, d), lambda j: (0, 0)),
                  pl.BlockSpec((d, tn), lambda j: (0, j)),
                  pl.BlockSpec((1, tn), lambda j: (0, j))],
        out_specs=pl.BlockSpec((rows, tn), lambda j: (0, j)),
        compiler_params=_params("arbitrary"),
        name="ada_mod",
    )(cc, w, b.reshape(1, n))


def _chunk_index(d, c, nc_ctx, nc):
    bwd = jnp.where(c < nc_ctx, nc_ctx - 1 - c, nc + nc_ctx - 1 - c)
    return jnp.where(d == 0, c, bwd)


def _tri_masks(d):
    ii = lax.broadcasted_iota(jnp.int32, (CHUNK, CHUNK), 0)
    jj = lax.broadcasted_iota(jnp.int32, (CHUNK, CHUNK), 1)
    sgn = 1 - 2 * d
    mask = (ii - jj) * sgn >= 0
    mask_t = (jj - ii) * sgn >= 0
    return mask, mask_t


def _rope3(v, cos_t, sin_a, sin_b, shift):
    width = v.shape[-1]
    return v * cos_t + pltpu.roll(v, width - shift, 1) * sin_a + pltpu.roll(v, shift, 1) * sin_b


PROJ_WIDTHS = (SSD_WIDTH, SSD_CONV_DIM, LANES, MLA_Q_LORA, MLA_KV_LORA, MLA_WIDTH,
               RET_QK_WIDTH, RET_QK_WIDTH, RET_WIDTH, RET_WIDTH)
PROJ_OFFSETS = tuple(int(o) for o in np.cumsum((0,) + PROJ_WIDTHS[:-1]))
DT_LANE = 0
KROPE_LANE = MLA_ROPE_LANE
ROW_CUM, ROW_DT, ROW_WEND, ROW_ETOT = 0, 16, 32, 48
ROW_PACK = 64
TILE_CHUNKS = TOKEN_TILE // CHUNK


def _relayout_w_in(w_in):
    splits = [int(s) for s in np.cumsum(IN_WIDTHS)[:-1]]
    z, xbc, dt, cq, ckv, kr, mg, rq, rk, rv, rg = jnp.split(w_in, splits, axis=-1)
    d = w_in.shape[0]
    small = jnp.concatenate([dt, jnp.zeros((d, KROPE_LANE - SSD_HEADS), w_in.dtype), kr,
                             jnp.zeros((d, LANES - KROPE_LANE - MLA_ROPE_DIM), w_in.dtype)], axis=-1)
    return jnp.concatenate([z, xbc, small, cq, ckv, mg, rq, rk, rv, rg], axis=-1).astype(BF16)


def _stream_specs(stream, t_lo, ctx_tiles):
    d = stream[0].shape[-1]
    if len(stream) == 1:
        return [pl.BlockSpec((1, TOKEN_TILE, d), lambda b, t: (b, t + t_lo, 0))]
    return [pl.BlockSpec((1, TOKEN_TILE, d), lambda b, t: (b, jnp.minimum(t + t_lo, ctx_tiles - 1), 0)),
            pl.BlockSpec((1, TOKEN_TILE, d), lambda b, t: (b, jnp.maximum(t + t_lo - ctx_tiles, 0), 0))]


def _halo_specs(stream, ctx_tiles):
    arr = stream[-1]
    d = arr.shape[-1]
    shift = 0 if len(stream) == 1 else ctx_tiles
    per_tile = TOKEN_TILE // SUBLANES
    nblk = arr.shape[1] // SUBLANES
    prev = lambda b, t: (b, jnp.clip((t - shift) * per_tile - 1, 0, nblk - 1), 0)
    nxt = lambda b, t: (b, jnp.clip((t - shift + 1) * per_tile, 0, nblk - 1), 0)
    return [pl.BlockSpec((1, SUBLANES, d), prev), pl.BlockSpec((1, SUBLANES, d), nxt)]


def _stream_tile(refs, t, ctx_tiles):
    if len(refs) == 1:
        return refs[0][0]
    return jnp.where(t < ctx_tiles, refs[0][0], refs[1][0])


def _inproj_kernel(*refs, n_stream, ctx_tiles, nt):
    x_refs = refs[:n_stream]
    (prev_ref, next_ref, mod_ref, w_ref, convw_ref, convb_ref, alogt_ref, dtbt_ref,
     cos_ref, sa_ref, sb_ref) = refs[n_stream:n_stream + 11]
    (zs_ref, xs_ref, xst_ref, bc_ref, colp_ref, rowp_ref, small_ref, cq_ref, ckv_ref, mgs_ref,
     q_ref, kbf_ref, kt_ref, v_ref, rgs_ref) = refs[n_stream + 11:]
    t = pl.program_id(1)
    mod = mod_ref[0, 0]
    sh = mod[:, :D_MODEL]
    sc = mod[:, D_MODEL:2 * D_MODEL]
    modulate = lambda v: (v * (1.0 + sc) + sh).astype(BF16)
    h = modulate(_stream_tile(x_refs, t, ctx_tiles))
    h_ext = jnp.concatenate([modulate(prev_ref[0]), h, modulate(next_ref[0])], axis=0)

    def proj(g, lo=0, width=None, lhs=None):
        off = PROJ_OFFSETS[g] + lo
        width = PROJ_WIDTHS[g] - lo if width is None else width
        return _dot(h if lhs is None else lhs, w_ref[:, off:off + width])

    first = jnp.logical_or(t == 0, t == ctx_tiles)
    last = jnp.logical_or(t == ctx_tiles - 1, t == nt - 1)
    row = lax.broadcasted_iota(jnp.int32, (TOKEN_TILE + 2 * SUBLANES, 1), 0)
    keep = jnp.where(jnp.logical_or(jnp.logical_and(row < SUBLANES, first),
                                    jnp.logical_and(row >= SUBLANES + TOKEN_TILE, last)), 0.0, 1.0)
    unit_w = 2 * LANES
    ext_parts = {}

    def ext_unit(e):
        ext_parts[e] = proj(1, e * unit_w, unit_w, lhs=h_ext) * keep

    def conv_strip(j, anchor):
        pad = SSD_CONV // 2
        sl = slice(j * LANES, (j + 1) * LANES)
        esl = slice((j % 2) * LANES, (j % 2 + 1) * LANES)
        ext = ext_parts[j // 2]
        acc = convb_ref[:, sl] + anchor
        for k in range(SSD_CONV):
            lo = SUBLANES - pad + k
            acc = acc + convw_ref[k:k + 1, sl] * ext[lo:lo + TOKEN_TILE, esl]
        u = _silu(acc)
        if j < SSD_WIDTH // LANES:
            xs_ref[0, :, sl] = u.astype(BF16)
            xst_ref[0, sl, :] = u.T.astype(BF16)
        else:
            bc_ref[0, :, j * LANES - SSD_WIDTH:(j + 1) * LANES - SSD_WIDTH] = u

    def gate_unit(g, ref, lo, width):
        def run():
            val = proj(g, lo, width)
            ref[0, :, lo:lo + width] = _silu(val)
            return val
        return run

    def plain_unit(g, ref, cast=None):
        def run():
            val = proj(g)
            ref[0] = val if cast is None else val.astype(cast)
            return val
        return run

    def ret_q_unit():
        val = proj(6)
        q_ref[0] = _rope3(val, cos_ref[...], sa_ref[...], sb_ref[...], RET_QK_DIM // 2)
        return val

    def ret_k_unit():
        val = proj(7)
        kk = _rope3(val, cos_ref[...], sa_ref[...], sb_ref[...], RET_QK_DIM // 2) * (RET_QK_DIM ** -0.5)
        kbf_ref[0] = kk.astype(BF16)
        kt_ref[0] = kk.T
        return val

    units = [gate_unit(0, zs_ref, lo, unit_w) for lo in range(0, SSD_WIDTH, unit_w)]
    units += [plain_unit(3, cq_ref), plain_unit(4, ckv_ref)]
    units += [gate_unit(5, mgs_ref, lo, unit_w) for lo in range(0, MLA_WIDTH, unit_w)]
    units += [ret_q_unit, ret_k_unit, plain_unit(8, v_ref, BF16)]
    units += [gate_unit(9, rgs_ref, lo, unit_w) for lo in range(0, RET_WIDTH, unit_w)]
    small = proj(2)
    small_ref[0] = small
    n_strips = SSD_CONV_DIM // LANES
    ext_unit(0)
    anchor = jnp.zeros((1, LANES), F32)
    for j in range(n_strips):
        if j % 2 == 0 and j // 2 + 1 < n_strips // 2:
            ext_unit(j // 2 + 1)
        nxt = units.pop(0)()[0:1, :LANES] * 0.0 if units else anchor
        conv_strip(j, anchor)
        anchor = nxt
    for unit in units:
        unit()

    dt_raw_t = small.T[DT_LANE:DT_LANE + SSD_HEADS, :]
    pad_rows = jnp.zeros((CHUNK - SSD_HEADS, CHUNK), F32)
    for d in range(2):
        _, mask_t = _tri_masks(d)
        tri_t = jnp.where(mask_t, 1.0, 0.0)
        dt_all = _softplus(dt_raw_t + dtbt_ref[d])
        la_all = dt_all * -jnp.exp(alogt_ref[d])
        base = d * ROW_PACK
        for j in range(TILE_CHUNKS):
            cs = slice(j * CHUNK, (j + 1) * CHUNK)
            dt_row, la_row = dt_all[:, cs], la_all[:, cs]
            cum_row = _dot_exact(la_row, tri_t)
            colp_ref[0, d, cs, :] = jnp.concatenate([cum_row, pad_rows], axis=0).T
            tot_row = jnp.sum(la_row, axis=1, keepdims=True)
            rowp_ref[0, j, base + ROW_CUM:base + ROW_CUM + SSD_HEADS, :] = cum_row
            rowp_ref[0, j, base + ROW_DT:base + ROW_DT + SSD_HEADS, :] = dt_row
            rowp_ref[0, j, base + ROW_WEND:base + ROW_WEND + SSD_HEADS, :] = dt_row * jnp.exp(tot_row - cum_row)
            rowp_ref[0, j, base + ROW_ETOT:base + ROW_ETOT + SSD_HEADS, :] = jnp.broadcast_to(
                jnp.exp(tot_row), (SSD_HEADS, CHUNK))


def _inproj(stream, mod_all, w, conv_w, conv_b, alogt, dtbt, cos_t, sin_a, sin_b, ctx_tiles):
    bsz, d = stream[0].shape[0], stream[0].shape[-1]
    s = sum(a.shape[1] for a in stream)
    nt = s // TOKEN_TILE
    nc = s // CHUNK
    tok = lambda wd: pl.BlockSpec((1, TOKEN_TILE, wd), lambda b, t: (b, t, 0))
    chan = lambda wd: pl.BlockSpec((1, wd, TOKEN_TILE), lambda b, t: (b, 0, t))
    full = lambda a: pl.BlockSpec(a.shape, lambda b, t: (0,) * a.ndim)
    tab = pl.BlockSpec((TOKEN_TILE, RET_QK_WIDTH), lambda b, t: (t, 0))
    bcw = SSD_CONV_DIM - SSD_WIDTH
    outs = [((bsz, s, SSD_WIDTH), F32, tok(SSD_WIDTH)),
            ((bsz, s, SSD_WIDTH), BF16, tok(SSD_WIDTH)),
            ((bsz, SSD_WIDTH, s), BF16, chan(SSD_WIDTH)),
            ((bsz, s, bcw), F32, tok(bcw)),
            ((bsz, 2, s, LANES), F32, pl.BlockSpec((1, 2, TOKEN_TILE, LANES), lambda b, t: (b, 0, t, 0))),
            ((bsz, nc, 2 * ROW_PACK, CHUNK), F32,
             pl.BlockSpec((1, TILE_CHUNKS, 2 * ROW_PACK, CHUNK), lambda b, t: (b, t, 0, 0))),
            ((bsz, s, LANES), F32, tok(LANES)),
            ((bsz, s, MLA_Q_LORA), F32, tok(MLA_Q_LORA)),
            ((bsz, s, MLA_KV_LORA), F32, tok(MLA_KV_LORA)),
            ((bsz, s, MLA_WIDTH), F32, tok(MLA_WIDTH)),
            ((bsz, s, RET_QK_WIDTH), F32, tok(RET_QK_WIDTH)),
            ((bsz, s, RET_QK_WIDTH), BF16, tok(RET_QK_WIDTH)),
            ((bsz, RET_QK_WIDTH, s), F32, chan(RET_QK_WIDTH)),
            ((bsz, s, RET_WIDTH), BF16, tok(RET_WIDTH)),
            ((bsz, s, RET_WIDTH), F32, tok(RET_WIDTH))]
    kern = functools.partial(_inproj_kernel, n_stream=len(stream), ctx_tiles=ctx_tiles, nt=nt)
    return pl.pallas_call(
        kern,
        out_shape=[jax.ShapeDtypeStruct(shape, dt) for shape, dt, _ in outs],
        grid=(bsz, nt),
        in_specs=_stream_specs(stream, 0, ctx_tiles) + _halo_specs(stream, ctx_tiles) + [
            pl.BlockSpec((1, 1, 1, 3 * d), lambda b, t: (b, jnp.minimum(t, ctx_tiles) // ctx_tiles, 0, 0)),
            pl.BlockSpec(w.shape, lambda b, t: (0, 0), pipeline_mode=pl.Buffered(1)),
            full(conv_w), full(conv_b), full(alogt), full(dtbt), tab, tab, tab],
        out_specs=[spec for _, _, spec in outs],
        compiler_params=_params("parallel", "arbitrary"),
        name="in_proj",
    )(*stream, stream[-1], stream[-1], mod_all, w, conv_w, conv_b, alogt, dtbt, cos_t, sin_a, sin_b)


def _ret_consts_kernel(rate_ref, dec_ref, ecum_ref, wend_ref, etot_ref):
    d = pl.program_id(0)
    mask, _ = _tri_masks(d)
    ii = lax.broadcasted_iota(jnp.int32, (CHUNK, CHUNK), 0)
    jj = lax.broadcasted_iota(jnp.int32, (CHUNK, CHUNK), 1)
    dist = jnp.abs(ii - jj).astype(F32)
    pos_i = lax.broadcasted_iota(jnp.int32, (CHUNK, RET_QK_DIM), 0)
    steps_col = jnp.where(d == 0, pos_i + 1, CHUNK - pos_i).astype(F32)
    pos_j = lax.broadcasted_iota(jnp.int32, (RET_QK_DIM, CHUNK), 1)
    left_row = jnp.where(d == 0, CHUNK - 1 - pos_j, pos_j).astype(F32)
    for hd in range(RET_HEADS):
        la = -jnp.exp(rate_ref[0, :, hd:hd + 1])
        dec_ref[0, hd] = jnp.exp(jnp.where(mask, dist * la, NEG_BIG))
        ecum_ref[0, :, hd * RET_QK_DIM:(hd + 1) * RET_QK_DIM] = jnp.exp(steps_col * la)
        wend_ref[0, hd * RET_QK_DIM:(hd + 1) * RET_QK_DIM, :] = jnp.exp(left_row * la)
        etot_ref[0, hd] = jnp.broadcast_to(jnp.exp(CHUNK * la), (1, LANES))


def _ret_consts(rates):
    return pl.pallas_call(
        _ret_consts_kernel,
        out_shape=[jax.ShapeDtypeStruct((2, RET_HEADS, CHUNK, CHUNK), F32),
                   jax.ShapeDtypeStruct((2, CHUNK, RET_QK_WIDTH), F32),
                   jax.ShapeDtypeStruct((2, RET_QK_WIDTH, CHUNK), F32),
                   jax.ShapeDtypeStruct((2, RET_HEADS, 1, LANES), F32)],
        grid=(2,),
        in_specs=[pl.BlockSpec((1, 1, LANES), lambda d: (d, 0, 0))],
        out_specs=[pl.BlockSpec((1, RET_HEADS, CHUNK, CHUNK), lambda d: (d, 0, 0, 0)),
                   pl.BlockSpec((1, CHUNK, RET_QK_WIDTH), lambda d: (d, 0, 0)),
                   pl.BlockSpec((1, RET_QK_WIDTH, CHUNK), lambda d: (d, 0, 0)),
                   pl.BlockSpec((1, RET_HEADS, 1, LANES), lambda d: (d, 0, 0, 0))],
        compiler_params=_params("arbitrary"),
        name="ret_consts",
    )(rates)


SCAN_BATCH = 2


def _mixer_scan_kernel(xs_ref, xst_ref, bc_ref, colp_ref, rowp_ref, z_ref, dwide_ref, normw_ref,
                       q_ref, kbf_ref, kt_ref, v_ref, g_ref, dec_ref, ecum_ref, wend_ref, etot_ref,
                       ssd_out_ref, ret_out_ref, yf_ref, yr_ref, ht_ref, hr_ref, *, nc_ctx, nc):
    d = pl.program_id(1)
    c = pl.program_id(2)
    cidx = _chunk_index(d, c, nc_ctx, nc)
    chains = range(SCAN_BATCH)

    @pl.when(c == 0)
    def _():
        ht_ref[...] = jnp.zeros_like(ht_ref)
        hr_ref[...] = jnp.zeros_like(hr_ref)

    mask, _ = _tri_masks(d)
    lane = lax.broadcasted_iota(jnp.int32, (CHUNK, LANES), 1)
    sub = lax.broadcasted_iota(jnp.int32, (CHUNK, LANES), 0)
    lane_lo = lane < SSD_HEAD_DIM
    sub_lo = sub < SSD_HEAD_DIM
    heads_per_group = SSD_HEADS // SSD_GROUPS

    ret = []
    for bb in chains:
        q = q_ref[bb]
        ret.append(dict(q_bf=q.astype(BF16), k_bf=kbf_ref[bb], qe_bf=(q * ecum_ref[0]).astype(BF16),
                        ktw_bf=(kt_ref[bb] * wend_ref[0]).astype(BF16), v_bf=v_ref[bb], lhs={}, ys=[]))

    def ret_scores(bb, hd):
        r = ret[bb]
        qs = slice(hd * RET_QK_DIM, (hd + 1) * RET_QK_DIM)
        sc = _dot_nt(r["q_bf"][:, qs], r["k_bf"][:, qs]) * dec_ref[0, hd]
        r["lhs"][hd] = jnp.concatenate([sc.astype(BF16), r["qe_bf"][:, qs]], axis=1)

    def ret_update(bb, hd):
        r = ret[bb]
        qs = slice(hd * RET_QK_DIM, (hd + 1) * RET_QK_DIM)
        v_h = r["v_bf"][:, hd * RET_V_DIM:(hd + 1) * RET_V_DIM]
        h_h = hr_ref[bb, hd]
        rhs = jnp.concatenate([v_h, h_h.astype(BF16)], axis=0)
        r["ys"].append(_dot(r["lhs"][hd], rhs))
        hr_ref[bb, hd] = etot_ref[0, hd] * h_h + _dot(r["ktw_bf"][qs, :], v_h)

    ssd = []
    for bb in chains:
        rowp = rowp_ref[bb, 0]
        ssd.append(dict(xs_bf=xs_ref[bb], bc=bc_ref[bb], cum_col=colp_ref[bb, 0],
                        cum_row=rowp[ROW_CUM:ROW_CUM + SSD_HEADS], dt_row=rowp[ROW_DT:ROW_DT + SSD_HEADS],
                        wend_row=rowp[ROW_WEND:ROW_WEND + SSD_HEADS],
                        etot_row=rowp[ROW_ETOT:ROW_ETOT + SSD_HEADS], gmats={}, ys=[]))

    def ssd_pair(bb, k):
        s = ssd[bb]
        g = (2 * k) // heads_per_group
        if g not in s["gmats"]:
            b_g = s["bc"][:, g * SSD_STATE:(g + 1) * SSD_STATE].astype(BF16)
            c_g = s["bc"][:, (SSD_GROUPS + g) * SSD_STATE:(SSD_GROUPS + g + 1) * SSD_STATE]
            s["gmats"][g] = (_dot_nt(c_g.astype(BF16), b_g), c_g, b_g)
        scores, c_g, b_g = s["gmats"][g]
        ha, hb = 2 * k, 2 * k + 1
        xs_pair = s["xs_bf"][:, k * LANES:(k + 1) * LANES]
        ht_pair = ht_ref[bb, k]
        rhs = jnp.concatenate([xs_pair, ht_pair.T.astype(BF16)], axis=0)
        y_heads = []
        for hd in (ha, hb):
            cb = jnp.broadcast_to(s["cum_col"][:, hd:hd + 1], (CHUNK, CHUNK))
            seg = cb - s["cum_row"][hd:hd + 1, :]
            dec = jnp.exp(jnp.where(mask, seg, NEG_BIG))
            m = scores * dec * s["dt_row"][hd:hd + 1, :]
            lhs = jnp.concatenate([m, c_g * jnp.exp(cb)], axis=1).astype(BF16)
            y_heads.append(_dot(lhs, rhs))
        s["ys"].append(jnp.where(lane_lo, y_heads[0], y_heads[1]))
        w_pair = jnp.where(sub_lo, s["wend_row"][ha:ha + 1, :], s["wend_row"][hb:hb + 1, :])
        e_pair = jnp.where(sub_lo, s["etot_row"][ha:ha + 1, :], s["etot_row"][hb:hb + 1, :])
        xw = (xst_ref[bb, k * LANES:(k + 1) * LANES, :].astype(F32) * w_pair).astype(BF16)
        ht_ref[bb, k] = e_pair * ht_pair + _dot(xw, b_g)

    for k in range(SSD_HEADS // 2):
        for bb in chains:
            ssd_pair(bb, k)
        for bb in chains:
            if k % 2 == 0:
                ret_scores(bb, k // 2)
            else:
                ret_update(bb, k // 2)
    y = [jnp.concatenate(ssd[bb]["ys"], axis=1) for bb in chains]
    yr = [jnp.concatenate(ret[bb]["ys"], axis=1) for bb in chains]

    @pl.when(d == 0)
    def _():
        for bb in chains:
            yf_ref[bb, cidx] = y[bb].astype(yf_ref.dtype)
            yr_ref[bb, cidx] = yr[bb].astype(yr_ref.dtype)

    @pl.when(d == 1)
    def _():
        for bb in chains:
            tot = yf_ref[bb, cidx].astype(F32) + y[bb] + ssd[bb]["xs_bf"].astype(F32) * dwide_ref[...]
            gated = tot * z_ref[bb]
            ms = jnp.mean(gated * gated, axis=-1, keepdims=True)
            ssd_out_ref[bb] = gated * lax.rsqrt(ms + RMS_EPS) * normw_ref[...]
            rtot = yr_ref[bb, cidx].astype(F32) + yr[bb]
            outs = []
            for hd in range(RET_HEADS):
                t_h = rtot[:, hd * RET_V_DIM:(hd + 1) * RET_V_DIM]
                mu = jnp.mean(t_h, axis=-1, keepdims=True)
                var = jnp.mean(jnp.square(t_h - mu), axis=-1, keepdims=True)
                outs.append((t_h - mu) * lax.rsqrt(var + LN_EPS))
            ret_out_ref[bb] = jnp.concatenate(outs, axis=1) * g_ref[bb]


def _mixer_scan(prep, z, rg, dwide, normw, rconsts, nc_ctx):
    xs_bf, xst_bf, bc, colp, rowp, rq, rk_bf, rkt, rv_bf = prep
    dec, ecum, wend, etot = rconsts
    bsz, s, _ = z.shape
    nc = s // CHUNK
    nb = SCAN_BATCH
    first_bwd = nc_ctx - 1

    def cix(d, c):
        return _chunk_index(d, c, nc_ctx, nc)

    def late(d, c):
        return jnp.where(d == 0, first_bwd, cix(d, c))

    tokc = lambda wd: pl.BlockSpec((nb, CHUNK, wd), lambda b, d, c: (b, cix(d, c), 0))
    tokl = lambda wd: pl.BlockSpec((nb, CHUNK, wd), lambda b, d, c: (b, late(d, c), 0))
    chan = lambda wd: pl.BlockSpec((nb, wd, CHUNK), lambda b, d, c: (b, 0, cix(d, c)))
    row1 = lambda wd: pl.BlockSpec((1, wd), lambda b, d, c: (0, 0))
    kern = functools.partial(_mixer_scan_kernel, nc_ctx=nc_ctx, nc=nc)
    return pl.pallas_call(
        kern,
        out_shape=[jax.ShapeDtypeStruct((bsz, s, SSD_WIDTH), F32),
                   jax.ShapeDtypeStruct((bsz, s, RET_WIDTH), F32)],
        grid=(bsz // nb, 2, nc),
        in_specs=[tokc(SSD_WIDTH), chan(SSD_WIDTH), tokc(bc.shape[-1]),
                  pl.BlockSpec((nb, 1, CHUNK, LANES), lambda b, d, c: (b, d, cix(d, c), 0)),
                  pl.BlockSpec((nb, 1, ROW_PACK, CHUNK), lambda b, d, c: (b, cix(d, c), d, 0)),
                  tokl(SSD_WIDTH), row1(SSD_WIDTH), row1(SSD_WIDTH),
                  tokc(RET_QK_WIDTH), tokc(RET_QK_WIDTH), chan(RET_QK_WIDTH), tokc(RET_WIDTH), tokl(RET_WIDTH),
                  pl.BlockSpec((1, RET_HEADS, CHUNK, CHUNK), lambda b, d, c: (d, 0, 0, 0)),
                  pl.BlockSpec((1, CHUNK, RET_QK_WIDTH), lambda b, d, c: (d, 0, 0)),
                  pl.BlockSpec((1, RET_QK_WIDTH, CHUNK), lambda b, d, c: (d, 0, 0)),
                  pl.BlockSpec((1, RET_HEADS, 1, LANES), lambda b, d, c: (d, 0, 0, 0))],
        out_specs=[tokl(SSD_WIDTH), tokl(RET_WIDTH)],
        scratch_shapes=[pltpu.VMEM((nb, nc, CHUNK, SSD_WIDTH), BF16),
                        pltpu.VMEM((nb, nc, CHUNK, RET_WIDTH), BF16),
                        pltpu.VMEM((nb, SSD_HEADS // 2, LANES, SSD_STATE), F32),
                        pltpu.VMEM((nb, RET_HEADS, RET_QK_DIM, RET_V_DIM), F32)],
        compiler_params=_params("parallel", "arbitrary", "arbitrary"),
        name="mixer_scan",
    )(xs_bf, xst_bf, bc, colp, rowp, z, dwide, normw, rq, rk_bf, rkt, rv_bf, rg, dec, ecum, wend, etot)


def _rms(v, g):
    return v * lax.rsqrt(jnp.mean(v * v, axis=-1, keepdims=True) + RMS_EPS) * g


def _mla_proj_kernel(cq_ref, ckv_ref, small_ref, cos_ref, sa_ref, sb_ref, qn_ref, kvn_ref,
                     wq_ref, wk_ref, wv_ref, q_ref, k_ref, v_ref):
    cqn = _rms(cq_ref[0], qn_ref[...]).astype(BF16)
    ckvn = _rms(ckv_ref[0], kvn_ref[...]).astype(BF16)
    q = _dot(cqn, wq_ref[...])
    kn = _dot(ckvn, wk_ref[...])
    vlane = lax.broadcasted_iota(jnp.int32, (1, MLA_PAIRS * MLA_V_SLOT), 1)
    ones = jnp.where(vlane % MLA_V_SLOT >= 2 * MLA_V_DIM, 1.0, 0.0)
    v_ref[0] = (_dot(ckvn, wv_ref[...]) + ones).astype(BF16)
    cos_t, sin_a, sin_b = cos_ref[...], sa_ref[...], sb_ref[...]
    lane = lax.broadcasted_iota(jnp.int32, small_ref.shape[1:], 1)
    in_rope = jnp.logical_and(lane >= KROPE_LANE, lane < KROPE_LANE + MLA_ROPE_DIM)
    kr = _rope3(jnp.where(in_rope, small_ref[0], 0.0), cos_t, sin_a, sin_b, MLA_ROPE_DIM // 2)
    for hd in range(MLA_HEADS):
        sl = slice(hd * MLA_HEAD_SLOT, (hd + 1) * MLA_HEAD_SLOT)
        q_ref[0, :, sl] = (_rope3(q[:, sl], cos_t, sin_a, sin_b, MLA_ROPE_DIM // 2) * MLA_Q_SCALE).astype(BF16)
        k_ref[0, :, sl] = (kn[:, sl] + kr).astype(BF16)


def _mla_proj(cq, ckv, small, cos_t, sin_a, sin_b, qn, kvn, wq, wk, wv):
    bsz, s, _ = cq.shape
    nt = s // TOKEN_TILE
    hw = MLA_HEADS * MLA_HEAD_SLOT
    vw = MLA_PAIRS * MLA_V_SLOT
    tok = lambda wd: pl.BlockSpec((1, TOKEN_TILE, wd), lambda b, t: (b, t, 0))
    tab = pl.BlockSpec((TOKEN_TILE, LANES), lambda b, t: (t, 0))
    full = lambda a: pl.BlockSpec(a.shape, lambda b, t: (0,) * a.ndim)
    return pl.pallas_call(
        _mla_proj_kernel,
        out_shape=[jax.ShapeDtypeStruct((bsz, s, hw), BF16),
                   jax.ShapeDtypeStruct((bsz, s, hw), BF16),
                   jax.ShapeDtypeStruct((bsz, s, vw), BF16)],
        grid=(bsz, nt),
        in_specs=[tok(MLA_Q_LORA), tok(MLA_KV_LORA), tok(LANES), tab, tab, tab,
                  full(qn), full(kvn), full(wq), full(wk), full(wv)],
        out_specs=[tok(hw), tok(hw), tok(vw)],
        compiler_params=_params("parallel", "arbitrary"),
        name="mla_proj",
    )(cq, ckv, small, cos_t, sin_a, sin_b, qn, kvn, wq, wk, wv)


ATTN_KEY_CHUNK = 256


def _attn_step(q_ref, k_ref, v_ref, g_ref, o_ref, slot_a, slot_b, n_keys):
    heads = range(2)
    hsl = [slice(hd * MLA_HEAD_SLOT, (hd + 1) * MLA_HEAD_SLOT) for hd in heads]
    if slot_a is not None:
        q = [q_ref[0, :, hsl[hd]] for hd in heads]
        mcol = [None, None]
    if slot_b is not None:
        mrow = [slot_b[1][hd] for hd in heads]
        pv = [None, None]
    for c in range(n_keys // ATTN_KEY_CHUNK):
        ks = slice(c * ATTN_KEY_CHUNK, (c + 1) * ATTN_KEY_CHUNK)
        for hd in heads:
            if slot_a is not None:
                s = _dot_nt(q[hd], k_ref[0, ks, hsl[hd]])
                slot_a[0][hd, :, ks] = s
                cm = s[:, :LANES]
                for w in range(1, ATTN_KEY_CHUNK // LANES):
                    cm = jnp.maximum(cm, s[:, w * LANES:(w + 1) * LANES])
                mcol[hd] = cm if c == 0 else jnp.maximum(mcol[hd], cm)
            if slot_b is not None:
                p = jnp.exp2(slot_b[0][hd, :, ks] - mrow[hd]).astype(BF16)
                part = _dot(p, v_ref[0, ks, :])
                pv[hd] = part if c == 0 else pv[hd] + part
    if slot_a is not None:
        for hd in heads:
            slot_a[1][hd] = jnp.max(mcol[hd], axis=-1, keepdims=True)
    if slot_b is not None:
        outs = [pv[hd][:, :LANES] / pv[hd][:, LANES:] for hd in heads]
        lane = lax.broadcasted_iota(jnp.int32, outs[0].shape, 1)
        o = jnp.where(lane < MLA_V_DIM, outs[0], outs[1])
        o_ref[0] = o * g_ref[0]


def _mla_attn_ctx_kernel(q_ref, k_ref, v_ref, g_ref, o_ref, s_ref, m_ref):
    n_keys = k_ref.shape[1]
    _attn_step(q_ref, k_ref, v_ref, g_ref, o_ref, (s_ref, m_ref), None, n_keys)
    _attn_step(q_ref, k_ref, v_ref, g_ref, o_ref, None, (s_ref, m_ref), n_keys)


def _mla_attn_ctx(q, k, v, gate, ctx_tiles):
    bsz, s, _ = q.shape
    tq = TOKEN_TILE
    n_ctx = ctx_tiles * tq
    pair_w = 2 * MLA_HEAD_SLOT
    return pl.pallas_call(
        _mla_attn_ctx_kernel,
        out_shape=jax.ShapeDtypeStruct((bsz, s, MLA_WIDTH), F32),
        grid=(bsz, MLA_PAIRS, ctx_tiles),
        in_specs=[pl.BlockSpec((1, tq, pair_w), lambda b, j, t: (b, t, j)),
                  pl.BlockSpec((1, n_ctx, pair_w), lambda b, j, t: (b, 0, j)),
                  pl.BlockSpec((1, n_ctx, MLA_V_SLOT), lambda b, j, t: (b, 0, j)),
                  pl.BlockSpec((1, tq, LANES), lambda b, j, t: (b, t, j))],
        out_specs=pl.BlockSpec((1, tq, LANES), lambda b, j, t: (b, t, j)),
        scratch_shapes=[pltpu.VMEM((2, tq, n_ctx), F32), pltpu.VMEM((2, tq, 1), F32)],
        compiler_params=_params("parallel", "arbitrary", "arbitrary"),
        name="mla_attn_ctx",
    )(q, k, v, gate)


def _mla_attn_lat_kernel(*refs, n_tiles, aliased):
    if aliased:
        refs = refs[1:]
    q_ref, k_ref, v_ref, g_ref, o_ref, s0, m0, s1, m1 = refs
    n_keys = k_ref.shape[1]
    slots = ((s0, m0), (s1, m1))
    n = pl.program_id(0)

    @pl.when(n == 0)
    def _():
        _attn_step(q_ref, k_ref, v_ref, g_ref, o_ref, slots[0], None, n_keys)

    for par in range(2):
        @pl.when(jnp.logical_and(jnp.logical_and(n > 0, n < n_tiles), n % 2 == par))
        def _():
            _attn_step(q_ref, k_ref, v_ref, g_ref, o_ref, slots[par], slots[1 - par], n_keys)

    @pl.when(n == n_tiles)
    def _():
        _attn_step(q_ref, k_ref, v_ref, g_ref, o_ref, None, slots[(n_tiles - 1) % 2], n_keys)


def _mla_attn_lat(q, k, v, gate, ctx_tiles, ctx_out=None):
    bsz, s, _ = q.shape
    tq = TOKEN_TILE
    nq = s // tq - ctx_tiles
    n_tiles = bsz * MLA_PAIRS * nq
    pair_w = 2 * MLA_HEAD_SLOT
    out_lo = 0 if ctx_out is not None else ctx_tiles

    def decode(i):
        return i // (MLA_PAIRS * nq), (i // nq) % MLA_PAIRS, i % nq + ctx_tiles

    def ahead(n):
        return decode(jnp.minimum(n, n_tiles - 1))

    def behind(n):
        return decode(jnp.maximum(n - 1, 0))

    def q_map(n):
        b, j, t = ahead(n)
        return b, t, j

    def k_map(n):
        b, j, _ = ahead(n)
        return b, 0, j

    def v_map(n):
        b, j, _ = behind(n)
        return b, 0, j

    def g_map(n):
        b, j, t = behind(n)
        return b, t, j

    def o_map(n):
        b, j, t = behind(n)
        return b, t - out_lo, j

    in_specs = [pl.BlockSpec((1, tq, pair_w), q_map),
                pl.BlockSpec((1, s, pair_w), k_map),
                pl.BlockSpec((1, s, MLA_V_SLOT), v_map),
                pl.BlockSpec((1, tq, LANES), g_map)]
    args = [q, k, v, gate]
    aliases = {}
    if ctx_out is not None:
        in_specs = [pl.BlockSpec(memory_space=pl.ANY)] + in_specs
        args = [ctx_out] + args
        aliases = {0: 0}
    kern = functools.partial(_mla_attn_lat_kernel, n_tiles=n_tiles, aliased=ctx_out is not None)
    return pl.pallas_call(
        kern,
        out_shape=jax.ShapeDtypeStruct((bsz, s - out_lo * tq, MLA_WIDTH), F32),
        grid=(n_tiles + 1,),
        in_specs=in_specs,
        out_specs=pl.BlockSpec((1, tq, LANES), o_map),
        scratch_shapes=[pltpu.VMEM((2, tq, s), F32), pltpu.VMEM((2, tq, 1), F32),
                        pltpu.VMEM((2, tq, s), F32), pltpu.VMEM((2, tq, 1), F32)],
        input_output_aliases=aliases,
        compiler_params=_params("arbitrary"),
        name="mla_attn",
    )(*args)


def _outproj_kernel(*refs, n_stream, ctx_tiles, t_lo, alpha):
    ssd_ref, mla_ref, ret_ref = refs[:3]
    x_refs = refs[3:3 + n_stream]
    mod_ref, w_ref, g_ref, b_ref, o_ref = refs[3 + n_stream:]
    acc = _dot(ssd_ref[0].astype(BF16), w_ref[:SSD_WIDTH, :])
    acc = acc + _dot(mla_ref[0].astype(BF16), w_ref[SSD_WIDTH:SSD_WIDTH + MLA_WIDTH, :])
    acc = acc + _dot(ret_ref[0].astype(BF16), w_ref[SSD_WIDTH + MLA_WIDTH:, :])
    gate = mod_ref[0, 0][:, 2 * D_MODEL:]
    x = _stream_tile(x_refs, pl.program_id(1) + t_lo, ctx_tiles)
    r = alpha * x + gate * acc
    mu = jnp.mean(r, axis=-1, keepdims=True)
    var = jnp.mean(jnp.square(r - mu), axis=-1, keepdims=True)
    o_ref[0] = (r - mu) * lax.rsqrt(var + LN_EPS) * g_ref[...] + b_ref[...]


def _outproj(ssd_o, mla_o, mla_lo, ret_o, stream, mod_all, w, ln_g, ln_b, t_lo, ctx_tiles, alpha):
    bsz, s, _ = ssd_o.shape
    d = stream[0].shape[-1]
    nt = s // TOKEN_TILE
    tok = lambda wd: pl.BlockSpec((1, TOKEN_TILE, wd), lambda b, t: (b, t + t_lo, 0))
    kern = functools.partial(_outproj_kernel, n_stream=len(stream), ctx_tiles=ctx_tiles, t_lo=t_lo, alpha=alpha)
    return pl.pallas_call(
        kern,
        out_shape=jax.ShapeDtypeStruct((bsz, s - t_lo * TOKEN_TILE, d), F32),
        grid=(bsz, nt - t_lo),
        in_specs=[tok(SSD_WIDTH),
                  pl.BlockSpec((1, TOKEN_TILE, MLA_WIDTH), lambda b, t: (b, t + t_lo - mla_lo, 0)),
                  tok(RET_WIDTH)] + _stream_specs(stream, t_lo, ctx_tiles) + [
                  pl.BlockSpec((1, 1, 1, 3 * d),
                               lambda b, t: (b, jnp.minimum(t + t_lo, ctx_tiles) // ctx_tiles, 0, 0)),
                  pl.BlockSpec(w.shape, lambda b, t: (0, 0)),
                  pl.BlockSpec((1, d), lambda b, t: (0, 0)),
                  pl.BlockSpec((1, d), lambda b, t: (0, 0))],
        out_specs=pl.BlockSpec((1, TOKEN_TILE, d), lambda b, t: (b, t, 0)),
        compiler_params=_params("parallel", "arbitrary"),
        name="out_proj",
    )(ssd_o, mla_o, ret_o, *stream, mod_all, w, ln_g.reshape(1, d), ln_b.reshape(1, d))


def _rope_tables(n_ctx, n_lat, rot_dim, width, lane0, period):
    t = np.arange(n_lat)
    row = (t // GRID_W).astype(np.float32)
    col = (t % GRID_W).astype(np.float32)
    n_freq = rot_dim // 4
    inv = (ROPE_THETA ** (-np.arange(n_freq, dtype=np.float32) / n_freq)).astype(np.float32)
    ang = np.concatenate([row[:, None] * inv, col[:, None] * inv], -1).astype(np.float32)
    cos, sin = np.cos(ang), np.sin(ang)
    half = rot_dim // 2
    lane = np.arange(width) % period - lane0
    in_rot = (lane >= 0) & (lane < rot_dim)
    idx = np.where(in_rot, lane % half, 0)
    first = in_rot & (lane < half)
    second = in_rot & (lane >= half)
    cos_t = np.where(in_rot[None, :], cos[:, idx], 1.0)
    sin_a = np.where(first[None, :], -sin[:, idx], 0.0)
    sin_b = np.where(second[None, :], sin[:, idx], 0.0)
    ctx_pad = lambda a, v: np.concatenate([np.full((n_ctx, width), v), a], axis=0).astype(np.float32)
    return ctx_pad(cos_t, 1.0), ctx_pad(sin_a, 0.0), ctx_pad(sin_b, 0.0)


def _mla_weights(w_uq, w_ukv):
    qd = MLA_NOPE_DIM + MLA_ROPE_DIM
    wq = w_uq.reshape(MLA_Q_LORA, MLA_HEADS, qd)
    wq = jnp.pad(wq, ((0, 0), (0, 0), (0, MLA_HEAD_SLOT - qd))).reshape(MLA_Q_LORA, MLA_HEADS * MLA_HEAD_SLOT)
    wkv = w_ukv.reshape(MLA_KV_LORA, MLA_HEADS, MLA_NOPE_DIM + MLA_V_DIM)
    wk = jnp.pad(wkv[..., :MLA_NOPE_DIM], ((0, 0), (0, 0), (0, MLA_HEAD_SLOT - MLA_NOPE_DIM)))
    wk = wk.reshape(MLA_KV_LORA, MLA_HEADS * MLA_HEAD_SLOT)
    wv = wkv[..., MLA_NOPE_DIM:].reshape(MLA_KV_LORA, MLA_PAIRS, 2 * MLA_V_DIM)
    wv = jnp.pad(wv, ((0, 0), (0, 0), (0, MLA_V_SLOT - 2 * MLA_V_DIM))).reshape(MLA_KV_LORA, MLA_PAIRS * MLA_V_SLOT)
    return wq.astype(BF16), wk.astype(BF16), wv.astype(BF16)


def _lane_pad(v):
    return jnp.pad(v, (0, LANES - v.shape[0])).reshape(1, LANES)


def kernel(x, c, ctx, c_ctx, w_ada, b_ada, w_in, ssd_conv_w, ssd_conv_b, ssd_a_log_f, ssd_a_log_b,
           ssd_dt_bias_f, ssd_dt_bias_b, ssd_d, ssd_norm_w, mla_q_norm, mla_w_uq, mla_kv_norm, mla_w_ukv,
           ret_log_rate_f, ret_log_rate_b, w_out, ln_g, ln_b):
    bsz, n_lat, d = x.shape
    n_ctx = ctx.shape[1]
    depth = w_in.shape[0]
    assert d == D_MODEL and n_ctx == TOKEN_TILE and n_lat % TOKEN_TILE == 0
    alpha = (2 * depth) ** 0.25
    s = n_ctx + n_lat
    nc_ctx = n_ctx // CHUNK
    ctx_tiles = n_ctx // TOKEN_TILE
    nt = s // TOKEN_TILE

    cos_m, sa_m, sb_m = _rope_tables(n_ctx, n_lat, MLA_ROPE_DIM, LANES, MLA_ROPE_LANE, LANES)
    cos_r, sa_r, sb_r = _rope_tables(n_ctx, n_lat, RET_QK_DIM, RET_QK_WIDTH, 0, RET_QK_DIM)

    rows = 16
    cc = jnp.zeros((rows, d), F32).at[:bsz].set(c).at[bsz].set(c_ctx)
    stream = (ctx, x)

    for i in range(depth):
        last = i == depth - 1
        mod = _ada(cc, w_ada[i], b_ada[i])
        mod_all = jnp.stack([jnp.broadcast_to(mod[bsz], (bsz, 3 * d)), mod[:bsz]], axis=1)
        mod_all = mod_all.reshape(bsz, 2, 1, 3 * d)
        both = lambda f, b: jnp.stack([f, b], axis=0)
        head_rows = lambda v: jnp.broadcast_to(v[:, None], (SSD_HEADS, TOKEN_TILE))
        alogt = both(head_rows(ssd_a_log_f[i]), head_rows(ssd_a_log_b[i]))
        dtbt = both(head_rows(ssd_dt_bias_f[i]), head_rows(ssd_dt_bias_b[i]))
        (zs, xs_bf, xst_bf, bc, colp, rowp, small, cq, ckv, mgs, rq, rk_bf, rkt, rv_bf, rgs) = _inproj(
            stream, mod_all, _relayout_w_in(w_in[i]), ssd_conv_w[i], ssd_conv_b[i].reshape(1, -1), alogt, dtbt,
            cos_r, sa_r, sb_r, ctx_tiles)

        dwide = jnp.repeat(ssd_d[i], SSD_HEAD_DIM).reshape(1, SSD_WIDTH)
        rates = both(_lane_pad(ret_log_rate_f[i]), _lane_pad(ret_log_rate_b[i]))
        prep = (xs_bf, xst_bf, bc, colp, rowp, rq, rk_bf, rkt, rv_bf)
        ssd_o, ret_o = _mixer_scan(prep, zs, rgs, dwide, ssd_norm_w[i].reshape(1, -1), _ret_consts(rates), nc_ctx)

        wq, wk, wv = _mla_weights(mla_w_uq[i], mla_w_ukv[i])
        q, k, v = _mla_proj(cq, ckv, small, cos_m, sa_m, sb_m, mla_q_norm[i].reshape(1, -1),
                            mla_kv_norm[i].reshape(1, -1), wq, wk, wv)
        mla_lo = ctx_tiles if last else 0
        mla_o = _mla_attn_lat(q, k, v, mgs, ctx_tiles, None if last else _mla_attn_ctx(q, k, v, mgs, ctx_tiles))

        out = _outproj(ssd_o, mla_o, mla_lo, ret_o, stream, mod_all, w_out[i].astype(BF16), ln_g[i], ln_b[i],
                       ctx_tiles if last else 0, ctx_tiles, alpha)
        stream = (out,)
    return stream[0]
```

```python
import functools
import math

import jax
import jax.numpy as jnp
import numpy as np
from jax import lax
from jax.experimental import pallas as pl
from jax.experimental.pallas import tpu as pltpu

F32 = jnp.float32
BF16 = jnp.bfloat16

LANES = 128
SUBLANES = 8
VMEM_LIMIT_BYTES = 56 * 1024 * 1024

GRID_W = 64
D_MODEL = 1024
SSD_WIDTH = 1024
SSD_HEAD_DIM = 64
SSD_HEADS = 16
SSD_GROUPS = 2
SSD_STATE = 128
SSD_CONV = 5
SSD_CONV_DIM = SSD_WIDTH + 2 * SSD_GROUPS * SSD_STATE
MLA_WIDTH = 512
MLA_V_DIM = 64
MLA_HEADS = 8
MLA_NOPE_DIM = 64
MLA_ROPE_DIM = 32
MLA_Q_LORA = 384
MLA_KV_LORA = 256
MLA_SCALE = (MLA_NOPE_DIM + MLA_ROPE_DIM) ** -0.5
RET_WIDTH = 512
RET_HEADS = 4
RET_V_DIM = 128
RET_QK_DIM = 64
RET_QK_WIDTH = RET_HEADS * RET_QK_DIM
CHUNK = 128
ROPE_THETA = 10000.0
LN_EPS = 1e-5
RMS_EPS = 1e-6
IN_WIDTHS = (SSD_WIDTH, SSD_CONV_DIM, SSD_HEADS, MLA_Q_LORA, MLA_KV_LORA, MLA_ROPE_DIM, MLA_WIDTH,
             RET_QK_WIDTH, RET_QK_WIDTH, RET_WIDTH, RET_WIDTH)

TOKEN_TILE = 256
MLA_HEAD_SLOT = 128
MLA_ROPE_LANE = MLA_NOPE_DIM
MLA_PAIRS = MLA_HEADS // 2
MLA_V_SLOT = 256
MLA_Q_SCALE = MLA_SCALE * math.log2(math.e)
NEG_BIG = -1e30


def _params(*sem):
    return pltpu.CompilerParams(dimension_semantics=sem, vmem_limit_bytes=VMEM_LIMIT_BYTES)


def _silu(v):
    return v * jax.nn.sigmoid(v)


def _softplus(v):
    return jnp.maximum(v, 0.0) + jnp.log1p(jnp.exp(-jnp.abs(v)))


def _dot(a, b):
    return jnp.dot(a, b, preferred_element_type=F32)


def _dot_exact(a, b):
    return jnp.dot(a, b, preferred_element_type=F32, precision=lax.Precision.HIGHEST)


def _dot_nt(a, b):
    return lax.dot_general(a, b, (((1,), (1,)), ((), ())), preferred_element_type=F32)


def _ada_kernel(c_ref, w_ref, b_ref, o_ref):
    o_ref[...] = _dot_exact(_silu(c_ref[...]), w_ref[0]) + b_ref[0]


def _ada(cc, w_all, b_all, layer):
    rows, d = cc.shape
    depth, _, n = w_all.shape
    tn = 1024
    return pl.pallas_call(
        _ada_kernel,
        out_shape=jax.ShapeDtypeStruct((rows, n), F32),
        grid=(n // tn,),
        in_specs=[pl.BlockSpec((rows, d), lambda j: (0, 0)),
                  pl.BlockSpec((1, d, tn), lambda j: (layer, 0, j)),
                  pl.BlockSpec((1, 1, tn), lambda j: (layer, 0, j))],
        out_specs=pl.BlockSpec((rows, tn), lambda j: (0, j)),
        compiler_params=_params("arbitrary"),
        name="ada_mod",
    )(cc, w_all, b_all.reshape(depth, 1, n))


def _chunk_index(d, c, nc_ctx, nc):
    bwd = jnp.where(c < nc_ctx, nc_ctx - 1 - c, nc + nc_ctx - 1 - c)
    return jnp.where(d == 0, c, bwd)


def _tri_masks(d):
    ii = lax.broadcasted_iota(jnp.int32, (CHUNK, CHUNK), 0)
    jj = lax.broadcasted_iota(jnp.int32, (CHUNK, CHUNK), 1)
    sgn = 1 - 2 * d
    mask = (ii - jj) * sgn >= 0
    mask_t = (jj - ii) * sgn >= 0
    return mask, mask_t


def _rope3(v, cos_t, sin_a, sin_b, shift):
    width = v.shape[-1]
    return v * cos_t + pltpu.roll(v, width - shift, 1) * sin_a + pltpu.roll(v, shift, 1) * sin_b


PROJ_WIDTHS = (SSD_WIDTH, SSD_CONV_DIM, LANES, MLA_Q_LORA, MLA_KV_LORA, MLA_WIDTH,
               RET_QK_WIDTH, RET_QK_WIDTH, RET_WIDTH, RET_WIDTH)
PROJ_OFFSETS = tuple(int(o) for o in np.cumsum((0,) + PROJ_WIDTHS[:-1]))
DT_LANE = 0
KROPE_LANE = MLA_ROPE_LANE
ROW_CUM, ROW_DT, ROW_WEND, ROW_ETOT = 0, 16, 32, 48
ROW_PACK = 64
TILE_CHUNKS = TOKEN_TILE // CHUNK


IN_OFFSETS = tuple(int(o) for o in np.cumsum((0,) + IN_WIDTHS[:-1]))
RELAYOUT_MOVES = ((0, 0, 0), (1, 1, 0), (2, 2, DT_LANE), (3, 3, 0), (4, 4, 0), (5, 2, KROPE_LANE), (6, 5, 0),
                  (7, 6, 0), (8, 7, 0), (9, 8, 0), (10, 9, 0))
RELAYOUT_ROWS = 128


def _relayout_w_in_kernel(w_ref, o_ref):
    small = PROJ_OFFSETS[2]
    o_ref[:, small:small + LANES] = jnp.zeros((o_ref.shape[0], LANES), BF16)
    for src, dst, lane in RELAYOUT_MOVES:
        lo = PROJ_OFFSETS[dst] + lane
        o_ref[:, lo:lo + IN_WIDTHS[src]] = w_ref[0, :, IN_OFFSETS[src]:IN_OFFSETS[src] + IN_WIDTHS[src]].astype(BF16)


def _relayout_w_in(w_in_all, layer):
    _, d, n_in = w_in_all.shape
    n_out = sum(PROJ_WIDTHS)
    return pl.pallas_call(
        _relayout_w_in_kernel,
        out_shape=jax.ShapeDtypeStruct((d, n_out), BF16),
        grid=(d // RELAYOUT_ROWS,),
        in_specs=[pl.BlockSpec((1, RELAYOUT_ROWS, n_in), lambda r: (layer, r, 0))],
        out_specs=pl.BlockSpec((RELAYOUT_ROWS, n_out), lambda r: (r, 0)),
        compiler_params=_params("arbitrary"),
        name="w_in_relayout",
    )(w_in_all)


def _stream_specs(stream, t_lo, ctx_tiles):
    d = stream[0].shape[-1]
    if len(stream) == 1:
        return [pl.BlockSpec((1, TOKEN_TILE, d), lambda b, t: (b, t + t_lo, 0))]
    return [pl.BlockSpec((1, TOKEN_TILE, d), lambda b, t: (b, jnp.minimum(t + t_lo, ctx_tiles - 1), 0)),
            pl.BlockSpec((1, TOKEN_TILE, d), lambda b, t: (b, jnp.maximum(t + t_lo - ctx_tiles, 0), 0))]


def _halo_specs(stream, ctx_tiles):
    arr = stream[-1]
    d = arr.shape[-1]
    shift = 0 if len(stream) == 1 else ctx_tiles
    per_tile = TOKEN_TILE // SUBLANES
    nblk = arr.shape[1] // SUBLANES
    prev = lambda b, t: (b, jnp.clip((t - shift) * per_tile - 1, 0, nblk - 1), 0)
    nxt = lambda b, t: (b, jnp.clip((t - shift + 1) * per_tile, 0, nblk - 1), 0)
    return [pl.BlockSpec((1, SUBLANES, d), prev), pl.BlockSpec((1, SUBLANES, d), nxt)]


def _stream_tile(refs, t, ctx_tiles):
    if len(refs) == 1:
        return refs[0][0]
    return jnp.where(t < ctx_tiles, refs[0][0], refs[1][0])


def _inproj_kernel(*refs, n_stream, ctx_tiles, nt):
    x_refs = refs[:n_stream]
    (prev_ref, next_ref, mod_ref, w_ref, convw_ref, convb_ref, alogt_ref, dtbt_ref,
     cos_ref, sa_ref, sb_ref) = refs[n_stream:n_stream + 11]
    (zs_ref, xs_ref, xst_ref, bc_ref, colp_ref, rowp_ref, small_ref, cq_ref, ckv_ref, mgs_ref,
     q_ref, kbf_ref, kt_ref, v_ref, rgs_ref) = refs[n_stream + 11:]
    t = pl.program_id(1)
    mod = mod_ref[0, 0]
    sh = mod[:, :D_MODEL]
    sc = mod[:, D_MODEL:2 * D_MODEL]
    modulate = lambda v: (v * (1.0 + sc) + sh).astype(BF16)
    h = modulate(_stream_tile(x_refs, t, ctx_tiles))
    h_ext = jnp.concatenate([modulate(prev_ref[0]), h, modulate(next_ref[0])], axis=0)

    def proj(g, lo=0, width=None, lhs=None):
        off = PROJ_OFFSETS[g] + lo
        width = PROJ_WIDTHS[g] - lo if width is None else width
        return _dot(h if lhs is None else lhs, w_ref[:, off:off + width])

    first = jnp.logical_or(t == 0, t == ctx_tiles)
    last = jnp.logical_or(t == ctx_tiles - 1, t == nt - 1)
    row = lax.broadcasted_iota(jnp.int32, (TOKEN_TILE + 2 * SUBLANES, 1), 0)
    keep = jnp.where(jnp.logical_or(jnp.logical_and(row < SUBLANES, first),
                                    jnp.logical_and(row >= SUBLANES + TOKEN_TILE, last)), 0.0, 1.0)
    unit_w = 2 * LANES
    ext_parts = {}

    def ext_unit(e):
        ext_parts[e] = proj(1, e * unit_w, unit_w, lhs=h_ext) * keep

    def conv_strip(j, anchor):
        pad = SSD_CONV // 2
        sl = slice(j * LANES, (j + 1) * LANES)
        esl = slice((j % 2) * LANES, (j % 2 + 1) * LANES)
        ext = ext_parts[j // 2]
        acc = convb_ref[:, sl] + anchor
        for k in range(SSD_CONV):
            lo = SUBLANES - pad + k
            acc = acc + convw_ref[k:k + 1, sl] * ext[lo:lo + TOKEN_TILE, esl]
        u = _silu(acc)
        if j < SSD_WIDTH // LANES:
            xs_ref[0, :, sl] = u.astype(BF16)
            xst_ref[0, sl, :] = u.T.astype(BF16)
        else:
            bc_ref[0, :, j * LANES - SSD_WIDTH:(j + 1) * LANES - SSD_WIDTH] = u

    def gate_unit(g, ref, lo, width):
        def run():
            val = proj(g, lo, width)
            ref[0, :, lo:lo + width] = _silu(val)
            return val
        return run

    def plain_unit(g, ref, cast=None):
        def run():
            val = proj(g)
            ref[0] = val if cast is None else val.astype(cast)
            return val
        return run

    def ret_q_unit():
        val = proj(6)
        q_ref[0] = _rope3(val, cos_ref[...], sa_ref[...], sb_ref[...], RET_QK_DIM // 2)
        return val

    def ret_k_unit():
        val = proj(7)
        kk = _rope3(val, cos_ref[...], sa_ref[...], sb_ref[...], RET_QK_DIM // 2) * (RET_QK_DIM ** -0.5)
        kbf_ref[0] = kk.astype(BF16)
        kt_ref[0] = kk.T
        return val

    units = [gate_unit(0, zs_ref, lo, unit_w) for lo in range(0, SSD_WIDTH, unit_w)]
    units += [plain_unit(3, cq_ref), plain_unit(4, ckv_ref)]
    units += [gate_unit(5, mgs_ref, lo, unit_w) for lo in range(0, MLA_WIDTH, unit_w)]
    units += [ret_q_unit, ret_k_unit, plain_unit(8, v_ref, BF16)]
    units += [gate_unit(9, rgs_ref, lo, unit_w) for lo in range(0, RET_WIDTH, unit_w)]
    small = proj(2)
    small_ref[0] = small
    n_strips = SSD_CONV_DIM // LANES
    ext_unit(0)
    anchor = jnp.zeros((1, LANES), F32)
    for j in range(n_strips):
        if j % 2 == 0 and j // 2 + 1 < n_strips // 2:
            ext_unit(j // 2 + 1)
        nxt = units.pop(0)()[0:1, :LANES] * 0.0 if units else anchor
        conv_strip(j, anchor)
        anchor = nxt
    for unit in units:
        unit()

    dt_raw_t = small.T[DT_LANE:DT_LANE + SSD_HEADS, :]
    pad_rows = jnp.zeros((CHUNK - SSD_HEADS, CHUNK), F32)
    for d in range(2):
        _, mask_t = _tri_masks(d)
        tri_t = jnp.where(mask_t, 1.0, 0.0)
        dt_all = _softplus(dt_raw_t + dtbt_ref[d])
        la_all = dt_all * -jnp.exp(alogt_ref[d])
        base = d * ROW_PACK
        for j in range(TILE_CHUNKS):
            cs = slice(j * CHUNK, (j + 1) * CHUNK)
            dt_row, la_row = dt_all[:, cs], la_all[:, cs]
            cum_row = _dot_exact(la_row, tri_t)
            colp_ref[0, d, cs, :] = jnp.concatenate([cum_row, pad_rows], axis=0).T
            tot_row = jnp.sum(la_row, axis=1, keepdims=True)
            rowp_ref[0, j, base + ROW_CUM:base + ROW_CUM + SSD_HEADS, :] = cum_row
            rowp_ref[0, j, base + ROW_DT:base + ROW_DT + SSD_HEADS, :] = dt_row
            rowp_ref[0, j, base + ROW_WEND:base + ROW_WEND + SSD_HEADS, :] = dt_row * jnp.exp(tot_row - cum_row)
            rowp_ref[0, j, base + ROW_ETOT:base + ROW_ETOT + SSD_HEADS, :] = jnp.broadcast_to(
                jnp.exp(tot_row), (SSD_HEADS, CHUNK))


def _inproj(stream, mod_all, w, conv_w, conv_b, alogt, dtbt, cos_t, sin_a, sin_b, ctx_tiles):
    bsz, d = stream[0].shape[0], stream[0].shape[-1]
    s = sum(a.shape[1] for a in stream)
    nt = s // TOKEN_TILE
    nc = s // CHUNK
    tok = lambda wd: pl.BlockSpec((1, TOKEN_TILE, wd), lambda b, t: (b, t, 0))
    chan = lambda wd: pl.BlockSpec((1, wd, TOKEN_TILE), lambda b, t: (b, 0, t))
    full = lambda a: pl.BlockSpec(a.shape, lambda b, t: (0,) * a.ndim)
    tab = pl.BlockSpec((TOKEN_TILE, RET_QK_WIDTH), lambda b, t: (t, 0))
    bcw = SSD_CONV_DIM - SSD_WIDTH
    outs = [((bsz, s, SSD_WIDTH), F32, tok(SSD_WIDTH)),
            ((bsz, s, SSD_WIDTH), BF16, tok(SSD_WIDTH)),
            ((bsz, SSD_WIDTH, s), BF16, chan(SSD_WIDTH)),
            ((bsz, s, bcw), F32, tok(bcw)),
            ((bsz, 2, s, LANES), F32, pl.BlockSpec((1, 2, TOKEN_TILE, LANES), lambda b, t: (b, 0, t, 0))),
            ((bsz, nc, 2 * ROW_PACK, CHUNK), F32,
             pl.BlockSpec((1, TILE_CHUNKS, 2 * ROW_PACK, CHUNK), lambda b, t: (b, t, 0, 0))),
            ((bsz, s, LANES), F32, tok(LANES)),
            ((bsz, s, MLA_Q_LORA), F32, tok(MLA_Q_LORA)),
            ((bsz, s, MLA_KV_LORA), F32, tok(MLA_KV_LORA)),
            ((bsz, s, MLA_WIDTH), F32, tok(MLA_WIDTH)),
            ((bsz, s, RET_QK_WIDTH), F32, tok(RET_QK_WIDTH)),
            ((bsz, s, RET_QK_WIDTH), BF16, tok(RET_QK_WIDTH)),
            ((bsz, RET_QK_WIDTH, s), F32, chan(RET_QK_WIDTH)),
            ((bsz, s, RET_WIDTH), BF16, tok(RET_WIDTH)),
            ((bsz, s, RET_WIDTH), F32, tok(RET_WIDTH))]
    kern = functools.partial(_inproj_kernel, n_stream=len(stream), ctx_tiles=ctx_tiles, nt=nt)
    return pl.pallas_call(
        kern,
        out_shape=[jax.ShapeDtypeStruct(shape, dt) for shape, dt, _ in outs],
        grid=(bsz, nt),
        in_specs=_stream_specs(stream, 0, ctx_tiles) + _halo_specs(stream, ctx_tiles) + [
            pl.BlockSpec((1, 1, 1, 3 * d), lambda b, t: (b, jnp.minimum(t, ctx_tiles) // ctx_tiles, 0, 0)),
            pl.BlockSpec(w.shape, lambda b, t: (0, 0), pipeline_mode=pl.Buffered(1)),
            full(conv_w), full(conv_b), full(alogt), full(dtbt), tab, tab, tab],
        out_specs=[spec for _, _, spec in outs],
        compiler_params=_params("parallel", "arbitrary"),
        name="in_proj",
    )(*stream, stream[-1], stream[-1], mod_all, w, conv_w, conv_b, alogt, dtbt, cos_t, sin_a, sin_b)


def _ret_consts_kernel(rate_ref, dec_ref, ecum_ref, wend_ref, etot_ref):
    d = pl.program_id(0)
    mask, _ = _tri_masks(d)
    ii = lax.broadcasted_iota(jnp.int32, (CHUNK, CHUNK), 0)
    jj = lax.broadcasted_iota(jnp.int32, (CHUNK, CHUNK), 1)
    dist = jnp.abs(ii - jj).astype(F32)
    pos_i = lax.broadcasted_iota(jnp.int32, (CHUNK, RET_QK_DIM), 0)
    steps_col = jnp.where(d == 0, pos_i + 1, CHUNK - pos_i).astype(F32)
    pos_j = lax.broadcasted_iota(jnp.int32, (RET_QK_DIM, CHUNK), 1)
    left_row = jnp.where(d == 0, CHUNK - 1 - pos_j, pos_j).astype(F32)
    for hd in range(RET_HEADS):
        la = -jnp.exp(rate_ref[0, :, hd:hd + 1])
        dec_ref[0, hd] = jnp.exp(jnp.where(mask, dist * la, NEG_BIG))
        ecum_ref[0, :, hd * RET_QK_DIM:(hd + 1) * RET_QK_DIM] = jnp.exp(steps_col * la)
        wend_ref[0, hd * RET_QK_DIM:(hd + 1) * RET_QK_DIM, :] = jnp.exp(left_row * la)
        etot_ref[0, hd] = jnp.broadcast_to(jnp.exp(CHUNK * la), (1, LANES))


def _ret_consts(rates):
    return pl.pallas_call(
        _ret_consts_kernel,
        out_shape=[jax.ShapeDtypeStruct((2, RET_HEADS, CHUNK, CHUNK), F32),
                   jax.ShapeDtypeStruct((2, CHUNK, RET_QK_WIDTH), F32),
                   jax.ShapeDtypeStruct((2, RET_QK_WIDTH, CHUNK), F32),
                   jax.ShapeDtypeStruct((2, RET_HEADS, 1, LANES), F32)],
        grid=(2,),
        in_specs=[pl.BlockSpec((1, 1, LANES), lambda d: (d, 0, 0))],
        out_specs=[pl.BlockSpec((1, RET_HEADS, CHUNK, CHUNK), lambda d: (d, 0, 0, 0)),
                   pl.BlockSpec((1, CHUNK, RET_QK_WIDTH), lambda d: (d, 0, 0)),
                   pl.BlockSpec((1, RET_QK_WIDTH, CHUNK), lambda d: (d, 0, 0)),
                   pl.BlockSpec((1, RET_HEADS, 1, LANES), lambda d: (d, 0, 0, 0))],
        compiler_params=_params("arbitrary"),
        name="ret_consts",
    )(rates)


SCAN_BATCH = 2


def _mixer_scan_kernel(xs_ref, xst_ref, bc_ref, colp_ref, rowp_ref, z_ref, dwide_ref, normw_ref,
                       q_ref, kbf_ref, kt_ref, v_ref, g_ref, dec_ref, ecum_ref, wend_ref, etot_ref,
                       ssd_out_ref, ret_out_ref, yf_ref, yr_ref, ht_ref, hr_ref, *, nc_ctx, nc):
    d = pl.program_id(1)
    c = pl.program_id(2)
    cidx = _chunk_index(d, c, nc_ctx, nc)
    chains = range(SCAN_BATCH)

    @pl.when(c == 0)
    def _():
        ht_ref[...] = jnp.zeros_like(ht_ref)
        hr_ref[...] = jnp.zeros_like(hr_ref)

    mask, _ = _tri_masks(d)
    lane = lax.broadcasted_iota(jnp.int32, (CHUNK, LANES), 1)
    sub = lax.broadcasted_iota(jnp.int32, (CHUNK, LANES), 0)
    lane_lo = lane < SSD_HEAD_DIM
    sub_lo = sub < SSD_HEAD_DIM
    heads_per_group = SSD_HEADS // SSD_GROUPS

    ret = []
    for bb in chains:
        q = q_ref[bb]
        ret.append(dict(q_bf=q.astype(BF16), k_bf=kbf_ref[bb], qe_bf=(q * ecum_ref[0]).astype(BF16),
                        ktw_bf=(kt_ref[bb] * wend_ref[0]).astype(BF16), v_bf=v_ref[bb], lhs={}, ys=[]))

    def ret_scores(bb, hd):
        r = ret[bb]
        qs = slice(hd * RET_QK_DIM, (hd + 1) * RET_QK_DIM)
        sc = _dot_nt(r["q_bf"][:, qs], r["k_bf"][:, qs]) * dec_ref[0, hd]
        r["lhs"][hd] = jnp.concatenate([sc.astype(BF16), r["qe_bf"][:, qs]], axis=1)

    def ret_update(bb, hd):
        r = ret[bb]
        qs = slice(hd * RET_QK_DIM, (hd + 1) * RET_QK_DIM)
        v_h = r["v_bf"][:, hd * RET_V_DIM:(hd + 1) * RET_V_DIM]
        h_h = hr_ref[bb, hd]
        rhs = jnp.concatenate([v_h, h_h.astype(BF16)], axis=0)
        r["ys"].append(_dot(r["lhs"][hd], rhs))
        hr_ref[bb, hd] = etot_ref[0, hd] * h_h + _dot(r["ktw_bf"][qs, :], v_h)

    ssd = []
    for bb in chains:
        rowp = rowp_ref[bb, 0]
        ssd.append(dict(xs_bf=xs_ref[bb], bc=bc_ref[bb], cum_col=colp_ref[bb, 0],
                        cum_row=rowp[ROW_CUM:ROW_CUM + SSD_HEADS], dt_row=rowp[ROW_DT:ROW_DT + SSD_HEADS],
                        wend_row=rowp[ROW_WEND:ROW_WEND + SSD_HEADS],
                        etot_row=rowp[ROW_ETOT:ROW_ETOT + SSD_HEADS], gmats={}, ys=[]))

    def ssd_pair(bb, k):
        s = ssd[bb]
        g = (2 * k) // heads_per_group
        if g not in s["gmats"]:
            b_g = s["bc"][:, g * SSD_STATE:(g + 1) * SSD_STATE].astype(BF16)
            c_g = s["bc"][:, (SSD_GROUPS + g) * SSD_STATE:(SSD_GROUPS + g + 1) * SSD_STATE]
            s["gmats"][g] = (_dot_nt(c_g.astype(BF16), b_g), c_g, b_g)
        scores, c_g, b_g = s["gmats"][g]
        ha, hb = 2 * k, 2 * k + 1
        xs_pair = s["xs_bf"][:, k * LANES:(k + 1) * LANES]
        ht_pair = ht_ref[bb, k]
        rhs = jnp.concatenate([xs_pair, ht_pair.T.astype(BF16)], axis=0)
        y_heads = []
        for hd in (ha, hb):
            cb = jnp.broadcast_to(s["cum_col"][:, hd:hd + 1], (CHUNK, CHUNK))
            seg = cb - s["cum_row"][hd:hd + 1, :]
            dec = jnp.exp(jnp.where(mask, seg, NEG_BIG))
            m = scores * dec * s["dt_row"][hd:hd + 1, :]
            lhs = jnp.concatenate([m, c_g * jnp.exp(cb)], axis=1).astype(BF16)
            y_heads.append(_dot(lhs, rhs))
        s["ys"].append(jnp.where(lane_lo, y_heads[0], y_heads[1]))
        w_pair = jnp.where(sub_lo, s["wend_row"][ha:ha + 1, :], s["wend_row"][hb:hb + 1, :])
        e_pair = jnp.where(sub_lo, s["etot_row"][ha:ha + 1, :], s["etot_row"][hb:hb + 1, :])
        xw = (xst_ref[bb, k * LANES:(k + 1) * LANES, :].astype(F32) * w_pair).astype(BF16)
        ht_ref[bb, k] = e_pair * ht_pair + _dot(xw, b_g)

    for k in range(SSD_HEADS // 2):
        for bb in chains:
            ssd_pair(bb, k)
        for bb in chains:
            if k % 2 == 0:
                ret_scores(bb, k // 2)
            else:
                ret_update(bb, k // 2)
    y = [jnp.concatenate(ssd[bb]["ys"], axis=1) for bb in chains]
    yr = [jnp.concatenate(ret[bb]["ys"], axis=1) for bb in chains]

    @pl.when(d == 0)
    def _():
        for bb in chains:
            yf_ref[bb, cidx] = y[bb].astype(yf_ref.dtype)
            yr_ref[bb, cidx] = yr[bb].astype(yr_ref.dtype)

    @pl.when(d == 1)
    def _():
        for bb in chains:
            tot = yf_ref[bb, cidx].astype(F32) + y[bb] + ssd[bb]["xs_bf"].astype(F32) * dwide_ref[...]
            gated = tot * z_ref[bb]
            ms = jnp.mean(gated * gated, axis=-1, keepdims=True)
            ssd_out_ref[bb] = gated * lax.rsqrt(ms + RMS_EPS) * normw_ref[...]
            rtot = yr_ref[bb, cidx].astype(F32) + yr[bb]
            outs = []
            for hd in range(RET_HEADS):
                t_h = rtot[:, hd * RET_V_DIM:(hd + 1) * RET_V_DIM]
                mu = jnp.mean(t_h, axis=-1, keepdims=True)
                var = jnp.mean(jnp.square(t_h - mu), axis=-1, keepdims=True)
                outs.append((t_h - mu) * lax.rsqrt(var + LN_EPS))
            ret_out_ref[bb] = jnp.concatenate(outs, axis=1) * g_ref[bb]


def _mixer_scan(prep, z, rg, dwide, normw, rconsts, nc_ctx):
    xs_bf, xst_bf, bc, colp, rowp, rq, rk_bf, rkt, rv_bf = prep
    dec, ecum, wend, etot = rconsts
    bsz, s, _ = z.shape
    nc = s // CHUNK
    nb = SCAN_BATCH
    first_bwd = nc_ctx - 1

    def cix(d, c):
        return _chunk_index(d, c, nc_ctx, nc)

    def late(d, c):
        return jnp.where(d == 0, first_bwd, cix(d, c))

    tokc = lambda wd: pl.BlockSpec((nb, CHUNK, wd), lambda b, d, c: (b, cix(d, c), 0))
    tokl = lambda wd: pl.BlockSpec((nb, CHUNK, wd), lambda b, d, c: (b, late(d, c), 0))
    chan = lambda wd: pl.BlockSpec((nb, wd, CHUNK), lambda b, d, c: (b, 0, cix(d, c)))
    row1 = lambda wd: pl.BlockSpec((1, wd), lambda b, d, c: (0, 0))
    kern = functools.partial(_mixer_scan_kernel, nc_ctx=nc_ctx, nc=nc)
    return pl.pallas_call(
        kern,
        out_shape=[jax.ShapeDtypeStruct((bsz, s, SSD_WIDTH), F32),
                   jax.ShapeDtypeStruct((bsz, s, RET_WIDTH), F32)],
        grid=(bsz // nb, 2, nc),
        in_specs=[tokc(SSD_WIDTH), chan(SSD_WIDTH), tokc(bc.shape[-1]),
                  pl.BlockSpec((nb, 1, CHUNK, LANES), lambda b, d, c: (b, d, cix(d, c), 0)),
                  pl.BlockSpec((nb, 1, ROW_PACK, CHUNK), lambda b, d, c: (b, cix(d, c), d, 0)),
                  tokl(SSD_WIDTH), row1(SSD_WIDTH), row1(SSD_WIDTH),
                  tokc(RET_QK_WIDTH), tokc(RET_QK_WIDTH), chan(RET_QK_WIDTH), tokc(RET_WIDTH), tokl(RET_WIDTH),
                  pl.BlockSpec((1, RET_HEADS, CHUNK, CHUNK), lambda b, d, c: (d, 0, 0, 0)),
                  pl.BlockSpec((1, CHUNK, RET_QK_WIDTH), lambda b, d, c: (d, 0, 0)),
                  pl.BlockSpec((1, RET_QK_WIDTH, CHUNK), lambda b, d, c: (d, 0, 0)),
                  pl.BlockSpec((1, RET_HEADS, 1, LANES), lambda b, d, c: (d, 0, 0, 0))],
        out_specs=[tokl(SSD_WIDTH), tokl(RET_WIDTH)],
        scratch_shapes=[pltpu.VMEM((nb, nc, CHUNK, SSD_WIDTH), BF16),
                        pltpu.VMEM((nb, nc, CHUNK, RET_WIDTH), BF16),
                        pltpu.VMEM((nb, SSD_HEADS // 2, LANES, SSD_STATE), F32),
                        pltpu.VMEM((nb, RET_HEADS, RET_QK_DIM, RET_V_DIM), F32)],
        compiler_params=_params("parallel", "arbitrary", "arbitrary"),
        name="mixer_scan",
    )(xs_bf, xst_bf, bc, colp, rowp, z, dwide, normw, rq, rk_bf, rkt, rv_bf, rg, dec, ecum, wend, etot)


def _rms(v, g):
    return v * lax.rsqrt(jnp.mean(v * v, axis=-1, keepdims=True) + RMS_EPS) * g


def _mla_proj_kernel(cq_ref, ckv_ref, small_ref, cos_ref, sa_ref, sb_ref, qn_ref, kvn_ref,
                     wq_ref, wk_ref, wv_ref, q_ref, k_ref, v_ref):
    cqn = _rms(cq_ref[0], qn_ref[...]).astype(BF16)
    ckvn = _rms(ckv_ref[0], kvn_ref[...]).astype(BF16)
    q = _dot(cqn, wq_ref[...])
    kn = _dot(ckvn, wk_ref[...])
    vlane = lax.broadcasted_iota(jnp.int32, (1, MLA_PAIRS * MLA_V_SLOT), 1)
    ones = jnp.where(vlane % MLA_V_SLOT >= 2 * MLA_V_DIM, 1.0, 0.0)
    v_ref[0] = (_dot(ckvn, wv_ref[...]) + ones).astype(BF16)
    cos_t, sin_a, sin_b = cos_ref[...], sa_ref[...], sb_ref[...]
    lane = lax.broadcasted_iota(jnp.int32, small_ref.shape[1:], 1)
    in_rope = jnp.logical_and(lane >= KROPE_LANE, lane < KROPE_LANE + MLA_ROPE_DIM)
    kr = _rope3(jnp.where(in_rope, small_ref[0], 0.0), cos_t, sin_a, sin_b, MLA_ROPE_DIM // 2)
    for hd in range(MLA_HEADS):
        sl = slice(hd * MLA_HEAD_SLOT, (hd + 1) * MLA_HEAD_SLOT)
        q_ref[0, :, sl] = (_rope3(q[:, sl], cos_t, sin_a, sin_b, MLA_ROPE_DIM // 2) * MLA_Q_SCALE).astype(BF16)
        k_ref[0, :, sl] = (kn[:, sl] + kr).astype(BF16)


def _mla_proj(cq, ckv, small, cos_t, sin_a, sin_b, qn, kvn, wq, wk, wv):
    bsz, s, _ = cq.shape
    nt = s // TOKEN_TILE
    hw = MLA_HEADS * MLA_HEAD_SLOT
    vw = MLA_PAIRS * MLA_V_SLOT
    tok = lambda wd: pl.BlockSpec((1, TOKEN_TILE, wd), lambda b, t: (b, t, 0))
    tab = pl.BlockSpec((TOKEN_TILE, LANES), lambda b, t: (t, 0))
    full = lambda a: pl.BlockSpec(a.shape, lambda b, t: (0,) * a.ndim)
    return pl.pallas_call(
        _mla_proj_kernel,
        out_shape=[jax.ShapeDtypeStruct((bsz, s, hw), BF16),
                   jax.ShapeDtypeStruct((bsz, s, hw), BF16),
                   jax.ShapeDtypeStruct((bsz, s, vw), BF16)],
        grid=(bsz, nt),
        in_specs=[tok(MLA_Q_LORA), tok(MLA_KV_LORA), tok(LANES), tab, tab, tab,
                  full(qn), full(kvn), full(wq), full(wk), full(wv)],
        out_specs=[tok(hw), tok(hw), tok(vw)],
        compiler_params=_params("parallel", "arbitrary"),
        name="mla_proj",
    )(cq, ckv, small, cos_t, sin_a, sin_b, qn, kvn, wq, wk, wv)


ATTN_KEY_CHUNK = 256


def _attn_step(q_ref, k_ref, v_ref, g_ref, o_ref, slot_a, slot_b, n_keys):
    heads = range(2)
    hsl = [slice(hd * MLA_HEAD_SLOT, (hd + 1) * MLA_HEAD_SLOT) for hd in heads]
    if slot_a is not None:
        q = [q_ref[0, :, hsl[hd]] for hd in heads]
        mcol = [None, None]
    if slot_b is not None:
        mrow = [slot_b[1][hd] for hd in heads]
        pv = [None, None]
    for c in range(n_keys // ATTN_KEY_CHUNK):
        ks = slice(c * ATTN_KEY_CHUNK, (c + 1) * ATTN_KEY_CHUNK)
        for hd in heads:
            if slot_a is not None:
                s = _dot_nt(q[hd], k_ref[0, ks, hsl[hd]])
                slot_a[0][hd, :, ks] = s
                cm = s[:, :LANES]
                for w in range(1, ATTN_KEY_CHUNK // LANES):
                    cm = jnp.maximum(cm, s[:, w * LANES:(w + 1) * LANES])
                mcol[hd] = cm if c == 0 else jnp.maximum(mcol[hd], cm)
            if slot_b is not None:
                p = jnp.exp2(slot_b[0][hd, :, ks] - mrow[hd]).astype(BF16)
                part = _dot(p, v_ref[0, ks, :])
                pv[hd] = part if c == 0 else pv[hd] + part
    if slot_a is not None:
        for hd in heads:
            slot_a[1][hd] = jnp.max(mcol[hd], axis=-1, keepdims=True)
    if slot_b is not None:
        outs = [pv[hd][:, :LANES] / pv[hd][:, LANES:] for hd in heads]
        lane = lax.broadcasted_iota(jnp.int32, outs[0].shape, 1)
        o = jnp.where(lane < MLA_V_DIM, outs[0], outs[1])
        o_ref[0] = o * g_ref[0]


def _mla_attn_ctx_kernel(q_ref, k_ref, v_ref, g_ref, o_ref, s_ref, m_ref):
    n_keys = k_ref.shape[1]
    _attn_step(q_ref, k_ref, v_ref, g_ref, o_ref, (s_ref, m_ref), None, n_keys)
    _attn_step(q_ref, k_ref, v_ref, g_ref, o_ref, None, (s_ref, m_ref), n_keys)


def _mla_attn_ctx(q, k, v, gate, ctx_tiles):
    bsz, s, _ = q.shape
    tq = TOKEN_TILE
    n_ctx = ctx_tiles * tq
    pair_w = 2 * MLA_HEAD_SLOT
    return pl.pallas_call(
        _mla_attn_ctx_kernel,
        out_shape=jax.ShapeDtypeStruct((bsz, s, MLA_WIDTH), F32),
        grid=(bsz, MLA_PAIRS, ctx_tiles),
        in_specs=[pl.BlockSpec((1, tq, pair_w), lambda b, j, t: (b, t, j)),
                  pl.BlockSpec((1, n_ctx, pair_w), lambda b, j, t: (b, 0, j)),
                  pl.BlockSpec((1, n_ctx, MLA_V_SLOT), lambda b, j, t: (b, 0, j)),
                  pl.BlockSpec((1, tq, LANES), lambda b, j, t: (b, t, j))],
        out_specs=pl.BlockSpec((1, tq, LANES), lambda b, j, t: (b, t, j)),
        scratch_shapes=[pltpu.VMEM((2, tq, n_ctx), F32), pltpu.VMEM((2, tq, 1), F32)],
        compiler_params=_params("parallel", "arbitrary", "arbitrary"),
        name="mla_attn_ctx",
    )(q, k, v, gate)


def _mla_attn_lat_kernel(*refs, n_tiles, aliased):
    if aliased:
        refs = refs[1:]
    q_ref, k_ref, v_ref, g_ref, o_ref, s0, m0, s1, m1 = refs
    n_keys = k_ref.shape[1]
    slots = ((s0, m0), (s1, m1))
    n = pl.program_id(0)

    @pl.when(n == 0)
    def _():
        _attn_step(q_ref, k_ref, v_ref, g_ref, o_ref, slots[0], None, n_keys)

    for par in range(2):
        @pl.when(jnp.logical_and(jnp.logical_and(n > 0, n < n_tiles), n % 2 == par))
        def _():
            _attn_step(q_ref, k_ref, v_ref, g_ref, o_ref, slots[par], slots[1 - par], n_keys)

    @pl.when(n == n_tiles)
    def _():
        _attn_step(q_ref, k_ref, v_ref, g_ref, o_ref, None, slots[(n_tiles - 1) % 2], n_keys)


def _mla_attn_lat(q, k, v, gate, ctx_tiles, ctx_out=None):
    bsz, s, _ = q.shape
    tq = TOKEN_TILE
    nq = s // tq - ctx_tiles
    n_tiles = bsz * MLA_PAIRS * nq
    pair_w = 2 * MLA_HEAD_SLOT
    out_lo = 0 if ctx_out is not None else ctx_tiles

    def decode(i):
        return i // (MLA_PAIRS * nq), (i // nq) % MLA_PAIRS, i % nq + ctx_tiles

    def ahead(n):
        return decode(jnp.minimum(n, n_tiles - 1))

    def behind(n):
        return decode(jnp.maximum(n - 1, 0))

    def q_map(n):
        b, j, t = ahead(n)
        return b, t, j

    def k_map(n):
        b, j, _ = ahead(n)
        return b, 0, j

    def v_map(n):
        b, j, _ = behind(n)
        return b, 0, j

    def g_map(n):
        b, j, t = behind(n)
        return b, t, j

    def o_map(n):
        b, j, t = behind(n)
        return b, t - out_lo, j

    in_specs = [pl.BlockSpec((1, tq, pair_w), q_map),
                pl.BlockSpec((1, s, pair_w), k_map),
                pl.BlockSpec((1, s, MLA_V_SLOT), v_map),
                pl.BlockSpec((1, tq, LANES), g_map)]
    args = [q, k, v, gate]
    aliases = {}
    if ctx_out is not None:
        in_specs = [pl.BlockSpec(memory_space=pl.ANY)] + in_specs
        args = [ctx_out] + args
        aliases = {0: 0}
    kern = functools.partial(_mla_attn_lat_kernel, n_tiles=n_tiles, aliased=ctx_out is not None)
    return pl.pallas_call(
        kern,
        out_shape=jax.ShapeDtypeStruct((bsz, s - out_lo * tq, MLA_WIDTH), F32),
        grid=(n_tiles + 1,),
        in_specs=in_specs,
        out_specs=pl.BlockSpec((1, tq, LANES), o_map),
        scratch_shapes=[pltpu.VMEM((2, tq, s), F32), pltpu.VMEM((2, tq, 1), F32),
                        pltpu.VMEM((2, tq, s), F32), pltpu.VMEM((2, tq, 1), F32)],
        input_output_aliases=aliases,
        compiler_params=_params("arbitrary"),
        name="mla_attn",
    )(*args)


def _outproj_kernel(*refs, n_stream, ctx_tiles, t_lo, alpha):
    ssd_ref, mla_ref, ret_ref = refs[:3]
    x_refs = refs[3:3 + n_stream]
    mod_ref, w_ref, g_ref, b_ref, o_ref = refs[3 + n_stream:]
    acc = _dot(ssd_ref[0].astype(BF16), w_ref[:SSD_WIDTH, :])
    acc = acc + _dot(mla_ref[0].astype(BF16), w_ref[SSD_WIDTH:SSD_WIDTH + MLA_WIDTH, :])
    acc = acc + _dot(ret_ref[0].astype(BF16), w_ref[SSD_WIDTH + MLA_WIDTH:, :])
    gate = mod_ref[0, 0][:, 2 * D_MODEL:]
    x = _stream_tile(x_refs, pl.program_id(1) + t_lo, ctx_tiles)
    r = alpha * x + gate * acc
    mu = jnp.mean(r, axis=-1, keepdims=True)
    var = jnp.mean(jnp.square(r - mu), axis=-1, keepdims=True)
    o_ref[0] = (r - mu) * lax.rsqrt(var + LN_EPS) * g_ref[...] + b_ref[...]


def _outproj(ssd_o, mla_o, mla_lo, ret_o, stream, mod_all, w, ln_g, ln_b, t_lo, ctx_tiles, alpha):
    bsz, s, _ = ssd_o.shape
    d = stream[0].shape[-1]
    nt = s // TOKEN_TILE
    tok = lambda wd: pl.BlockSpec((1, TOKEN_TILE, wd), lambda b, t: (b, t + t_lo, 0))
    kern = functools.partial(_outproj_kernel, n_stream=len(stream), ctx_tiles=ctx_tiles, t_lo=t_lo, alpha=alpha)
    return pl.pallas_call(
        kern,
        out_shape=jax.ShapeDtypeStruct((bsz, s - t_lo * TOKEN_TILE, d), F32),
        grid=(bsz, nt - t_lo),
        in_specs=[tok(SSD_WIDTH),
                  pl.BlockSpec((1, TOKEN_TILE, MLA_WIDTH), lambda b, t: (b, t + t_lo - mla_lo, 0)),
                  tok(RET_WIDTH)] + _stream_specs(stream, t_lo, ctx_tiles) + [
                  pl.BlockSpec((1, 1, 1, 3 * d),
                               lambda b, t: (b, jnp.minimum(t + t_lo, ctx_tiles) // ctx_tiles, 0, 0)),
                  pl.BlockSpec(w.shape, lambda b, t: (0, 0)),
                  pl.BlockSpec((1, d), lambda b, t: (0, 0)),
                  pl.BlockSpec((1, d), lambda b, t: (0, 0))],
        out_specs=pl.BlockSpec((1, TOKEN_TILE, d), lambda b, t: (b, t, 0)),
        compiler_params=_params("parallel", "arbitrary"),
        name="out_proj",
    )(ssd_o, mla_o, ret_o, *stream, mod_all, w, ln_g.reshape(1, d), ln_b.reshape(1, d))


def _rope_tables(n_ctx, n_lat, rot_dim, width, lane0, period):
    t = np.arange(n_lat)
    row = (t // GRID_W).astype(np.float32)
    col = (t % GRID_W).astype(np.float32)
    n_freq = rot_dim // 4
    inv = (ROPE_THETA ** (-np.arange(n_freq, dtype=np.float32) / n_freq)).astype(np.float32)
    ang = np.concatenate([row[:, None] * inv, col[:, None] * inv], -1).astype(np.float32)
    cos, sin = np.cos(ang), np.sin(ang)
    half = rot_dim // 2
    lane = np.arange(width) % period - lane0
    in_rot = (lane >= 0) & (lane < rot_dim)
    idx = np.where(in_rot, lane % half, 0)
    first = in_rot & (lane < half)
    second = in_rot & (lane >= half)
    cos_t = np.where(in_rot[None, :], cos[:, idx], 1.0)
    sin_a = np.where(first[None, :], -sin[:, idx], 0.0)
    sin_b = np.where(second[None, :], sin[:, idx], 0.0)
    ctx_pad = lambda a, v: np.concatenate([np.full((n_ctx, width), v), a], axis=0).astype(np.float32)
    return ctx_pad(cos_t, 1.0), ctx_pad(sin_a, 0.0), ctx_pad(sin_b, 0.0)


def _mla_weights(w_uq, w_ukv):
    qd = MLA_NOPE_DIM + MLA_ROPE_DIM
    wq = w_uq.reshape(MLA_Q_LORA, MLA_HEADS, qd)
    wq = jnp.pad(wq, ((0, 0), (0, 0), (0, MLA_HEAD_SLOT - qd))).reshape(MLA_Q_LORA, MLA_HEADS * MLA_HEAD_SLOT)
    wkv = w_ukv.reshape(MLA_KV_LORA, MLA_HEADS, MLA_NOPE_DIM + MLA_V_DIM)
    wk = jnp.pad(wkv[..., :MLA_NOPE_DIM], ((0, 0), (0, 0), (0, MLA_HEAD_SLOT - MLA_NOPE_DIM)))
    wk = wk.reshape(MLA_KV_LORA, MLA_HEADS * MLA_HEAD_SLOT)
    wv = wkv[..., MLA_NOPE_DIM:].reshape(MLA_KV_LORA, MLA_PAIRS, 2 * MLA_V_DIM)
    wv = jnp.pad(wv, ((0, 0), (0, 0), (0, MLA_V_SLOT - 2 * MLA_V_DIM))).reshape(MLA_KV_LORA, MLA_PAIRS * MLA_V_SLOT)
    return wq.astype(BF16), wk.astype(BF16), wv.astype(BF16)


def _lane_pad(v):
    return jnp.pad(v, (0, LANES - v.shape[0])).reshape(1, LANES)


def kernel(x, c, ctx, c_ctx, w_ada, b_ada, w_in, ssd_conv_w, ssd_conv_b, ssd_a_log_f, ssd_a_log_b,
           ssd_dt_bias_f, ssd_dt_bias_b, ssd_d, ssd_norm_w, mla_q_norm, mla_w_uq, mla_kv_norm, mla_w_ukv,
           ret_log_rate_f, ret_log_rate_b, w_out, ln_g, ln_b):
    bsz, n_lat, d = x.shape
    n_ctx = ctx.shape[1]
    depth = w_in.shape[0]
    assert d == D_MODEL and n_ctx == TOKEN_TILE and n_lat % TOKEN_TILE == 0
    alpha = (2 * depth) ** 0.25
    s = n_ctx + n_lat
    nc_ctx = n_ctx // CHUNK
    ctx_tiles = n_ctx // TOKEN_TILE
    nt = s // TOKEN_TILE

    cos_m, sa_m, sb_m = _rope_tables(n_ctx, n_lat, MLA_ROPE_DIM, LANES, MLA_ROPE_LANE, LANES)
    cos_r, sa_r, sb_r = _rope_tables(n_ctx, n_lat, RET_QK_DIM, RET_QK_WIDTH, 0, RET_QK_DIM)

    rows = 16
    cc = jnp.zeros((rows, d), F32).at[:bsz].set(c).at[bsz].set(c_ctx)
    stream = (ctx, x)

    for i in range(depth):
        last = i == depth - 1
        mod = _ada(cc, w_ada, b_ada, i)
        mod_all = jnp.stack([jnp.broadcast_to(mod[bsz], (bsz, 3 * d)), mod[:bsz]], axis=1)
        mod_all = mod_all.reshape(bsz, 2, 1, 3 * d)
        both = lambda f, b: jnp.stack([f, b], axis=0)
        head_rows = lambda v: jnp.broadcast_to(v[:, None], (SSD_HEADS, TOKEN_TILE))
        alogt = both(head_rows(ssd_a_log_f[i]), head_rows(ssd_a_log_b[i]))
        dtbt = both(head_rows(ssd_dt_bias_f[i]), head_rows(ssd_dt_bias_b[i]))
        (zs, xs_bf, xst_bf, bc, colp, rowp, small, cq, ckv, mgs, rq, rk_bf, rkt, rv_bf, rgs) = _inproj(
            stream, mod_all, _relayout_w_in(w_in, i), ssd_conv_w[i], ssd_conv_b[i].reshape(1, -1), alogt, dtbt,
            cos_r, sa_r, sb_r, ctx_tiles)

        dwide = jnp.repeat(ssd_d[i], SSD_HEAD_DIM).reshape(1, SSD_WIDTH)
        rates = both(_lane_pad(ret_log_rate_f[i]), _lane_pad(ret_log_rate_b[i]))
        prep = (xs_bf, xst_bf, bc, colp, rowp, rq, rk_bf, rkt, rv_bf)
        ssd_o, ret_o = _mixer_scan(prep, zs, rgs, dwide, ssd_norm_w[i].reshape(1, -1), _ret_consts(rates), nc_ctx)

        wq, wk, wv = _mla_weights(mla_w_uq[i], mla_w_ukv[i])
        q, k, v = _mla_proj(cq, ckv, small, cos_m, sa_m, sb_m, mla_q_norm[i].reshape(1, -1),
                            mla_kv_norm[i].reshape(1, -1), wq, wk, wv)
        mla_lo = ctx_tiles if last else 0
        mla_o = _mla_attn_lat(q, k, v, mgs, ctx_tiles, None if last else _mla_attn_ctx(q, k, v, mgs, ctx_tiles))

        out = _outproj(ssd_o, mla_o, mla_lo, ret_o, stream, mod_all, w_out[i].astype(BF16), ln_g[i], ln_b[i],
                       ctx_tiles if last else 0, ctx_tiles, alpha)
        stream = (out,)
    return stream[0]
```

```python
import functools
import math

import jax
import jax.numpy as jnp
import numpy as np
from jax import lax
from jax.experimental import pallas as pl
from jax.experimental.pallas import tpu as pltpu

F32 = jnp.float32
BF16 = jnp.bfloat16

LANES = 128
SUBLANES = 8
VMEM_LIMIT_BYTES = 56 * 1024 * 1024

GRID_W = 64
D_MODEL = 1024
SSD_WIDTH = 1024
SSD_HEAD_DIM = 64
SSD_HEADS = 16
SSD_GROUPS = 2
SSD_STATE = 128
SSD_CONV = 5
SSD_CONV_DIM = SSD_WIDTH + 2 * SSD_GROUPS * SSD_STATE
MLA_WIDTH = 512
MLA_V_DIM = 64
MLA_HEADS = 8
MLA_NOPE_DIM = 64
MLA_ROPE_DIM = 32
MLA_Q_LORA = 384
MLA_KV_LORA = 256
MLA_SCALE = (MLA_NOPE_DIM + MLA_ROPE_DIM) ** -0.5
RET_WIDTH = 512
RET_HEADS = 4
RET_V_DIM = 128
RET_QK_DIM = 64
RET_QK_WIDTH = RET_HEADS * RET_QK_DIM
CHUNK = 128
ROPE_THETA = 10000.0
LN_EPS = 1e-5
RMS_EPS = 1e-6
IN_WIDTHS = (SSD_WIDTH, SSD_CONV_DIM, SSD_HEADS, MLA_Q_LORA, MLA_KV_LORA, MLA_ROPE_DIM, MLA_WIDTH,
             RET_QK_WIDTH, RET_QK_WIDTH, RET_WIDTH, RET_WIDTH)

TOKEN_TILE = 256
MLA_HEAD_SLOT = 128
MLA_ROPE_LANE = MLA_NOPE_DIM
MLA_PAIRS = MLA_HEADS // 2
MLA_V_SLOT = 256
MLA_Q_SCALE = MLA_SCALE * math.log2(math.e)
NEG_BIG = -1e30


def _params(*sem):
    return pltpu.CompilerParams(dimension_semantics=sem, vmem_limit_bytes=VMEM_LIMIT_BYTES)


def _silu(v):
    return v * jax.nn.sigmoid(v)


def _softplus(v):
    return jnp.maximum(v, 0.0) + jnp.log1p(jnp.exp(-jnp.abs(v)))


def _dot(a, b):
    return jnp.dot(a, b, preferred_element_type=F32)


def _dot_exact(a, b):
    return jnp.dot(a, b, preferred_element_type=F32, precision=lax.Precision.HIGHEST)


def _dot_nt(a, b):
    return lax.dot_general(a, b, (((1,), (1,)), ((), ())), preferred_element_type=F32)


def _ada_kernel(c_ref, w_ref, b_ref, o_ref):
    o_ref[...] = _dot_exact(_silu(c_ref[...]), w_ref[0]) + b_ref[0]


def _ada(cc, w_all, b_all, layer):
    rows, d = cc.shape
    depth, _, n = w_all.shape
    tn = 1024
    return pl.pallas_call(
        _ada_kernel,
        out_shape=jax.ShapeDtypeStruct((rows, n), F32),
        grid=(n // tn,),
        in_specs=[pl.BlockSpec((rows, d), lambda j: (0, 0)),
                  pl.BlockSpec((1, d, tn), lambda j: (layer, 0, j)),
                  pl.BlockSpec((1, 1, tn), lambda j: (layer, 0, j))],
        out_specs=pl.BlockSpec((rows, tn), lambda j: (0, j)),
        compiler_params=_params("arbitrary"),
        name="ada_mod",
    )(cc, w_all, b_all.reshape(depth, 1, n))


def _chunk_index(d, c, nc_ctx, nc):
    bwd = jnp.where(c < nc_ctx, nc_ctx - 1 - c, nc + nc_ctx - 1 - c)
    return jnp.where(d == 0, c, bwd)


def _tri_masks(d):
    ii = lax.broadcasted_iota(jnp.int32, (CHUNK, CHUNK), 0)
    jj = lax.broadcasted_iota(jnp.int32, (CHUNK, CHUNK), 1)
    sgn = 1 - 2 * d
    mask = (ii - jj) * sgn >= 0
    mask_t = (jj - ii) * sgn >= 0
    return mask, mask_t


def _rope3(v, cos_t, sin_a, sin_b, shift):
    width = v.shape[-1]
    return v * cos_t + pltpu.roll(v, width - shift, 1) * sin_a + pltpu.roll(v, shift, 1) * sin_b


PROJ_WIDTHS = (SSD_WIDTH, SSD_CONV_DIM, LANES, MLA_Q_LORA, MLA_KV_LORA, MLA_WIDTH,
               RET_QK_WIDTH, RET_QK_WIDTH, RET_WIDTH, RET_WIDTH)
PROJ_OFFSETS = tuple(int(o) for o in np.cumsum((0,) + PROJ_WIDTHS[:-1]))
DT_LANE = 0
KROPE_LANE = MLA_ROPE_LANE
ROW_CUM, ROW_DT, ROW_WEND, ROW_ETOT = 0, 16, 32, 48
ROW_PACK = 64
TILE_CHUNKS = TOKEN_TILE // CHUNK


IN_OFFSETS = tuple(int(o) for o in np.cumsum((0,) + IN_WIDTHS[:-1]))
RELAYOUT_MOVES = ((0, 0, 0), (1, 1, 0), (2, 2, DT_LANE), (3, 3, 0), (4, 4, 0), (5, 2, KROPE_LANE), (6, 5, 0),
                  (7, 6, 0), (8, 7, 0), (9, 8, 0), (10, 9, 0))
RELAYOUT_ROWS = 128


def _relayout_w_in_kernel(w_ref, o_ref):
    small = PROJ_OFFSETS[2]
    o_ref[:, small:small + LANES] = jnp.zeros((o_ref.shape[0], LANES), BF16)
    for src, dst, lane in RELAYOUT_MOVES:
        lo = PROJ_OFFSETS[dst] + lane
        o_ref[:, lo:lo + IN_WIDTHS[src]] = w_ref[0, :, IN_OFFSETS[src]:IN_OFFSETS[src] + IN_WIDTHS[src]].astype(BF16)


def _relayout_w_in(w_in_all, layer):
    _, d, n_in = w_in_all.shape
    n_out = sum(PROJ_WIDTHS)
    return pl.pallas_call(
        _relayout_w_in_kernel,
        out_shape=jax.ShapeDtypeStruct((d, n_out), BF16),
        grid=(d // RELAYOUT_ROWS,),
        in_specs=[pl.BlockSpec((1, RELAYOUT_ROWS, n_in), lambda r: (layer, r, 0))],
        out_specs=pl.BlockSpec((RELAYOUT_ROWS, n_out), lambda r: (r, 0)),
        compiler_params=_params("arbitrary"),
        name="w_in_relayout",
    )(w_in_all)


def _stream_specs(stream, t_lo, ctx_tiles):
    d = stream[0].shape[-1]
    if len(stream) == 1:
        return [pl.BlockSpec((1, TOKEN_TILE, d), lambda b, t: (b, t + t_lo, 0))]
    return [pl.BlockSpec((1, TOKEN_TILE, d), lambda b, t: (b, jnp.minimum(t + t_lo, ctx_tiles - 1), 0)),
            pl.BlockSpec((1, TOKEN_TILE, d), lambda b, t: (b, jnp.maximum(t + t_lo - ctx_tiles, 0), 0))]


def _halo_specs(stream, ctx_tiles):
    arr = stream[-1]
    d = arr.shape[-1]
    shift = 0 if len(stream) == 1 else ctx_tiles
    per_tile = TOKEN_TILE // SUBLANES
    nblk = arr.shape[1] // SUBLANES
    prev = lambda b, t: (b, jnp.clip((t - shift) * per_tile - 1, 0, nblk - 1), 0)
    nxt = lambda b, t: (b, jnp.clip((t - shift + 1) * per_tile, 0, nblk - 1), 0)
    return [pl.BlockSpec((1, SUBLANES, d), prev), pl.BlockSpec((1, SUBLANES, d), nxt)]


def _stream_tile(refs, t, ctx_tiles):
    if len(refs) == 1:
        return refs[0][0]
    return jnp.where(t < ctx_tiles, refs[0][0], refs[1][0])


def _inproj_kernel(*refs, n_stream, ctx_tiles, nt):
    x_refs = refs[:n_stream]
    (prev_ref, next_ref, mod_ref, w_ref, convw_ref, convb_ref, alogt_ref, dtbt_ref,
     cos_ref, sa_ref, sb_ref) = refs[n_stream:n_stream + 11]
    (zs_ref, xs_ref, xst_ref, bc_ref, colp_ref, rowp_ref, small_ref, cq_ref, ckv_ref, mgs_ref,
     q_ref, kbf_ref, kt_ref, v_ref, rgs_ref) = refs[n_stream + 11:]
    t = pl.program_id(1)
    mod = mod_ref[0, 0]
    sh = mod[:, :D_MODEL]
    sc = mod[:, D_MODEL:2 * D_MODEL]
    modulate = lambda v: (v * (1.0 + sc) + sh).astype(BF16)
    h = modulate(_stream_tile(x_refs, t, ctx_tiles))
    h_ext = jnp.concatenate([modulate(prev_ref[0]), h, modulate(next_ref[0])], axis=0)

    def proj(g, lo=0, width=None, lhs=None):
        off = PROJ_OFFSETS[g] + lo
        width = PROJ_WIDTHS[g] - lo if width is None else width
        return _dot(h if lhs is None else lhs, w_ref[:, off:off + width])

    first = jnp.logical_or(t == 0, t == ctx_tiles)
    last = jnp.logical_or(t == ctx_tiles - 1, t == nt - 1)
    row = lax.broadcasted_iota(jnp.int32, (TOKEN_TILE + 2 * SUBLANES, 1), 0)
    keep = jnp.where(jnp.logical_or(jnp.logical_and(row < SUBLANES, first),
                                    jnp.logical_and(row >= SUBLANES + TOKEN_TILE, last)), 0.0, 1.0)
    unit_w = 2 * LANES
    ext_parts = {}

    def ext_unit(e):
        ext_parts[e] = proj(1, e * unit_w, unit_w, lhs=h_ext) * keep

    def conv_strip(j, anchor):
        pad = SSD_CONV // 2
        sl = slice(j * LANES, (j + 1) * LANES)
        esl = slice((j % 2) * LANES, (j % 2 + 1) * LANES)
        ext = ext_parts[j // 2]
        acc = convb_ref[:, sl] + anchor
        for k in range(SSD_CONV):
            lo = SUBLANES - pad + k
            acc = acc + convw_ref[k:k + 1, sl] * ext[lo:lo + TOKEN_TILE, esl]
        u = _silu(acc)
        if j < SSD_WIDTH // LANES:
            xs_ref[0, :, sl] = u.astype(BF16)
            xst_ref[0, sl, :] = u.T.astype(BF16)
        else:
            bc_ref[0, :, j * LANES - SSD_WIDTH:(j + 1) * LANES - SSD_WIDTH] = u

    def gate_unit(g, ref, lo, width):
        def run():
            val = proj(g, lo, width)
            ref[0, :, lo:lo + width] = _silu(val)
            return val
        return run

    def plain_unit(g, ref, cast=None):
        def run():
            val = proj(g)
            ref[0] = val if cast is None else val.astype(cast)
            return val
        return run

    def ret_q_unit():
        val = proj(6)
        q_ref[0] = _rope3(val, cos_ref[...], sa_ref[...], sb_ref[...], RET_QK_DIM // 2)
        return val

    def ret_k_unit():
        val = proj(7)
        kk = _rope3(val, cos_ref[...], sa_ref[...], sb_ref[...], RET_QK_DIM // 2) * (RET_QK_DIM ** -0.5)
        kbf_ref[0] = kk.astype(BF16)
        kt_ref[0] = kk.T
        return val

    units = [gate_unit(0, zs_ref, lo, unit_w) for lo in range(0, SSD_WIDTH, unit_w)]
    units += [plain_unit(3, cq_ref), plain_unit(4, ckv_ref)]
    units += [gate_unit(5, mgs_ref, lo, unit_w) for lo in range(0, MLA_WIDTH, unit_w)]
    units += [ret_q_unit, ret_k_unit, plain_unit(8, v_ref, BF16)]
    units += [gate_unit(9, rgs_ref, lo, unit_w) for lo in range(0, RET_WIDTH, unit_w)]
    small = proj(2)
    small_ref[0] = small
    n_strips = SSD_CONV_DIM // LANES
    ext_unit(0)
    anchor = jnp.zeros((1, LANES), F32)
    for j in range(n_strips):
        if j % 2 == 0 and j // 2 + 1 < n_strips // 2:
            ext_unit(j // 2 + 1)
        nxt = units.pop(0)()[0:1, :LANES] * 0.0 if units else anchor
        conv_strip(j, anchor)
        anchor = nxt
    for unit in units:
        unit()

    dt_raw_t = small.T[DT_LANE:DT_LANE + SSD_HEADS, :]
    pad_rows = jnp.zeros((CHUNK - SSD_HEADS, CHUNK), F32)
    for d in range(2):
        _, mask_t = _tri_masks(d)
        tri_t = jnp.where(mask_t, 1.0, 0.0)
        dt_all = _softplus(dt_raw_t + dtbt_ref[d])
        la_all = dt_all * -jnp.exp(alogt_ref[d])
        base = d * ROW_PACK
        for j in range(TILE_CHUNKS):
            cs = slice(j * CHUNK, (j + 1) * CHUNK)
            dt_row, la_row = dt_all[:, cs], la_all[:, cs]
            cum_row = _dot_exact(la_row, tri_t)
            colp_ref[0, d, cs, :] = jnp.concatenate([cum_row, pad_rows], axis=0).T
            tot_row = jnp.sum(la_row, axis=1, keepdims=True)
            rowp_ref[0, j, base + ROW_CUM:base + ROW_CUM + SSD_HEADS, :] = cum_row
            rowp_ref[0, j, base + ROW_DT:base + ROW_DT + SSD_HEADS, :] = dt_row
            rowp_ref[0, j, base + ROW_WEND:base + ROW_WEND + SSD_HEADS, :] = dt_row * jnp.exp(tot_row - cum_row)
            rowp_ref[0, j, base + ROW_ETOT:base + ROW_ETOT + SSD_HEADS, :] = jnp.broadcast_to(
                jnp.exp(tot_row), (SSD_HEADS, CHUNK))


def _inproj(stream, mod_all, w, conv_w, conv_b, alogt, dtbt, cos_t, sin_a, sin_b, ctx_tiles):
    bsz, d = stream[0].shape[0], stream[0].shape[-1]
    s = sum(a.shape[1] for a in stream)
    nt = s // TOKEN_TILE
    nc = s // CHUNK
    tok = lambda wd: pl.BlockSpec((1, TOKEN_TILE, wd), lambda b, t: (b, t, 0))
    chan = lambda wd: pl.BlockSpec((1, wd, TOKEN_TILE), lambda b, t: (b, 0, t))
    full = lambda a: pl.BlockSpec(a.shape, lambda b, t: (0,) * a.ndim)
    tab = pl.BlockSpec((TOKEN_TILE, RET_QK_WIDTH), lambda b, t: (t, 0))
    bcw = SSD_CONV_DIM - SSD_WIDTH
    outs = [((bsz, s, SSD_WIDTH), F32, tok(SSD_WIDTH)),
            ((bsz, s, SSD_WIDTH), BF16, tok(SSD_WIDTH)),
            ((bsz, SSD_WIDTH, s), BF16, chan(SSD_WIDTH)),
            ((bsz, s, bcw), F32, tok(bcw)),
            ((bsz, 2, s, LANES), F32, pl.BlockSpec((1, 2, TOKEN_TILE, LANES), lambda b, t: (b, 0, t, 0))),
            ((bsz, nc, 2 * ROW_PACK, CHUNK), F32,
             pl.BlockSpec((1, TILE_CHUNKS, 2 * ROW_PACK, CHUNK), lambda b, t: (b, t, 0, 0))),
            ((bsz, s, LANES), F32, tok(LANES)),
            ((bsz, s, MLA_Q_LORA), F32, tok(MLA_Q_LORA)),
            ((bsz, s, MLA_KV_LORA), F32, tok(MLA_KV_LORA)),
            ((bsz, s, MLA_WIDTH), F32, tok(MLA_WIDTH)),
            ((bsz, s, RET_QK_WIDTH), F32, tok(RET_QK_WIDTH)),
            ((bsz, s, RET_QK_WIDTH), BF16, tok(RET_QK_WIDTH)),
            ((bsz, RET_QK_WIDTH, s), F32, chan(RET_QK_WIDTH)),
            ((bsz, s, RET_WIDTH), BF16, tok(RET_WIDTH)),
            ((bsz, s, RET_WIDTH), F32, tok(RET_WIDTH))]
    kern = functools.partial(_inproj_kernel, n_stream=len(stream), ctx_tiles=ctx_tiles, nt=nt)
    return pl.pallas_call(
        kern,
        out_shape=[jax.ShapeDtypeStruct(shape, dt) for shape, dt, _ in outs],
        grid=(bsz, nt),
        in_specs=_stream_specs(stream, 0, ctx_tiles) + _halo_specs(stream, ctx_tiles) + [
            pl.BlockSpec((1, 1, 1, 3 * d), lambda b, t: (b, jnp.minimum(t, ctx_tiles) // ctx_tiles, 0, 0)),
            pl.BlockSpec(w.shape, lambda b, t: (0, 0), pipeline_mode=pl.Buffered(1)),
            full(conv_w), full(conv_b), full(alogt), full(dtbt), tab, tab, tab],
        out_specs=[spec for _, _, spec in outs],
        compiler_params=_params("parallel", "arbitrary"),
        name="in_proj",
    )(*stream, stream[-1], stream[-1], mod_all, w, conv_w, conv_b, alogt, dtbt, cos_t, sin_a, sin_b)


def _ret_consts_kernel(rate_ref, dec_ref, ecum_ref, wend_ref, etot_ref):
    d = pl.program_id(0)
    mask, _ = _tri_masks(d)
    ii = lax.broadcasted_iota(jnp.int32, (CHUNK, CHUNK), 0)
    jj = lax.broadcasted_iota(jnp.int32, (CHUNK, CHUNK), 1)
    dist = jnp.abs(ii - jj).astype(F32)
    pos_i = lax.broadcasted_iota(jnp.int32, (CHUNK, RET_QK_DIM), 0)
    steps_col = jnp.where(d == 0, pos_i + 1, CHUNK - pos_i).astype(F32)
    pos_j = lax.broadcasted_iota(jnp.int32, (RET_QK_DIM, CHUNK), 1)
    left_row = jnp.where(d == 0, CHUNK - 1 - pos_j, pos_j).astype(F32)
    for hd in range(RET_HEADS):
        la = -jnp.exp(rate_ref[0, :, hd:hd + 1])
        dec_ref[0, hd] = jnp.exp(jnp.where(mask, dist * la, NEG_BIG))
        ecum_ref[0, :, hd * RET_QK_DIM:(hd + 1) * RET_QK_DIM] = jnp.exp(steps_col * la)
        wend_ref[0, hd * RET_QK_DIM:(hd + 1) * RET_QK_DIM, :] = jnp.exp(left_row * la)
        etot_ref[0, hd] = jnp.broadcast_to(jnp.exp(CHUNK * la), (1, LANES))


def _ret_consts(rates):
    return pl.pallas_call(
        _ret_consts_kernel,
        out_shape=[jax.ShapeDtypeStruct((2, RET_HEADS, CHUNK, CHUNK), F32),
                   jax.ShapeDtypeStruct((2, CHUNK, RET_QK_WIDTH), F32),
                   jax.ShapeDtypeStruct((2, RET_QK_WIDTH, CHUNK), F32),
                   jax.ShapeDtypeStruct((2, RET_HEADS, 1, LANES), F32)],
        grid=(2,),
        in_specs=[pl.BlockSpec((1, 1, LANES), lambda d: (d, 0, 0))],
        out_specs=[pl.BlockSpec((1, RET_HEADS, CHUNK, CHUNK), lambda d: (d, 0, 0, 0)),
                   pl.BlockSpec((1, CHUNK, RET_QK_WIDTH), lambda d: (d, 0, 0)),
                   pl.BlockSpec((1, RET_QK_WIDTH, CHUNK), lambda d: (d, 0, 0)),
                   pl.BlockSpec((1, RET_HEADS, 1, LANES), lambda d: (d, 0, 0, 0))],
        compiler_params=_params("arbitrary"),
        name="ret_consts",
    )(rates)


SCAN_BATCH = 2


def _mixer_scan_kernel(*refs, direction):
    (xs_ref, xst_ref, bc_ref, colp_ref, rowp_ref, q_ref, kbf_ref, kt_ref, v_ref,
     dec_ref, ecum_ref, wend_ref, etot_ref) = refs[:13]
    if direction == 0:
        yf_ref, yr_ref, ht_ref, hr_ref = refs[13:]
    else:
        (yf_ref, yr_ref, z_ref, g_ref, dwide_ref, normw_ref,
         ssd_out_ref, ret_out_ref, ht_ref, hr_ref) = refs[13:]
    d = direction
    c = pl.program_id(1)
    chains = range(SCAN_BATCH)

    @pl.when(c == 0)
    def _():
        ht_ref[...] = jnp.zeros_like(ht_ref)
        hr_ref[...] = jnp.zeros_like(hr_ref)

    mask, _ = _tri_masks(d)
    lane = lax.broadcasted_iota(jnp.int32, (CHUNK, LANES), 1)
    sub = lax.broadcasted_iota(jnp.int32, (CHUNK, LANES), 0)
    lane_lo = lane < SSD_HEAD_DIM
    sub_lo = sub < SSD_HEAD_DIM
    heads_per_group = SSD_HEADS // SSD_GROUPS

    ret = []
    for bb in chains:
        q = q_ref[bb]
        ret.append(dict(q_bf=q.astype(BF16), k_bf=kbf_ref[bb], qe_bf=(q * ecum_ref[0]).astype(BF16),
                        ktw_bf=(kt_ref[bb] * wend_ref[0]).astype(BF16), v_bf=v_ref[bb], lhs={}, ys=[]))

    def ret_scores(bb, hd):
        r = ret[bb]
        qs = slice(hd * RET_QK_DIM, (hd + 1) * RET_QK_DIM)
        sc = _dot_nt(r["q_bf"][:, qs], r["k_bf"][:, qs]) * dec_ref[0, hd]
        r["lhs"][hd] = jnp.concatenate([sc.astype(BF16), r["qe_bf"][:, qs]], axis=1)

    def ret_update(bb, hd):
        r = ret[bb]
        qs = slice(hd * RET_QK_DIM, (hd + 1) * RET_QK_DIM)
        v_h = r["v_bf"][:, hd * RET_V_DIM:(hd + 1) * RET_V_DIM]
        h_h = hr_ref[bb, hd]
        rhs = jnp.concatenate([v_h, h_h.astype(BF16)], axis=0)
        r["ys"].append(_dot(r["lhs"][hd], rhs))
        hr_ref[bb, hd] = etot_ref[0, hd] * h_h + _dot(r["ktw_bf"][qs, :], v_h)

    ssd = []
    for bb in chains:
        rowp = rowp_ref[bb, 0]
        ssd.append(dict(xs_bf=xs_ref[bb], bc=bc_ref[bb], cum_col=colp_ref[bb, 0],
                        cum_row=rowp[ROW_CUM:ROW_CUM + SSD_HEADS], dt_row=rowp[ROW_DT:ROW_DT + SSD_HEADS],
                        wend_row=rowp[ROW_WEND:ROW_WEND + SSD_HEADS],
                        etot_row=rowp[ROW_ETOT:ROW_ETOT + SSD_HEADS], gmats={}, ys=[]))

    def ssd_pair(bb, k):
        s = ssd[bb]
        g = (2 * k) // heads_per_group
        if g not in s["gmats"]:
            b_g = s["bc"][:, g * SSD_STATE:(g + 1) * SSD_STATE].astype(BF16)
            c_g = s["bc"][:, (SSD_GROUPS + g) * SSD_STATE:(SSD_GROUPS + g + 1) * SSD_STATE]
            s["gmats"][g] = (_dot_nt(c_g.astype(BF16), b_g), c_g, b_g)
        scores, c_g, b_g = s["gmats"][g]
        ha, hb = 2 * k, 2 * k + 1
        xs_pair = s["xs_bf"][:, k * LANES:(k + 1) * LANES]
        ht_pair = ht_ref[bb, k]
        rhs = jnp.concatenate([xs_pair, ht_pair.T.astype(BF16)], axis=0)
        y_heads = []
        for hd in (ha, hb):
            cb = jnp.broadcast_to(s["cum_col"][:, hd:hd + 1], (CHUNK, CHUNK))
            seg = cb - s["cum_row"][hd:hd + 1, :]
            dec = jnp.exp(jnp.where(mask, seg, NEG_BIG))
            m = scores * dec * s["dt_row"][hd:hd + 1, :]
            lhs = jnp.concatenate([m, c_g * jnp.exp(cb)], axis=1).astype(BF16)
            y_heads.append(_dot(lhs, rhs))
        s["ys"].append(jnp.where(lane_lo, y_heads[0], y_heads[1]))
        w_pair = jnp.where(sub_lo, s["wend_row"][ha:ha + 1, :], s["wend_row"][hb:hb + 1, :])
        e_pair = jnp.where(sub_lo, s["etot_row"][ha:ha + 1, :], s["etot_row"][hb:hb + 1, :])
        xw = (xst_ref[bb, k * LANES:(k + 1) * LANES, :].astype(F32) * w_pair).astype(BF16)
        ht_ref[bb, k] = e_pair * ht_pair + _dot(xw, b_g)

    for k in range(SSD_HEADS // 2):
        for bb in chains:
            ssd_pair(bb, k)
        for bb in chains:
            if k % 2 == 0:
                ret_scores(bb, k // 2)
            else:
                ret_update(bb, k // 2)
    y = [jnp.concatenate(ssd[bb]["ys"], axis=1) for bb in chains]
    yr = [jnp.concatenate(ret[bb]["ys"], axis=1) for bb in chains]

    for bb in chains:
        if direction == 0:
            yf_ref[bb] = y[bb].astype(yf_ref.dtype)
            yr_ref[bb] = yr[bb].astype(yr_ref.dtype)
            continue
        tot = yf_ref[bb].astype(F32) + y[bb] + ssd[bb]["xs_bf"].astype(F32) * dwide_ref[...]
        gated = tot * z_ref[bb]
        ms = jnp.mean(gated * gated, axis=-1, keepdims=True)
        ssd_out_ref[bb] = gated * lax.rsqrt(ms + RMS_EPS) * normw_ref[...]
        rtot = yr_ref[bb].astype(F32) + yr[bb]
        outs = []
        for hd in range(RET_HEADS):
            t_h = rtot[:, hd * RET_V_DIM:(hd + 1) * RET_V_DIM]
            mu = jnp.mean(t_h, axis=-1, keepdims=True)
            var = jnp.mean(jnp.square(t_h - mu), axis=-1, keepdims=True)
            outs.append((t_h - mu) * lax.rsqrt(var + LN_EPS))
        ret_out_ref[bb] = jnp.concatenate(outs, axis=1) * g_ref[bb]


def _mixer_scan(prep, z, rg, dwide, normw, rconsts, nc_ctx):
    xs_bf, xst_bf, bc, colp, rowp, rq, rk_bf, rkt, rv_bf = prep
    dec, ecum, wend, etot = rconsts
    bsz, s, _ = z.shape
    nc = s // CHUNK
    nb = SCAN_BATCH
    scratch = [pltpu.VMEM((nb, SSD_HEADS // 2, LANES, SSD_STATE), F32),
               pltpu.VMEM((nb, RET_HEADS, RET_QK_DIM, RET_V_DIM), F32)]

    def call(d, extra_in, extra_specs, out_shape, out_specs):
        cix = lambda c: _chunk_index(d, c, nc_ctx, nc)
        tokc = lambda wd: pl.BlockSpec((nb, CHUNK, wd), lambda b, c: (b, cix(c), 0))
        chan = lambda wd: pl.BlockSpec((nb, wd, CHUNK), lambda b, c: (b, 0, cix(c)))
        per_dir = lambda a: pl.BlockSpec((1,) + a.shape[1:], lambda b, c: (d,) + (0,) * (a.ndim - 1))
        common = [tokc(SSD_WIDTH), chan(SSD_WIDTH), tokc(bc.shape[-1]),
                  pl.BlockSpec((nb, 1, CHUNK, LANES), lambda b, c: (b, d, cix(c), 0)),
                  pl.BlockSpec((nb, 1, ROW_PACK, CHUNK), lambda b, c: (b, cix(c), d, 0)),
                  tokc(RET_QK_WIDTH), tokc(RET_QK_WIDTH), chan(RET_QK_WIDTH), tokc(RET_WIDTH),
                  per_dir(dec), per_dir(ecum), per_dir(wend), per_dir(etot)]
        return pl.pallas_call(
            functools.partial(_mixer_scan_kernel, direction=d),
            out_shape=out_shape,
            grid=(bsz // nb, nc),
            in_specs=common + extra_specs(tokc),
            out_specs=out_specs(tokc),
            scratch_shapes=scratch,
            compiler_params=_params("parallel", "arbitrary"),
            name="mixer_scan_fwd" if d == 0 else "mixer_scan_bwd",
        )(xs_bf, xst_bf, bc, colp, rowp, rq, rk_bf, rkt, rv_bf, dec, ecum, wend, etot, *extra_in)

    row1 = lambda wd: pl.BlockSpec((1, wd), lambda b, c: (0, 0))
    yf, yr = call(0, [], lambda tokc: [],
                  [jax.ShapeDtypeStruct((bsz, s, SSD_WIDTH), BF16), jax.ShapeDtypeStruct((bsz, s, RET_WIDTH), BF16)],
                  lambda tokc: [tokc(SSD_WIDTH), tokc(RET_WIDTH)])
    return call(1, [yf, yr, z, rg, dwide, normw],
                lambda tokc: [tokc(SSD_WIDTH), tokc(RET_WIDTH), tokc(SSD_WIDTH), tokc(RET_WIDTH),
                              row1(SSD_WIDTH), row1(SSD_WIDTH)],
                [jax.ShapeDtypeStruct((bsz, s, SSD_WIDTH), F32), jax.ShapeDtypeStruct((bsz, s, RET_WIDTH), F32)],
                lambda tokc: [tokc(SSD_WIDTH), tokc(RET_WIDTH)])


def _rms(v, g):
    return v * lax.rsqrt(jnp.mean(v * v, axis=-1, keepdims=True) + RMS_EPS) * g


def _mla_proj_kernel(cq_ref, ckv_ref, small_ref, cos_ref, sa_ref, sb_ref, qn_ref, kvn_ref,
                     wq_ref, wk_ref, wv_ref, q_ref, k_ref, v_ref):
    cqn = _rms(cq_ref[0], qn_ref[...]).astype(BF16)
    ckvn = _rms(ckv_ref[0], kvn_ref[...]).astype(BF16)
    q = _dot(cqn, wq_ref[...])
    kn = _dot(ckvn, wk_ref[...])
    vlane = lax.broadcasted_iota(jnp.int32, (1, MLA_PAIRS * MLA_V_SLOT), 1)
    ones = jnp.where(vlane % MLA_V_SLOT >= 2 * MLA_V_DIM, 1.0, 0.0)
    v_ref[0] = (_dot(ckvn, wv_ref[...]) + ones).astype(BF16)
    cos_t, sin_a, sin_b = cos_ref[...], sa_ref[...], sb_ref[...]
    lane = lax.broadcasted_iota(jnp.int32, small_ref.shape[1:], 1)
    in_rope = jnp.logical_and(lane >= KROPE_LANE, lane < KROPE_LANE + MLA_ROPE_DIM)
    kr = _rope3(jnp.where(in_rope, small_ref[0], 0.0), cos_t, sin_a, sin_b, MLA_ROPE_DIM // 2)
    for hd in range(MLA_HEADS):
        sl = slice(hd * MLA_HEAD_SLOT, (hd + 1) * MLA_HEAD_SLOT)
        q_ref[0, :, sl] = (_rope3(q[:, sl], cos_t, sin_a, sin_b, MLA_ROPE_DIM // 2) * MLA_Q_SCALE).astype(BF16)
        k_ref[0, :, sl] = (kn[:, sl] + kr).astype(BF16)


def _mla_proj(cq, ckv, small, cos_t, sin_a, sin_b, qn, kvn, wq, wk, wv):
    bsz, s, _ = cq.shape
    nt = s // TOKEN_TILE
    hw = MLA_HEADS * MLA_HEAD_SLOT
    vw = MLA_PAIRS * MLA_V_SLOT
    tok = lambda wd: pl.BlockSpec((1, TOKEN_TILE, wd), lambda b, t: (b, t, 0))
    tab = pl.BlockSpec((TOKEN_TILE, LANES), lambda b, t: (t, 0))
    full = lambda a: pl.BlockSpec(a.shape, lambda b, t: (0,) * a.ndim)
    return pl.pallas_call(
        _mla_proj_kernel,
        out_shape=[jax.ShapeDtypeStruct((bsz, s, hw), BF16),
                   jax.ShapeDtypeStruct((bsz, s, hw), BF16),
                   jax.ShapeDtypeStruct((bsz, s, vw), BF16)],
        grid=(bsz, nt),
        in_specs=[tok(MLA_Q_LORA), tok(MLA_KV_LORA), tok(LANES), tab, tab, tab,
                  full(qn), full(kvn), full(wq), full(wk), full(wv)],
        out_specs=[tok(hw), tok(hw), tok(vw)],
        compiler_params=_params("parallel", "arbitrary"),
        name="mla_proj",
    )(cq, ckv, small, cos_t, sin_a, sin_b, qn, kvn, wq, wk, wv)


ATTN_KEY_CHUNK = 256


def _attn_step(q_ref, k_ref, v_ref, g_ref, o_ref, slot_a, slot_b, n_keys):
    heads = range(2)
    hsl = [slice(hd * MLA_HEAD_SLOT, (hd + 1) * MLA_HEAD_SLOT) for hd in heads]
    if slot_a is not None:
        q = [q_ref[0, :, hsl[hd]] for hd in heads]
        mcol = [None, None]
    if slot_b is not None:
        mrow = [slot_b[1][hd] for hd in heads]
        pv = [None, None]
    for c in range(n_keys // ATTN_KEY_CHUNK):
        ks = slice(c * ATTN_KEY_CHUNK, (c + 1) * ATTN_KEY_CHUNK)
        for hd in heads:
            if slot_a is not None:
                s = _dot_nt(q[hd], k_ref[0, ks, hsl[hd]])
                slot_a[0][hd, :, ks] = s
                cm = s[:, :LANES]
                for w in range(1, ATTN_KEY_CHUNK // LANES):
                    cm = jnp.maximum(cm, s[:, w * LANES:(w + 1) * LANES])
                mcol[hd] = cm if c == 0 else jnp.maximum(mcol[hd], cm)
            if slot_b is not None:
                p = jnp.exp2(slot_b[0][hd, :, ks] - mrow[hd]).astype(BF16)
                part = _dot(p, v_ref[0, ks, :])
                pv[hd] = part if c == 0 else pv[hd] + part
    if slot_a is not None:
        for hd in heads:
            slot_a[1][hd] = jnp.max(mcol[hd], axis=-1, keepdims=True)
    if slot_b is not None:
        outs = [pv[hd][:, :LANES] / pv[hd][:, LANES:] for hd in heads]
        lane = lax.broadcasted_iota(jnp.int32, outs[0].shape, 1)
        o = jnp.where(lane < MLA_V_DIM, outs[0], outs[1])
        o_ref[0] = o * g_ref[0]


def _mla_attn_ctx_kernel(q_ref, k_ref, v_ref, g_ref, o_ref, s_ref, m_ref):
    n_keys = k_ref.shape[1]
    _attn_step(q_ref, k_ref, v_ref, g_ref, o_ref, (s_ref, m_ref), None, n_keys)
    _attn_step(q_ref, k_ref, v_ref, g_ref, o_ref, None, (s_ref, m_ref), n_keys)


def _mla_attn_ctx(q, k, v, gate, ctx_tiles):
    bsz, s, _ = q.shape
    tq = TOKEN_TILE
    n_ctx = ctx_tiles * tq
    pair_w = 2 * MLA_HEAD_SLOT
    return pl.pallas_call(
        _mla_attn_ctx_kernel,
        out_shape=jax.ShapeDtypeStruct((bsz, s, MLA_WIDTH), F32),
        grid=(bsz, MLA_PAIRS, ctx_tiles),
        in_specs=[pl.BlockSpec((1, tq, pair_w), lambda b, j, t: (b, t, j)),
                  pl.BlockSpec((1, n_ctx, pair_w), lambda b, j, t: (b, 0, j)),
                  pl.BlockSpec((1, n_ctx, MLA_V_SLOT), lambda b, j, t: (b, 0, j)),
                  pl.BlockSpec((1, tq, LANES), lambda b, j, t: (b, t, j))],
        out_specs=pl.BlockSpec((1, tq, LANES), lambda b, j, t: (b, t, j)),
        scratch_shapes=[pltpu.VMEM((2, tq, n_ctx), F32), pltpu.VMEM((2, tq, 1), F32)],
        compiler_params=_params("parallel", "arbitrary", "arbitrary"),
        name="mla_attn_ctx",
    )(q, k, v, gate)


def _mla_attn_lat_kernel(*refs, n_tiles, aliased):
    if aliased:
        refs = refs[1:]
    q_ref, k_ref, v_ref, g_ref, o_ref, s0, m0, s1, m1 = refs
    n_keys = k_ref.shape[1]
    slots = ((s0, m0), (s1, m1))
    n = pl.program_id(0)

    @pl.when(n == 0)
    def _():
        _attn_step(q_ref, k_ref, v_ref, g_ref, o_ref, slots[0], None, n_keys)

    for par in range(2):
        @pl.when(jnp.logical_and(jnp.logical_and(n > 0, n < n_tiles), n % 2 == par))
        def _():
            _attn_step(q_ref, k_ref, v_ref, g_ref, o_ref, slots[par], slots[1 - par], n_keys)

    @pl.when(n == n_tiles)
    def _():
        _attn_step(q_ref, k_ref, v_ref, g_ref, o_ref, None, slots[(n_tiles - 1) % 2], n_keys)


def _mla_attn_lat(q, k, v, gate, ctx_tiles, ctx_out=None):
    bsz, s, _ = q.shape
    tq = TOKEN_TILE
    nq = s // tq - ctx_tiles
    n_tiles = bsz * MLA_PAIRS * nq
    pair_w = 2 * MLA_HEAD_SLOT
    out_lo = 0 if ctx_out is not None else ctx_tiles

    def decode(i):
        return i // (MLA_PAIRS * nq), (i // nq) % MLA_PAIRS, i % nq + ctx_tiles

    def ahead(n):
        return decode(jnp.minimum(n, n_tiles - 1))

    def behind(n):
        return decode(jnp.maximum(n - 1, 0))

    def q_map(n):
        b, j, t = ahead(n)
        return b, t, j

    def k_map(n):
        b, j, _ = ahead(n)
        return b, 0, j

    def v_map(n):
        b, j, _ = behind(n)
        return b, 0, j

    def g_map(n):
        b, j, t = behind(n)
        return b, t, j

    def o_map(n):
        b, j, t = behind(n)
        return b, t - out_lo, j

    in_specs = [pl.BlockSpec((1, tq, pair_w), q_map),
                pl.BlockSpec((1, s, pair_w), k_map),
                pl.BlockSpec((1, s, MLA_V_SLOT), v_map),
                pl.BlockSpec((1, tq, LANES), g_map)]
    args = [q, k, v, gate]
    aliases = {}
    if ctx_out is not None:
        in_specs = [pl.BlockSpec(memory_space=pl.ANY)] + in_specs
        args = [ctx_out] + args
        aliases = {0: 0}
    kern = functools.partial(_mla_attn_lat_kernel, n_tiles=n_tiles, aliased=ctx_out is not None)
    return pl.pallas_call(
        kern,
        out_shape=jax.ShapeDtypeStruct((bsz, s - out_lo * tq, MLA_WIDTH), F32),
        grid=(n_tiles + 1,),
        in_specs=in_specs,
        out_specs=pl.BlockSpec((1, tq, LANES), o_map),
        scratch_shapes=[pltpu.VMEM((2, tq, s), F32), pltpu.VMEM((2, tq, 1), F32),
                        pltpu.VMEM((2, tq, s), F32), pltpu.VMEM((2, tq, 1), F32)],
        input_output_aliases=aliases,
        compiler_params=_params("arbitrary"),
        name="mla_attn",
    )(*args)


def _outproj_kernel(*refs, n_stream, ctx_tiles, t_lo, alpha):
    ssd_ref, mla_ref, ret_ref = refs[:3]
    x_refs = refs[3:3 + n_stream]
    mod_ref, w_ref, g_ref, b_ref, o_ref = refs[3 + n_stream:]
    acc = _dot(ssd_ref[0].astype(BF16), w_ref[:SSD_WIDTH, :])
    acc = acc + _dot(mla_ref[0].astype(BF16), w_ref[SSD_WIDTH:SSD_WIDTH + MLA_WIDTH, :])
    acc = acc + _dot(ret_ref[0].astype(BF16), w_ref[SSD_WIDTH + MLA_WIDTH:, :])
    gate = mod_ref[0, 0][:, 2 * D_MODEL:]
    x = _stream_tile(x_refs, pl.program_id(1) + t_lo, ctx_tiles)
    r = alpha * x + gate * acc
    mu = jnp.mean(r, axis=-1, keepdims=True)
    var = jnp.mean(jnp.square(r - mu), axis=-1, keepdims=True)
    o_ref[0] = (r - mu) * lax.rsqrt(var + LN_EPS) * g_ref[...] + b_ref[...]


def _outproj(ssd_o, mla_o, mla_lo, ret_o, stream, mod_all, w, ln_g, ln_b, t_lo, ctx_tiles, alpha):
    bsz, s, _ = ssd_o.shape
    d = stream[0].shape[-1]
    nt = s // TOKEN_TILE
    tok = lambda wd: pl.BlockSpec((1, TOKEN_TILE, wd), lambda b, t: (b, t + t_lo, 0))
    kern = functools.partial(_outproj_kernel, n_stream=len(stream), ctx_tiles=ctx_tiles, t_lo=t_lo, alpha=alpha)
    return pl.pallas_call(
        kern,
        out_shape=jax.ShapeDtypeStruct((bsz, s - t_lo * TOKEN_TILE, d), F32),
        grid=(bsz, nt - t_lo),
        in_specs=[tok(SSD_WIDTH),
                  pl.BlockSpec((1, TOKEN_TILE, MLA_WIDTH), lambda b, t: (b, t + t_lo - mla_lo, 0)),
                  tok(RET_WIDTH)] + _stream_specs(stream, t_lo, ctx_tiles) + [
                  pl.BlockSpec((1, 1, 1, 3 * d),
                               lambda b, t: (b, jnp.minimum(t + t_lo, ctx_tiles) // ctx_tiles, 0, 0)),
                  pl.BlockSpec(w.shape, lambda b, t: (0, 0)),
                  pl.BlockSpec((1, d), lambda b, t: (0, 0)),
                  pl.BlockSpec((1, d), lambda b, t: (0, 0))],
        out_specs=pl.BlockSpec((1, TOKEN_TILE, d), lambda b, t: (b, t, 0)),
        compiler_params=_params("parallel", "arbitrary"),
        name="out_proj",
    )(ssd_o, mla_o, ret_o, *stream, mod_all, w, ln_g.reshape(1, d), ln_b.reshape(1, d))


def _rope_tables(n_ctx, n_lat, rot_dim, width, lane0, period):
    t = np.arange(n_lat)
    row = (t // GRID_W).astype(np.float32)
    col = (t % GRID_W).astype(np.float32)
    n_freq = rot_dim // 4
    inv = (ROPE_THETA ** (-np.arange(n_freq, dtype=np.float32) / n_freq)).astype(np.float32)
    ang = np.concatenate([row[:, None] * inv, col[:, None] * inv], -1).astype(np.float32)
    cos, sin = np.cos(ang), np.sin(ang)
    half = rot_dim // 2
    lane = np.arange(width) % period - lane0
    in_rot = (lane >= 0) & (lane < rot_dim)
    idx = np.where(in_rot, lane % half, 0)
    first = in_rot & (lane < half)
    second = in_rot & (lane >= half)
    cos_t = np.where(in_rot[None, :], cos[:, idx], 1.0)
    sin_a = np.where(first[None, :], -sin[:, idx], 0.0)
    sin_b = np.where(second[None, :], sin[:, idx], 0.0)
    ctx_pad = lambda a, v: np.concatenate([np.full((n_ctx, width), v), a], axis=0).astype(np.float32)
    return ctx_pad(cos_t, 1.0), ctx_pad(sin_a, 0.0), ctx_pad(sin_b, 0.0)


def _mla_weights(w_uq, w_ukv):
    qd = MLA_NOPE_DIM + MLA_ROPE_DIM
    wq = w_uq.reshape(MLA_Q_LORA, MLA_HEADS, qd)
    wq = jnp.pad(wq, ((0, 0), (0, 0), (0, MLA_HEAD_SLOT - qd))).reshape(MLA_Q_LORA, MLA_HEADS * MLA_HEAD_SLOT)
    wkv = w_ukv.reshape(MLA_KV_LORA, MLA_HEADS, MLA_NOPE_DIM + MLA_V_DIM)
    wk = jnp.pad(wkv[..., :MLA_NOPE_DIM], ((0, 0), (0, 0), (0, MLA_HEAD_SLOT - MLA_NOPE_DIM)))
    wk = wk.reshape(MLA_KV_LORA, MLA_HEADS * MLA_HEAD_SLOT)
    wv = wkv[..., MLA_NOPE_DIM:].reshape(MLA_KV_LORA, MLA_PAIRS, 2 * MLA_V_DIM)
    wv = jnp.pad(wv, ((0, 0), (0, 0), (0, MLA_V_SLOT - 2 * MLA_V_DIM))).reshape(MLA_KV_LORA, MLA_PAIRS * MLA_V_SLOT)
    return wq.astype(BF16), wk.astype(BF16), wv.astype(BF16)


def _lane_pad(v):
    return jnp.pad(v, (0, LANES - v.shape[0])).reshape(1, LANES)


def kernel(x, c, ctx, c_ctx, w_ada, b_ada, w_in, ssd_conv_w, ssd_conv_b, ssd_a_log_f, ssd_a_log_b,
           ssd_dt_bias_f, ssd_dt_bias_b, ssd_d, ssd_norm_w, mla_q_norm, mla_w_uq, mla_kv_norm, mla_w_ukv,
           ret_log_rate_f, ret_log_rate_b, w_out, ln_g, ln_b):
    bsz, n_lat, d = x.shape
    n_ctx = ctx.shape[1]
    depth = w_in.shape[0]
    assert d == D_MODEL and n_ctx == TOKEN_TILE and n_lat % TOKEN_TILE == 0
    alpha = (2 * depth) ** 0.25
    s = n_ctx + n_lat
    nc_ctx = n_ctx // CHUNK
    ctx_tiles = n_ctx // TOKEN_TILE
    nt = s // TOKEN_TILE

    cos_m, sa_m, sb_m = _rope_tables(n_ctx, n_lat, MLA_ROPE_DIM, LANES, MLA_ROPE_LANE, LANES)
    cos_r, sa_r, sb_r = _rope_tables(n_ctx, n_lat, RET_QK_DIM, RET_QK_WIDTH, 0, RET_QK_DIM)

    rows = 16
    cc = jnp.zeros((rows, d), F32).at[:bsz].set(c).at[bsz].set(c_ctx)
    stream = (ctx, x)

    for i in range(depth):
        last = i == depth - 1
        mod = _ada(cc, w_ada, b_ada, i)
        mod_all = jnp.stack([jnp.broadcast_to(mod[bsz], (bsz, 3 * d)), mod[:bsz]], axis=1)
        mod_all = mod_all.reshape(bsz, 2, 1, 3 * d)
        both = lambda f, b: jnp.stack([f, b], axis=0)
        head_rows = lambda v: jnp.broadcast_to(v[:, None], (SSD_HEADS, TOKEN_TILE))
        alogt = both(head_rows(ssd_a_log_f[i]), head_rows(ssd_a_log_b[i]))
        dtbt = both(head_rows(ssd_dt_bias_f[i]), head_rows(ssd_dt_bias_b[i]))
        (zs, xs_bf, xst_bf, bc, colp, rowp, small, cq, ckv, mgs, rq, rk_bf, rkt, rv_bf, rgs) = _inproj(
            stream, mod_all, _relayout_w_in(w_in, i), ssd_conv_w[i], ssd_conv_b[i].reshape(1, -1), alogt, dtbt,
            cos_r, sa_r, sb_r, ctx_tiles)

        dwide = jnp.repeat(ssd_d[i], SSD_HEAD_DIM).reshape(1, SSD_WIDTH)
        rates = both(_lane_pad(ret_log_rate_f[i]), _lane_pad(ret_log_rate_b[i]))
        prep = (xs_bf, xst_bf, bc, colp, rowp, rq, rk_bf, rkt, rv_bf)
        ssd_o, ret_o = _mixer_scan(prep, zs, rgs, dwide, ssd_norm_w[i].reshape(1, -1), _ret_consts(rates), nc_ctx)

        wq, wk, wv = _mla_weights(mla_w_uq[i], mla_w_ukv[i])
        q, k, v = _mla_proj(cq, ckv, small, cos_m, sa_m, sb_m, mla_q_norm[i].reshape(1, -1),
                            mla_kv_norm[i].reshape(1, -1), wq, wk, wv)
        mla_lo = ctx_tiles if last else 0
        mla_o = _mla_attn_lat(q, k, v, mgs, ctx_tiles, None if last else _mla_attn_ctx(q, k, v, mgs, ctx_tiles))

        out = _outproj(ssd_o, mla_o, mla_lo, ret_o, stream, mod_all, w_out[i].astype(BF16), ln_g[i], ln_b[i],
                       ctx_tiles if last else 0, ctx_tiles, alpha)
        stream = (out,)
    return stream[0]
```

```python
import functools
import math

import jax
import jax.numpy as jnp
import numpy as np
from jax import lax
from jax.experimental import pallas as pl
from jax.experimental.pallas import tpu as pltpu

F32 = jnp.float32
BF16 = jnp.bfloat16

LANES = 128
SUBLANES = 8
VMEM_LIMIT_BYTES = 56 * 1024 * 1024

GRID_W = 64
D_MODEL = 1024
SSD_WIDTH = 1024
SSD_HEAD_DIM = 64
SSD_HEADS = 16
SSD_GROUPS = 2
SSD_STATE = 128
SSD_CONV = 5
SSD_CONV_DIM = SSD_WIDTH + 2 * SSD_GROUPS * SSD_STATE
MLA_WIDTH = 512
MLA_V_DIM = 64
MLA_HEADS = 8
MLA_NOPE_DIM = 64
MLA_ROPE_DIM = 32
MLA_Q_LORA = 384
MLA_KV_LORA = 256
MLA_SCALE = (MLA_NOPE_DIM + MLA_ROPE_DIM) ** -0.5
RET_WIDTH = 512
RET_HEADS = 4
RET_V_DIM = 128
RET_QK_DIM = 64
RET_QK_WIDTH = RET_HEADS * RET_QK_DIM
CHUNK = 128
ROPE_THETA = 10000.0
LN_EPS = 1e-5
RMS_EPS = 1e-6
IN_WIDTHS = (SSD_WIDTH, SSD_CONV_DIM, SSD_HEADS, MLA_Q_LORA, MLA_KV_LORA, MLA_ROPE_DIM, MLA_WIDTH,
             RET_QK_WIDTH, RET_QK_WIDTH, RET_WIDTH, RET_WIDTH)

TOKEN_TILE = 256
MLA_HEAD_SLOT = 128
MLA_ROPE_LANE = MLA_NOPE_DIM
MLA_PAIRS = MLA_HEADS // 2
MLA_V_SLOT = 256
MLA_Q_SCALE = MLA_SCALE * math.log2(math.e)
NEG_BIG = -1e30


def _params(*sem):
    return pltpu.CompilerParams(dimension_semantics=sem, vmem_limit_bytes=VMEM_LIMIT_BYTES)


def _silu(v):
    return v * jax.nn.sigmoid(v)


def _softplus(v):
    return jnp.maximum(v, 0.0) + jnp.log1p(jnp.exp(-jnp.abs(v)))


def _dot(a, b):
    return jnp.dot(a, b, preferred_element_type=F32)


def _dot_exact(a, b):
    return jnp.dot(a, b, preferred_element_type=F32, precision=lax.Precision.HIGHEST)


def _dot_nt(a, b):
    return lax.dot_general(a, b, (((1,), (1,)), ((), ())), preferred_element_type=F32)


def _ada_kernel(c_ref, w_ref, b_ref, o_ref):
    o_ref[...] = _dot_exact(_silu(c_ref[...]), w_ref[0]) + b_ref[0]


def _ada(cc, w_all, b_all, layer):
    rows, d = cc.shape
    depth, _, n = w_all.shape
    tn = 1024
    return pl.pallas_call(
        _ada_kernel,
        out_shape=jax.ShapeDtypeStruct((rows, n), F32),
        grid=(n // tn,),
        in_specs=[pl.BlockSpec((rows, d), lambda j: (0, 0)),
                  pl.BlockSpec((1, d, tn), lambda j: (layer, 0, j)),
                  pl.BlockSpec((1, 1, tn), lambda j: (layer, 0, j))],
        out_specs=pl.BlockSpec((rows, tn), lambda j: (0, j)),
        compiler_params=_params("arbitrary"),
        name="ada_mod",
    )(cc, w_all, b_all.reshape(depth, 1, n))


def _chunk_index(d, c, nc_ctx, nc):
    bwd = jnp.where(c < nc_ctx, nc_ctx - 1 - c, nc + nc_ctx - 1 - c)
    return jnp.where(d == 0, c, bwd)


def _tri_masks(d):
    ii = lax.broadcasted_iota(jnp.int32, (CHUNK, CHUNK), 0)
    jj = lax.broadcasted_iota(jnp.int32, (CHUNK, CHUNK), 1)
    sgn = 1 - 2 * d
    mask = (ii - jj) * sgn >= 0
    mask_t = (jj - ii) * sgn >= 0
    return mask, mask_t


def _rope3(v, cos_t, sin_a, sin_b, shift):
    width = v.shape[-1]
    return v * cos_t + pltpu.roll(v, width - shift, 1) * sin_a + pltpu.roll(v, shift, 1) * sin_b


PROJ_WIDTHS = (SSD_WIDTH, SSD_CONV_DIM, LANES, MLA_Q_LORA, MLA_KV_LORA, MLA_WIDTH,
               RET_QK_WIDTH, RET_QK_WIDTH, RET_WIDTH, RET_WIDTH)
PROJ_OFFSETS = tuple(int(o) for o in np.cumsum((0,) + PROJ_WIDTHS[:-1]))
DT_LANE = 0
KROPE_LANE = MLA_ROPE_LANE
ROW_CUM, ROW_DT, ROW_WEND, ROW_ETOT = 0, 16, 32, 48
ROW_PACK = 64
TILE_CHUNKS = TOKEN_TILE // CHUNK


IN_OFFSETS = tuple(int(o) for o in np.cumsum((0,) + IN_WIDTHS[:-1]))
RELAYOUT_MOVES = ((0, 0, 0), (1, 1, 0), (2, 2, DT_LANE), (3, 3, 0), (4, 4, 0), (5, 2, KROPE_LANE), (6, 5, 0),
                  (7, 6, 0), (8, 7, 0), (9, 8, 0), (10, 9, 0))
RELAYOUT_ROWS = 128


def _relayout_w_in_kernel(w_ref, o_ref):
    small = PROJ_OFFSETS[2]
    o_ref[:, small:small + LANES] = jnp.zeros((o_ref.shape[0], LANES), BF16)
    for src, dst, lane in RELAYOUT_MOVES:
        lo = PROJ_OFFSETS[dst] + lane
        o_ref[:, lo:lo + IN_WIDTHS[src]] = w_ref[0, :, IN_OFFSETS[src]:IN_OFFSETS[src] + IN_WIDTHS[src]].astype(BF16)


def _relayout_w_in(w_in_all, layer):
    _, d, n_in = w_in_all.shape
    n_out = sum(PROJ_WIDTHS)
    return pl.pallas_call(
        _relayout_w_in_kernel,
        out_shape=jax.ShapeDtypeStruct((d, n_out), BF16),
        grid=(d // RELAYOUT_ROWS,),
        in_specs=[pl.BlockSpec((1, RELAYOUT_ROWS, n_in), lambda r: (layer, r, 0))],
        out_specs=pl.BlockSpec((RELAYOUT_ROWS, n_out), lambda r: (r, 0)),
        compiler_params=_params("arbitrary"),
        name="w_in_relayout",
    )(w_in_all)


def _stream_specs(stream, t_lo, ctx_tiles):
    d = stream[0].shape[-1]
    if len(stream) == 1:
        return [pl.BlockSpec((1, TOKEN_TILE, d), lambda b, t: (b, t + t_lo, 0))]
    return [pl.BlockSpec((1, TOKEN_TILE, d), lambda b, t: (b, jnp.minimum(t + t_lo, ctx_tiles - 1), 0)),
            pl.BlockSpec((1, TOKEN_TILE, d), lambda b, t: (b, jnp.maximum(t + t_lo - ctx_tiles, 0), 0))]


def _halo_specs(stream, ctx_tiles):
    arr = stream[-1]
    d = arr.shape[-1]
    shift = 0 if len(stream) == 1 else ctx_tiles
    per_tile = TOKEN_TILE // SUBLANES
    nblk = arr.shape[1] // SUBLANES
    prev = lambda b, t: (b, jnp.clip((t - shift) * per_tile - 1, 0, nblk - 1), 0)
    nxt = lambda b, t: (b, jnp.clip((t - shift + 1) * per_tile, 0, nblk - 1), 0)
    return [pl.BlockSpec((1, SUBLANES, d), prev), pl.BlockSpec((1, SUBLANES, d), nxt)]


def _stream_tile(refs, t, ctx_tiles):
    if len(refs) == 1:
        return refs[0][0]
    return jnp.where(t < ctx_tiles, refs[0][0], refs[1][0])


def _inproj_kernel(*refs, n_stream, ctx_tiles, nt):
    x_refs = refs[:n_stream]
    (prev_ref, next_ref, mod_ref, w_ref, convw_ref, convb_ref, alogt_ref, dtbt_ref,
     cos_ref, sa_ref, sb_ref) = refs[n_stream:n_stream + 11]
    (zs_ref, xs_ref, xst_ref, bc_ref, colp_ref, rowp_ref, small_ref, cq_ref, ckv_ref, mgs_ref,
     q_ref, kbf_ref, kt_ref, v_ref, rgs_ref) = refs[n_stream + 11:]
    t = pl.program_id(1)
    mod = mod_ref[0, 0]
    sh = mod[:, :D_MODEL]
    sc = mod[:, D_MODEL:2 * D_MODEL]
    modulate = lambda v: (v * (1.0 + sc) + sh).astype(BF16)
    h = modulate(_stream_tile(x_refs, t, ctx_tiles))
    h_ext = jnp.concatenate([modulate(prev_ref[0]), h, modulate(next_ref[0])], axis=0)

    def proj(g, lo=0, width=None, lhs=None):
        off = PROJ_OFFSETS[g] + lo
        width = PROJ_WIDTHS[g] - lo if width is None else width
        return _dot(h if lhs is None else lhs, w_ref[:, off:off + width])

    first = jnp.logical_or(t == 0, t == ctx_tiles)
    last = jnp.logical_or(t == ctx_tiles - 1, t == nt - 1)
    row = lax.broadcasted_iota(jnp.int32, (TOKEN_TILE + 2 * SUBLANES, 1), 0)
    keep = jnp.where(jnp.logical_or(jnp.logical_and(row < SUBLANES, first),
                                    jnp.logical_and(row >= SUBLANES + TOKEN_TILE, last)), 0.0, 1.0)
    unit_w = 2 * LANES
    ext_parts = {}

    def ext_unit(e):
        ext_parts[e] = proj(1, e * unit_w, unit_w, lhs=h_ext) * keep

    def conv_strip(j, anchor):
        pad = SSD_CONV // 2
        sl = slice(j * LANES, (j + 1) * LANES)
        esl = slice((j % 2) * LANES, (j % 2 + 1) * LANES)
        ext = ext_parts[j // 2]
        acc = convb_ref[:, sl] + anchor
        for k in range(SSD_CONV):
            lo = SUBLANES - pad + k
            acc = acc + convw_ref[k:k + 1, sl] * ext[lo:lo + TOKEN_TILE, esl]
        u = _silu(acc)
        if j < SSD_WIDTH // LANES:
            xs_ref[0, :, sl] = u.astype(BF16)
            xst_ref[0, sl, :] = u.T.astype(BF16)
        else:
            bc_ref[0, :, j * LANES - SSD_WIDTH:(j + 1) * LANES - SSD_WIDTH] = u

    def gate_unit(g, ref, lo, width):
        def run():
            val = proj(g, lo, width)
            ref[0, :, lo:lo + width] = _silu(val)
            return val
        return run

    def plain_unit(g, ref, cast=None):
        def run():
            val = proj(g)
            ref[0] = val if cast is None else val.astype(cast)
            return val
        return run

    def ret_q_unit():
        val = proj(6)
        q_ref[0] = _rope3(val, cos_ref[...], sa_ref[...], sb_ref[...], RET_QK_DIM // 2)
        return val

    def ret_k_unit():
        val = proj(7)
        kk = _rope3(val, cos_ref[...], sa_ref[...], sb_ref[...], RET_QK_DIM // 2) * (RET_QK_DIM ** -0.5)
        kbf_ref[0] = kk.astype(BF16)
        kt_ref[0] = kk.T
        return val

    units = [gate_unit(0, zs_ref, lo, unit_w) for lo in range(0, SSD_WIDTH, unit_w)]
    units += [plain_unit(3, cq_ref), plain_unit(4, ckv_ref)]
    units += [gate_unit(5, mgs_ref, lo, unit_w) for lo in range(0, MLA_WIDTH, unit_w)]
    units += [ret_q_unit, ret_k_unit, plain_unit(8, v_ref, BF16)]
    units += [gate_unit(9, rgs_ref, lo, unit_w) for lo in range(0, RET_WIDTH, unit_w)]
    small = proj(2)
    small_ref[0] = small
    n_strips = SSD_CONV_DIM // LANES
    ext_unit(0)
    anchor = jnp.zeros((1, LANES), F32)
    for j in range(n_strips):
        if j % 2 == 0 and j // 2 + 1 < n_strips // 2:
            ext_unit(j // 2 + 1)
        nxt = units.pop(0)()[0:1, :LANES] * 0.0 if units else anchor
        conv_strip(j, anchor)
        anchor = nxt
    for unit in units:
        unit()

    dt_raw_t = small.T[DT_LANE:DT_LANE + SSD_HEADS, :]
    pad_rows = jnp.zeros((CHUNK - SSD_HEADS, CHUNK), F32)
    for d in range(2):
        _, mask_t = _tri_masks(d)
        tri_t = jnp.where(mask_t, 1.0, 0.0)
        dt_all = _softplus(dt_raw_t + dtbt_ref[d])
        la_all = dt_all * -jnp.exp(alogt_ref[d])
        base = d * ROW_PACK
        for j in range(TILE_CHUNKS):
            cs = slice(j * CHUNK, (j + 1) * CHUNK)
            dt_row, la_row = dt_all[:, cs], la_all[:, cs]
            cum_row = _dot_exact(la_row, tri_t)
            colp_ref[0, d, cs, :] = jnp.concatenate([cum_row, pad_rows], axis=0).T
            tot_row = jnp.sum(la_row, axis=1, keepdims=True)
            rowp_ref[0, j, base + ROW_CUM:base + ROW_CUM + SSD_HEADS, :] = cum_row
            rowp_ref[0, j, base + ROW_DT:base + ROW_DT + SSD_HEADS, :] = dt_row
            rowp_ref[0, j, base + ROW_WEND:base + ROW_WEND + SSD_HEADS, :] = dt_row * jnp.exp(tot_row - cum_row)
            rowp_ref[0, j, base + ROW_ETOT:base + ROW_ETOT + SSD_HEADS, :] = jnp.broadcast_to(
                jnp.exp(tot_row), (SSD_HEADS, CHUNK))


def _inproj(stream, mod_all, w, conv_w, conv_b, alogt, dtbt, cos_t, sin_a, sin_b, ctx_tiles):
    bsz, d = stream[0].shape[0], stream[0].shape[-1]
    s = sum(a.shape[1] for a in stream)
    nt = s // TOKEN_TILE
    nc = s // CHUNK
    tok = lambda wd: pl.BlockSpec((1, TOKEN_TILE, wd), lambda b, t: (b, t, 0))
    chan = lambda wd: pl.BlockSpec((1, wd, TOKEN_TILE), lambda b, t: (b, 0, t))
    full = lambda a: pl.BlockSpec(a.shape, lambda b, t: (0,) * a.ndim)
    tab = pl.BlockSpec((TOKEN_TILE, RET_QK_WIDTH), lambda b, t: (t, 0))
    bcw = SSD_CONV_DIM - SSD_WIDTH
    outs = [((bsz, s, SSD_WIDTH), F32, tok(SSD_WIDTH)),
            ((bsz, s, SSD_WIDTH), BF16, tok(SSD_WIDTH)),
            ((bsz, SSD_WIDTH, s), BF16, chan(SSD_WIDTH)),
            ((bsz, s, bcw), F32, tok(bcw)),
            ((bsz, 2, s, LANES), F32, pl.BlockSpec((1, 2, TOKEN_TILE, LANES), lambda b, t: (b, 0, t, 0))),
            ((bsz, nc, 2 * ROW_PACK, CHUNK), F32,
             pl.BlockSpec((1, TILE_CHUNKS, 2 * ROW_PACK, CHUNK), lambda b, t: (b, t, 0, 0))),
            ((bsz, s, LANES), F32, tok(LANES)),
            ((bsz, s, MLA_Q_LORA), F32, tok(MLA_Q_LORA)),
            ((bsz, s, MLA_KV_LORA), F32, tok(MLA_KV_LORA)),
            ((bsz, s, MLA_WIDTH), F32, tok(MLA_WIDTH)),
            ((bsz, s, RET_QK_WIDTH), F32, tok(RET_QK_WIDTH)),
            ((bsz, s, RET_QK_WIDTH), BF16, tok(RET_QK_WIDTH)),
            ((bsz, RET_QK_WIDTH, s), F32, chan(RET_QK_WIDTH)),
            ((bsz, s, RET_WIDTH), BF16, tok(RET_WIDTH)),
            ((bsz, s, RET_WIDTH), F32, tok(RET_WIDTH))]
    kern = functools.partial(_inproj_kernel, n_stream=len(stream), ctx_tiles=ctx_tiles, nt=nt)
    return pl.pallas_call(
        kern,
        out_shape=[jax.ShapeDtypeStruct(shape, dt) for shape, dt, _ in outs],
        grid=(bsz, nt),
        in_specs=_stream_specs(stream, 0, ctx_tiles) + _halo_specs(stream, ctx_tiles) + [
            pl.BlockSpec((1, 1, 1, 3 * d), lambda b, t: (b, jnp.minimum(t, ctx_tiles) // ctx_tiles, 0, 0)),
            pl.BlockSpec(w.shape, lambda b, t: (0, 0), pipeline_mode=pl.Buffered(1)),
            full(conv_w), full(conv_b), full(alogt), full(dtbt), tab, tab, tab],
        out_specs=[spec for _, _, spec in outs],
        compiler_params=_params("parallel", "arbitrary"),
        name="in_proj",
    )(*stream, stream[-1], stream[-1], mod_all, w, conv_w, conv_b, alogt, dtbt, cos_t, sin_a, sin_b)


def _ret_consts_kernel(rate_ref, dec_ref, ecum_ref, wend_ref, etot_ref):
    d = pl.program_id(0)
    mask, _ = _tri_masks(d)
    ii = lax.broadcasted_iota(jnp.int32, (CHUNK, CHUNK), 0)
    jj = lax.broadcasted_iota(jnp.int32, (CHUNK, CHUNK), 1)
    dist = jnp.abs(ii - jj).astype(F32)
    pos_i = lax.broadcasted_iota(jnp.int32, (CHUNK, RET_QK_DIM), 0)
    steps_col = jnp.where(d == 0, pos_i + 1, CHUNK - pos_i).astype(F32)
    pos_j = lax.broadcasted_iota(jnp.int32, (RET_QK_DIM, CHUNK), 1)
    left_row = jnp.where(d == 0, CHUNK - 1 - pos_j, pos_j).astype(F32)
    for hd in range(RET_HEADS):
        la = -jnp.exp(rate_ref[0, :, hd:hd + 1])
        dec_ref[0, hd] = jnp.exp(jnp.where(mask, dist * la, NEG_BIG))
        ecum_ref[0, :, hd * RET_QK_DIM:(hd + 1) * RET_QK_DIM] = jnp.exp(steps_col * la)
        wend_ref[0, hd * RET_QK_DIM:(hd + 1) * RET_QK_DIM, :] = jnp.exp(left_row * la)
        etot_ref[0, hd] = jnp.broadcast_to(jnp.exp(CHUNK * la), (1, LANES))


def _ret_consts(rates):
    return pl.pallas_call(
        _ret_consts_kernel,
        out_shape=[jax.ShapeDtypeStruct((2, RET_HEADS, CHUNK, CHUNK), F32),
                   jax.ShapeDtypeStruct((2, CHUNK, RET_QK_WIDTH), F32),
                   jax.ShapeDtypeStruct((2, RET_QK_WIDTH, CHUNK), F32),
                   jax.ShapeDtypeStruct((2, RET_HEADS, 1, LANES), F32)],
        grid=(2,),
        in_specs=[pl.BlockSpec((1, 1, LANES), lambda d: (d, 0, 0))],
        out_specs=[pl.BlockSpec((1, RET_HEADS, CHUNK, CHUNK), lambda d: (d, 0, 0, 0)),
                   pl.BlockSpec((1, CHUNK, RET_QK_WIDTH), lambda d: (d, 0, 0)),
                   pl.BlockSpec((1, RET_QK_WIDTH, CHUNK), lambda d: (d, 0, 0)),
                   pl.BlockSpec((1, RET_HEADS, 1, LANES), lambda d: (d, 0, 0, 0))],
        compiler_params=_params("arbitrary"),
        name="ret_consts",
    )(rates)


SCAN_BATCH = 4


def _mixer_scan_kernel(*refs, direction):
    (xs_ref, xst_ref, bc_ref, colp_ref, rowp_ref, q_ref, kbf_ref, kt_ref, v_ref,
     dec_ref, ecum_ref, wend_ref, etot_ref) = refs[:13]
    if direction == 0:
        yf_ref, yr_ref, ht_ref, hr_ref = refs[13:]
    else:
        (yf_ref, yr_ref, z_ref, g_ref, dwide_ref, normw_ref,
         ssd_out_ref, ret_out_ref, ht_ref, hr_ref) = refs[13:]
    d = direction
    c = pl.program_id(1)
    chains = range(SCAN_BATCH)

    @pl.when(c == 0)
    def _():
        ht_ref[...] = jnp.zeros_like(ht_ref)
        hr_ref[...] = jnp.zeros_like(hr_ref)

    mask, _ = _tri_masks(d)
    lane = lax.broadcasted_iota(jnp.int32, (CHUNK, LANES), 1)
    sub = lax.broadcasted_iota(jnp.int32, (CHUNK, LANES), 0)
    lane_lo = lane < SSD_HEAD_DIM
    sub_lo = sub < SSD_HEAD_DIM
    heads_per_group = SSD_HEADS // SSD_GROUPS

    ret = []
    for bb in chains:
        q = q_ref[bb]
        ret.append(dict(q_bf=q.astype(BF16), k_bf=kbf_ref[bb], qe_bf=(q * ecum_ref[0]).astype(BF16),
                        ktw_bf=(kt_ref[bb] * wend_ref[0]).astype(BF16), v_bf=v_ref[bb], lhs={}, ys=[]))

    def ret_scores(bb, hd):
        r = ret[bb]
        qs = slice(hd * RET_QK_DIM, (hd + 1) * RET_QK_DIM)
        sc = _dot_nt(r["q_bf"][:, qs], r["k_bf"][:, qs]) * dec_ref[0, hd]
        r["lhs"][hd] = jnp.concatenate([sc.astype(BF16), r["qe_bf"][:, qs]], axis=1)

    def ret_update(bb, hd):
        r = ret[bb]
        qs = slice(hd * RET_QK_DIM, (hd + 1) * RET_QK_DIM)
        v_h = r["v_bf"][:, hd * RET_V_DIM:(hd + 1) * RET_V_DIM]
        h_h = hr_ref[bb, hd]
        rhs = jnp.concatenate([v_h, h_h.astype(BF16)], axis=0)
        r["ys"].append(_dot(r["lhs"][hd], rhs))
        hr_ref[bb, hd] = etot_ref[0, hd] * h_h + _dot(r["ktw_bf"][qs, :], v_h)

    ssd = []
    for bb in chains:
        rowp = rowp_ref[bb, 0]
        ssd.append(dict(xs_bf=xs_ref[bb], bc=bc_ref[bb], cum_col=colp_ref[bb, 0],
                        cum_row=rowp[ROW_CUM:ROW_CUM + SSD_HEADS], dt_row=rowp[ROW_DT:ROW_DT + SSD_HEADS],
                        wend_row=rowp[ROW_WEND:ROW_WEND + SSD_HEADS],
                        etot_row=rowp[ROW_ETOT:ROW_ETOT + SSD_HEADS], gmats={}, ys=[]))

    def ssd_pair(bb, k):
        s = ssd[bb]
        g = (2 * k) // heads_per_group
        if g not in s["gmats"]:
            b_g = s["bc"][:, g * SSD_STATE:(g + 1) * SSD_STATE].astype(BF16)
            c_g = s["bc"][:, (SSD_GROUPS + g) * SSD_STATE:(SSD_GROUPS + g + 1) * SSD_STATE]
            s["gmats"][g] = (_dot_nt(c_g.astype(BF16), b_g), c_g, b_g)
        scores, c_g, b_g = s["gmats"][g]
        ha, hb = 2 * k, 2 * k + 1
        xs_pair = s["xs_bf"][:, k * LANES:(k + 1) * LANES]
        ht_pair = ht_ref[bb, k]
        rhs = jnp.concatenate([xs_pair, ht_pair.T.astype(BF16)], axis=0)
        y_heads = []
        for hd in (ha, hb):
            cb = jnp.broadcast_to(s["cum_col"][:, hd:hd + 1], (CHUNK, CHUNK))
            seg = cb - s["cum_row"][hd:hd + 1, :]
            dec = jnp.exp(jnp.where(mask, seg, NEG_BIG))
            m = scores * dec * s["dt_row"][hd:hd + 1, :]
            lhs = jnp.concatenate([m, c_g * jnp.exp(cb)], axis=1).astype(BF16)
            y_heads.append(_dot(lhs, rhs))
        s["ys"].append(jnp.where(lane_lo, y_heads[0], y_heads[1]))
        w_pair = jnp.where(sub_lo, s["wend_row"][ha:ha + 1, :], s["wend_row"][hb:hb + 1, :])
        e_pair = jnp.where(sub_lo, s["etot_row"][ha:ha + 1, :], s["etot_row"][hb:hb + 1, :])
        xw = (xst_ref[bb, k * LANES:(k + 1) * LANES, :].astype(F32) * w_pair).astype(BF16)
        ht_ref[bb, k] = e_pair * ht_pair + _dot(xw, b_g)

    for k in range(SSD_HEADS // 2):
        for bb in chains:
            ssd_pair(bb, k)
        for bb in chains:
            if k % 2 == 0:
                ret_scores(bb, k // 2)
            else:
                ret_update(bb, k // 2)
    y = [jnp.concatenate(ssd[bb]["ys"], axis=1) for bb in chains]
    yr = [jnp.concatenate(ret[bb]["ys"], axis=1) for bb in chains]

    for bb in chains:
        if direction == 0:
            yf_ref[bb] = y[bb].astype(yf_ref.dtype)
            yr_ref[bb] = yr[bb].astype(yr_ref.dtype)
            continue
        tot = yf_ref[bb].astype(F32) + y[bb] + ssd[bb]["xs_bf"].astype(F32) * dwide_ref[...]
        gated = tot * z_ref[bb]
        ms = jnp.mean(gated * gated, axis=-1, keepdims=True)
        ssd_out_ref[bb] = gated * lax.rsqrt(ms + RMS_EPS) * normw_ref[...]
        rtot = yr_ref[bb].astype(F32) + yr[bb]
        outs = []
        for hd in range(RET_HEADS):
            t_h = rtot[:, hd * RET_V_DIM:(hd + 1) * RET_V_DIM]
            mu = jnp.mean(t_h, axis=-1, keepdims=True)
            var = jnp.mean(jnp.square(t_h - mu), axis=-1, keepdims=True)
            outs.append((t_h - mu) * lax.rsqrt(var + LN_EPS))
        ret_out_ref[bb] = jnp.concatenate(outs, axis=1) * g_ref[bb]


def _mixer_scan(prep, z, rg, dwide, normw, rconsts, nc_ctx):
    xs_bf, xst_bf, bc, colp, rowp, rq, rk_bf, rkt, rv_bf = prep
    dec, ecum, wend, etot = rconsts
    bsz, s, _ = z.shape
    nc = s // CHUNK
    nb = SCAN_BATCH
    scratch = [pltpu.VMEM((nb, SSD_HEADS // 2, LANES, SSD_STATE), F32),
               pltpu.VMEM((nb, RET_HEADS, RET_QK_DIM, RET_V_DIM), F32)]

    def call(d, extra_in, extra_specs, out_shape, out_specs):
        cix = lambda c: _chunk_index(d, c, nc_ctx, nc)
        tokc = lambda wd: pl.BlockSpec((nb, CHUNK, wd), lambda b, c: (b, cix(c), 0))
        chan = lambda wd: pl.BlockSpec((nb, wd, CHUNK), lambda b, c: (b, 0, cix(c)))
        per_dir = lambda a: pl.BlockSpec((1,) + a.shape[1:], lambda b, c: (d,) + (0,) * (a.ndim - 1))
        common = [tokc(SSD_WIDTH), chan(SSD_WIDTH), tokc(bc.shape[-1]),
                  pl.BlockSpec((nb, 1, CHUNK, LANES), lambda b, c: (b, d, cix(c), 0)),
                  pl.BlockSpec((nb, 1, ROW_PACK, CHUNK), lambda b, c: (b, cix(c), d, 0)),
                  tokc(RET_QK_WIDTH), tokc(RET_QK_WIDTH), chan(RET_QK_WIDTH), tokc(RET_WIDTH),
                  per_dir(dec), per_dir(ecum), per_dir(wend), per_dir(etot)]
        return pl.pallas_call(
            functools.partial(_mixer_scan_kernel, direction=d),
            out_shape=out_shape,
            grid=(bsz // nb, nc),
            in_specs=common + extra_specs(tokc),
            out_specs=out_specs(tokc),
            scratch_shapes=scratch,
            compiler_params=_params("parallel", "arbitrary"),
            name="mixer_scan_fwd" if d == 0 else "mixer_scan_bwd",
        )(xs_bf, xst_bf, bc, colp, rowp, rq, rk_bf, rkt, rv_bf, dec, ecum, wend, etot, *extra_in)

    row1 = lambda wd: pl.BlockSpec((1, wd), lambda b, c: (0, 0))
    yf, yr = call(0, [], lambda tokc: [],
                  [jax.ShapeDtypeStruct((bsz, s, SSD_WIDTH), BF16), jax.ShapeDtypeStruct((bsz, s, RET_WIDTH), BF16)],
                  lambda tokc: [tokc(SSD_WIDTH), tokc(RET_WIDTH)])
    return call(1, [yf, yr, z, rg, dwide, normw],
                lambda tokc: [tokc(SSD_WIDTH), tokc(RET_WIDTH), tokc(SSD_WIDTH), tokc(RET_WIDTH),
                              row1(SSD_WIDTH), row1(SSD_WIDTH)],
                [jax.ShapeDtypeStruct((bsz, s, SSD_WIDTH), F32), jax.ShapeDtypeStruct((bsz, s, RET_WIDTH), F32)],
                lambda tokc: [tokc(SSD_WIDTH), tokc(RET_WIDTH)])


def _rms(v, g):
    return v * lax.rsqrt(jnp.mean(v * v, axis=-1, keepdims=True) + RMS_EPS) * g


def _mla_proj_kernel(cq_ref, ckv_ref, small_ref, cos_ref, sa_ref, sb_ref, qn_ref, kvn_ref,
                     wq_ref, wk_ref, wv_ref, q_ref, k_ref, v_ref):
    cqn = _rms(cq_ref[0], qn_ref[...]).astype(BF16)
    ckvn = _rms(ckv_ref[0], kvn_ref[...]).astype(BF16)
    q = _dot(cqn, wq_ref[...])
    kn = _dot(ckvn, wk_ref[...])
    vlane = lax.broadcasted_iota(jnp.int32, (1, MLA_PAIRS * MLA_V_SLOT), 1)
    ones = jnp.where(vlane % MLA_V_SLOT >= 2 * MLA_V_DIM, 1.0, 0.0)
    v_ref[0] = (_dot(ckvn, wv_ref[...]) + ones).astype(BF16)
    cos_t, sin_a, sin_b = cos_ref[...], sa_ref[...], sb_ref[...]
    lane = lax.broadcasted_iota(jnp.int32, small_ref.shape[1:], 1)
    in_rope = jnp.logical_and(lane >= KROPE_LANE, lane < KROPE_LANE + MLA_ROPE_DIM)
    kr = _rope3(jnp.where(in_rope, small_ref[0], 0.0), cos_t, sin_a, sin_b, MLA_ROPE_DIM // 2)
    for hd in range(MLA_HEADS):
        sl = slice(hd * MLA_HEAD_SLOT, (hd + 1) * MLA_HEAD_SLOT)
        q_ref[0, :, sl] = (_rope3(q[:, sl], cos_t, sin_a, sin_b, MLA_ROPE_DIM // 2) * MLA_Q_SCALE).astype(BF16)
        k_ref[0, :, sl] = (kn[:, sl] + kr).astype(BF16)


def _mla_proj(cq, ckv, small, cos_t, sin_a, sin_b, qn, kvn, wq, wk, wv):
    bsz, s, _ = cq.shape
    nt = s // TOKEN_TILE
    hw = MLA_HEADS * MLA_HEAD_SLOT
    vw = MLA_PAIRS * MLA_V_SLOT
    tok = lambda wd: pl.BlockSpec((1, TOKEN_TILE, wd), lambda b, t: (b, t, 0))
    tab = pl.BlockSpec((TOKEN_TILE, LANES), lambda b, t: (t, 0))
    full = lambda a: pl.BlockSpec(a.shape, lambda b, t: (0,) * a.ndim)
    return pl.pallas_call(
        _mla_proj_kernel,
        out_shape=[jax.ShapeDtypeStruct((bsz, s, hw), BF16),
                   jax.ShapeDtypeStruct((bsz, s, hw), BF16),
                   jax.ShapeDtypeStruct((bsz, s, vw), BF16)],
        grid=(bsz, nt),
        in_specs=[tok(MLA_Q_LORA), tok(MLA_KV_LORA), tok(LANES), tab, tab, tab,
                  full(qn), full(kvn), full(wq), full(wk), full(wv)],
        out_specs=[tok(hw), tok(hw), tok(vw)],
        compiler_params=_params("parallel", "arbitrary"),
        name="mla_proj",
    )(cq, ckv, small, cos_t, sin_a, sin_b, qn, kvn, wq, wk, wv)


ATTN_KEY_CHUNK = 256


def _attn_step(q_ref, k_ref, v_ref, g_ref, o_ref, slot_a, slot_b, n_keys):
    heads = range(2)
    hsl = [slice(hd * MLA_HEAD_SLOT, (hd + 1) * MLA_HEAD_SLOT) for hd in heads]
    if slot_a is not None:
        q = [q_ref[0, :, hsl[hd]] for hd in heads]
        mcol = [None, None]
    if slot_b is not None:
        mrow = [slot_b[1][hd] for hd in heads]
        pv = [None, None]
    for c in range(n_keys // ATTN_KEY_CHUNK):
        ks = slice(c * ATTN_KEY_CHUNK, (c + 1) * ATTN_KEY_CHUNK)
        for hd in heads:
            if slot_a is not None:
                s = _dot_nt(q[hd], k_ref[0, ks, hsl[hd]])
                slot_a[0][hd, :, ks] = s
                cm = s[:, :LANES]
                for w in range(1, ATTN_KEY_CHUNK // LANES):
                    cm = jnp.maximum(cm, s[:, w * LANES:(w + 1) * LANES])
                mcol[hd] = cm if c == 0 else jnp.maximum(mcol[hd], cm)
            if slot_b is not None:
                p = jnp.exp2(slot_b[0][hd, :, ks] - mrow[hd]).astype(BF16)
                part = _dot(p, v_ref[0, ks, :])
                pv[hd] = part if c == 0 else pv[hd] + part
    if slot_a is not None:
        for hd in heads:
            slot_a[1][hd] = jnp.max(mcol[hd], axis=-1, keepdims=True)
    if slot_b is not None:
        outs = [pv[hd][:, :LANES] / pv[hd][:, LANES:] for hd in heads]
        lane = lax.broadcasted_iota(jnp.int32, outs[0].shape, 1)
        o = jnp.where(lane < MLA_V_DIM, outs[0], outs[1])
        o_ref[0] = o * g_ref[0]


def _mla_attn_ctx_kernel(q_ref, k_ref, v_ref, g_ref, o_ref, s_ref, m_ref):
    n_keys = k_ref.shape[1]
    _attn_step(q_ref, k_ref, v_ref, g_ref, o_ref, (s_ref, m_ref), None, n_keys)
    _attn_step(q_ref, k_ref, v_ref, g_ref, o_ref, None, (s_ref, m_ref), n_keys)


def _mla_attn_ctx(q, k, v, gate, ctx_tiles):
    bsz, s, _ = q.shape
    tq = TOKEN_TILE
    n_ctx = ctx_tiles * tq
    pair_w = 2 * MLA_HEAD_SLOT
    return pl.pallas_call(
        _mla_attn_ctx_kernel,
        out_shape=jax.ShapeDtypeStruct((bsz, s, MLA_WIDTH), F32),
        grid=(bsz, MLA_PAIRS, ctx_tiles),
        in_specs=[pl.BlockSpec((1, tq, pair_w), lambda b, j, t: (b, t, j)),
                  pl.BlockSpec((1, n_ctx, pair_w), lambda b, j, t: (b, 0, j)),
                  pl.BlockSpec((1, n_ctx, MLA_V_SLOT), lambda b, j, t: (b, 0, j)),
                  pl.BlockSpec((1, tq, LANES), lambda b, j, t: (b, t, j))],
        out_specs=pl.BlockSpec((1, tq, LANES), lambda b, j, t: (b, t, j)),
        scratch_shapes=[pltpu.VMEM((2, tq, n_ctx), F32), pltpu.VMEM((2, tq, 1), F32)],
        compiler_params=_params("parallel", "arbitrary", "arbitrary"),
        name="mla_attn_ctx",
    )(q, k, v, gate)


def _mla_attn_lat_kernel(*refs, n_tiles, aliased):
    if aliased:
        refs = refs[1:]
    q_ref, k_ref, v_ref, g_ref, o_ref, s0, m0, s1, m1 = refs
    n_keys = k_ref.shape[1]
    slots = ((s0, m0), (s1, m1))
    n = pl.program_id(0)

    @pl.when(n == 0)
    def _():
        _attn_step(q_ref, k_ref, v_ref, g_ref, o_ref, slots[0], None, n_keys)

    for par in range(2):
        @pl.when(jnp.logical_and(jnp.logical_and(n > 0, n < n_tiles), n % 2 == par))
        def _():
            _attn_step(q_ref, k_ref, v_ref, g_ref, o_ref, slots[par], slots[1 - par], n_keys)

    @pl.when(n == n_tiles)
    def _():
        _attn_step(q_ref, k_ref, v_ref, g_ref, o_ref, None, slots[(n_tiles - 1) % 2], n_keys)


def _mla_attn_lat(q, k, v, gate, ctx_tiles, ctx_out=None):
    bsz, s, _ = q.shape
    tq = TOKEN_TILE
    nq = s // tq - ctx_tiles
    n_tiles = bsz * MLA_PAIRS * nq
    pair_w = 2 * MLA_HEAD_SLOT
    out_lo = 0 if ctx_out is not None else ctx_tiles

    def decode(i):
        return i // (MLA_PAIRS * nq), (i // nq) % MLA_PAIRS, i % nq + ctx_tiles

    def ahead(n):
        return decode(jnp.minimum(n, n_tiles - 1))

    def behind(n):
        return decode(jnp.maximum(n - 1, 0))

    def q_map(n):
        b, j, t = ahead(n)
        return b, t, j

    def k_map(n):
        b, j, _ = ahead(n)
        return b, 0, j

    def v_map(n):
        b, j, _ = behind(n)
        return b, 0, j

    def g_map(n):
        b, j, t = behind(n)
        return b, t, j

    def o_map(n):
        b, j, t = behind(n)
        return b, t - out_lo, j

    in_specs = [pl.BlockSpec((1, tq, pair_w), q_map),
                pl.BlockSpec((1, s, pair_w), k_map),
                pl.BlockSpec((1, s, MLA_V_SLOT), v_map),
                pl.BlockSpec((1, tq, LANES), g_map)]
    args = [q, k, v, gate]
    aliases = {}
    if ctx_out is not None:
        in_specs = [pl.BlockSpec(memory_space=pl.ANY)] + in_specs
        args = [ctx_out] + args
        aliases = {0: 0}
    kern = functools.partial(_mla_attn_lat_kernel, n_tiles=n_tiles, aliased=ctx_out is not None)
    return pl.pallas_call(
        kern,
        out_shape=jax.ShapeDtypeStruct((bsz, s - out_lo * tq, MLA_WIDTH), F32),
        grid=(n_tiles + 1,),
        in_specs=in_specs,
        out_specs=pl.BlockSpec((1, tq, LANES), o_map),
        scratch_shapes=[pltpu.VMEM((2, tq, s), F32), pltpu.VMEM((2, tq, 1), F32),
                        pltpu.VMEM((2, tq, s), F32), pltpu.VMEM((2, tq, 1), F32)],
        input_output_aliases=aliases,
        compiler_params=_params("arbitrary"),
        name="mla_attn",
    )(*args)


def _outproj_kernel(*refs, n_stream, ctx_tiles, t_lo, alpha):
    ssd_ref, mla_ref, ret_ref = refs[:3]
    x_refs = refs[3:3 + n_stream]
    mod_ref, w_ref, g_ref, b_ref, o_ref = refs[3 + n_stream:]
    acc = _dot(ssd_ref[0].astype(BF16), w_ref[:SSD_WIDTH, :])
    acc = acc + _dot(mla_ref[0].astype(BF16), w_ref[SSD_WIDTH:SSD_WIDTH + MLA_WIDTH, :])
    acc = acc + _dot(ret_ref[0].astype(BF16), w_ref[SSD_WIDTH + MLA_WIDTH:, :])
    gate = mod_ref[0, 0][:, 2 * D_MODEL:]
    x = _stream_tile(x_refs, pl.program_id(1) + t_lo, ctx_tiles)
    r = alpha * x + gate * acc
    mu = jnp.mean(r, axis=-1, keepdims=True)
    var = jnp.mean(jnp.square(r - mu), axis=-1, keepdims=True)
    o_ref[0] = (r - mu) * lax.rsqrt(var + LN_EPS) * g_ref[...] + b_ref[...]


def _outproj(ssd_o, mla_o, mla_lo, ret_o, stream, mod_all, w, ln_g, ln_b, t_lo, ctx_tiles, alpha):
    bsz, s, _ = ssd_o.shape
    d = stream[0].shape[-1]
    nt = s // TOKEN_TILE
    tok = lambda wd: pl.BlockSpec((1, TOKEN_TILE, wd), lambda b, t: (b, t + t_lo, 0))
    kern = functools.partial(_outproj_kernel, n_stream=len(stream), ctx_tiles=ctx_tiles, t_lo=t_lo, alpha=alpha)
    return pl.pallas_call(
        kern,
        out_shape=jax.ShapeDtypeStruct((bsz, s - t_lo * TOKEN_TILE, d), F32),
        grid=(bsz, nt - t_lo),
        in_specs=[tok(SSD_WIDTH),
                  pl.BlockSpec((1, TOKEN_TILE, MLA_WIDTH), lambda b, t: (b, t + t_lo - mla_lo, 0)),
                  tok(RET_WIDTH)] + _stream_specs(stream, t_lo, ctx_tiles) + [
                  pl.BlockSpec((1, 1, 1, 3 * d),
                               lambda b, t: (b, jnp.minimum(t + t_lo, ctx_tiles) // ctx_tiles, 0, 0)),
                  pl.BlockSpec(w.shape, lambda b, t: (0, 0)),
                  pl.BlockSpec((1, d), lambda b, t: (0, 0)),
                  pl.BlockSpec((1, d), lambda b, t: (0, 0))],
        out_specs=pl.BlockSpec((1, TOKEN_TILE, d), lambda b, t: (b, t, 0)),
        compiler_params=_params("parallel", "arbitrary"),
        name="out_proj",
    )(ssd_o, mla_o, ret_o, *stream, mod_all, w, ln_g.reshape(1, d), ln_b.reshape(1, d))


def _rope_tables(n_ctx, n_lat, rot_dim, width, lane0, period):
    t = np.arange(n_lat)
    row = (t // GRID_W).astype(np.float32)
    col = (t % GRID_W).astype(np.float32)
    n_freq = rot_dim // 4
    inv = (ROPE_THETA ** (-np.arange(n_freq, dtype=np.float32) / n_freq)).astype(np.float32)
    ang = np.concatenate([row[:, None] * inv, col[:, None] * inv], -1).astype(np.float32)
    cos, sin = np.cos(ang), np.sin(ang)
    half = rot_dim // 2
    lane = np.arange(width) % period - lane0
    in_rot = (lane >= 0) & (lane < rot_dim)
    idx = np.where(in_rot, lane % half, 0)
    first = in_rot & (lane < half)
    second = in_rot & (lane >= half)
    cos_t = np.where(in_rot[None, :], cos[:, idx], 1.0)
    sin_a = np.where(first[None, :], -sin[:, idx], 0.0)
    sin_b = np.where(second[None, :], sin[:, idx], 0.0)
    ctx_pad = lambda a, v: np.concatenate([np.full((n_ctx, width), v), a], axis=0).astype(np.float32)
    return ctx_pad(cos_t, 1.0), ctx_pad(sin_a, 0.0), ctx_pad(sin_b, 0.0)


def _mla_weights(w_uq, w_ukv):
    qd = MLA_NOPE_DIM + MLA_ROPE_DIM
    wq = w_uq.reshape(MLA_Q_LORA, MLA_HEADS, qd)
    wq = jnp.pad(wq, ((0, 0), (0, 0), (0, MLA_HEAD_SLOT - qd))).reshape(MLA_Q_LORA, MLA_HEADS * MLA_HEAD_SLOT)
    wkv = w_ukv.reshape(MLA_KV_LORA, MLA_HEADS, MLA_NOPE_DIM + MLA_V_DIM)
    wk = jnp.pad(wkv[..., :MLA_NOPE_DIM], ((0, 0), (0, 0), (0, MLA_HEAD_SLOT - MLA_NOPE_DIM)))
    wk = wk.reshape(MLA_KV_LORA, MLA_HEADS * MLA_HEAD_SLOT)
    wv = wkv[..., MLA_NOPE_DIM:].reshape(MLA_KV_LORA, MLA_PAIRS, 2 * MLA_V_DIM)
    wv = jnp.pad(wv, ((0, 0), (0, 0), (0, MLA_V_SLOT - 2 * MLA_V_DIM))).reshape(MLA_KV_LORA, MLA_PAIRS * MLA_V_SLOT)
    return wq.astype(BF16), wk.astype(BF16), wv.astype(BF16)


def _lane_pad(v):
    return jnp.pad(v, (0, LANES - v.shape[0])).reshape(1, LANES)


def kernel(x, c, ctx, c_ctx, w_ada, b_ada, w_in, ssd_conv_w, ssd_conv_b, ssd_a_log_f, ssd_a_log_b,
           ssd_dt_bias_f, ssd_dt_bias_b, ssd_d, ssd_norm_w, mla_q_norm, mla_w_uq, mla_kv_norm, mla_w_ukv,
           ret_log_rate_f, ret_log_rate_b, w_out, ln_g, ln_b):
    bsz, n_lat, d = x.shape
    n_ctx = ctx.shape[1]
    depth = w_in.shape[0]
    assert d == D_MODEL and n_ctx == TOKEN_TILE and n_lat % TOKEN_TILE == 0
    alpha = (2 * depth) ** 0.25
    s = n_ctx + n_lat
    nc_ctx = n_ctx // CHUNK
    ctx_tiles = n_ctx // TOKEN_TILE
    nt = s // TOKEN_TILE

    cos_m, sa_m, sb_m = _rope_tables(n_ctx, n_lat, MLA_ROPE_DIM, LANES, MLA_ROPE_LANE, LANES)
    cos_r, sa_r, sb_r = _rope_tables(n_ctx, n_lat, RET_QK_DIM, RET_QK_WIDTH, 0, RET_QK_DIM)

    rows = 16
    cc = jnp.zeros((rows, d), F32).at[:bsz].set(c).at[bsz].set(c_ctx)
    stream = (ctx, x)

    for i in range(depth):
        last = i == depth - 1
        mod = _ada(cc, w_ada, b_ada, i)
        mod_all = jnp.stack([jnp.broadcast_to(mod[bsz], (bsz, 3 * d)), mod[:bsz]], axis=1)
        mod_all = mod_all.reshape(bsz, 2, 1, 3 * d)
        both = lambda f, b: jnp.stack([f, b], axis=0)
        head_rows = lambda v: jnp.broadcast_to(v[:, None], (SSD_HEADS, TOKEN_TILE))
        alogt = both(head_rows(ssd_a_log_f[i]), head_rows(ssd_a_log_b[i]))
        dtbt = both(head_rows(ssd_dt_bias_f[i]), head_rows(ssd_dt_bias_b[i]))
        (zs, xs_bf, xst_bf, bc, colp, rowp, small, cq, ckv, mgs, rq, rk_bf, rkt, rv_bf, rgs) = _inproj(
            stream, mod_all, _relayout_w_in(w_in, i), ssd_conv_w[i], ssd_conv_b[i].reshape(1, -1), alogt, dtbt,
            cos_r, sa_r, sb_r, ctx_tiles)

        dwide = jnp.repeat(ssd_d[i], SSD_HEAD_DIM).reshape(1, SSD_WIDTH)
        rates = both(_lane_pad(ret_log_rate_f[i]), _lane_pad(ret_log_rate_b[i]))
        prep = (xs_bf, xst_bf, bc, colp, rowp, rq, rk_bf, rkt, rv_bf)
        ssd_o, ret_o = _mixer_scan(prep, zs, rgs, dwide, ssd_norm_w[i].reshape(1, -1), _ret_consts(rates), nc_ctx)

        wq, wk, wv = _mla_weights(mla_w_uq[i], mla_w_ukv[i])
        q, k, v = _mla_proj(cq, ckv, small, cos_m, sa_m, sb_m, mla_q_norm[i].reshape(1, -1),
                            mla_kv_norm[i].reshape(1, -1), wq, wk, wv)
        mla_lo = ctx_tiles if last else 0
        mla_o = _mla_attn_lat(q, k, v, mgs, ctx_tiles, None if last else _mla_attn_ctx(q, k, v, mgs, ctx_tiles))

        out = _outproj(ssd_o, mla_o, mla_lo, ret_o, stream, mod_all, w_out[i].astype(BF16), ln_g[i], ln_b[i],
                       ctx_tiles if last else 0, ctx_tiles, alpha)
        stream = (out,)
    return stream[0]
```

```python
import functools
import math

import jax
import jax.numpy as jnp
import numpy as np
from jax import lax
from jax.experimental import pallas as pl
from jax.experimental.pallas import tpu as pltpu

F32 = jnp.float32
BF16 = jnp.bfloat16

LANES = 128
SUBLANES = 8
VMEM_LIMIT_BYTES = 56 * 1024 * 1024

GRID_W = 64
D_MODEL = 1024
SSD_WIDTH = 1024
SSD_HEAD_DIM = 64
SSD_HEADS = 16
SSD_GROUPS = 2
SSD_STATE = 128
SSD_CONV = 5
SSD_CONV_DIM = SSD_WIDTH + 2 * SSD_GROUPS * SSD_STATE
MLA_WIDTH = 512
MLA_V_DIM = 64
MLA_HEADS = 8
MLA_NOPE_DIM = 64
MLA_ROPE_DIM = 32
MLA_Q_LORA = 384
MLA_KV_LORA = 256
MLA_SCALE = (MLA_NOPE_DIM + MLA_ROPE_DIM) ** -0.5
RET_WIDTH = 512
RET_HEADS = 4
RET_V_DIM = 128
RET_QK_DIM = 64
RET_QK_WIDTH = RET_HEADS * RET_QK_DIM
CHUNK = 128
ROPE_THETA = 10000.0
LN_EPS = 1e-5
RMS_EPS = 1e-6
IN_WIDTHS = (SSD_WIDTH, SSD_CONV_DIM, SSD_HEADS, MLA_Q_LORA, MLA_KV_LORA, MLA_ROPE_DIM, MLA_WIDTH,
             RET_QK_WIDTH, RET_QK_WIDTH, RET_WIDTH, RET_WIDTH)

TOKEN_TILE = 256
MLA_HEAD_SLOT = 128
MLA_ROPE_LANE = MLA_NOPE_DIM
MLA_PAIRS = MLA_HEADS // 2
MLA_V_SLOT = 256
MLA_Q_SCALE = MLA_SCALE * math.log2(math.e)
NEG_BIG = -1e30


def _params(*sem):
    return pltpu.CompilerParams(dimension_semantics=sem, vmem_limit_bytes=VMEM_LIMIT_BYTES)


def _silu(v):
    return v * jax.nn.sigmoid(v)


def _softplus(v):
    return jnp.maximum(v, 0.0) + jnp.log1p(jnp.exp(-jnp.abs(v)))


def _dot(a, b):
    return jnp.dot(a, b, preferred_element_type=F32)


def _dot_exact(a, b):
    return jnp.dot(a, b, preferred_element_type=F32, precision=lax.Precision.HIGHEST)


def _dot_nt(a, b):
    return lax.dot_general(a, b, (((1,), (1,)), ((), ())), preferred_element_type=F32)


def _ada_kernel(c_ref, w_ref, b_ref, o_ref):
    o_ref[...] = _dot_exact(_silu(c_ref[...]), w_ref[0]) + b_ref[0]


def _ada(cc, w_all, b_all, layer):
    rows, d = cc.shape
    depth, _, n = w_all.shape
    tn = 1024
    return pl.pallas_call(
        _ada_kernel,
        out_shape=jax.ShapeDtypeStruct((rows, n), F32),
        grid=(n // tn,),
        in_specs=[pl.BlockSpec((rows, d), lambda j: (0, 0)),
                  pl.BlockSpec((1, d, tn), lambda j: (layer, 0, j)),
                  pl.BlockSpec((1, 1, tn), lambda j: (layer, 0, j))],
        out_specs=pl.BlockSpec((rows, tn), lambda j: (0, j)),
        compiler_params=_params("arbitrary"),
        name="ada_mod",
    )(cc, w_all, b_all.reshape(depth, 1, n))


def _chunk_index(d, c, nc_ctx, nc):
    bwd = jnp.where(c < nc_ctx, nc_ctx - 1 - c, nc + nc_ctx - 1 - c)
    return jnp.where(d == 0, c, bwd)


def _tri_masks(d):
    ii = lax.broadcasted_iota(jnp.int32, (CHUNK, CHUNK), 0)
    jj = lax.broadcasted_iota(jnp.int32, (CHUNK, CHUNK), 1)
    sgn = 1 - 2 * d
    mask = (ii - jj) * sgn >= 0
    mask_t = (jj - ii) * sgn >= 0
    return mask, mask_t


def _rope3(v, cos_t, sin_a, sin_b, shift):
    width = v.shape[-1]
    return v * cos_t + pltpu.roll(v, width - shift, 1) * sin_a + pltpu.roll(v, shift, 1) * sin_b


PROJ_WIDTHS = (SSD_WIDTH, SSD_CONV_DIM, LANES, MLA_Q_LORA, MLA_KV_LORA, MLA_WIDTH,
               RET_QK_WIDTH, RET_QK_WIDTH, RET_WIDTH, RET_WIDTH)
PROJ_OFFSETS = tuple(int(o) for o in np.cumsum((0,) + PROJ_WIDTHS[:-1]))
DT_LANE = 0
KROPE_LANE = MLA_ROPE_LANE
ROW_CUM, ROW_DT, ROW_WEND, ROW_ETOT = 0, 16, 32, 48
ROW_PACK = 64
TILE_CHUNKS = TOKEN_TILE // CHUNK


IN_OFFSETS = tuple(int(o) for o in np.cumsum((0,) + IN_WIDTHS[:-1]))
RELAYOUT_MOVES = ((0, 0, 0), (1, 1, 0), (2, 2, DT_LANE), (3, 3, 0), (4, 4, 0), (5, 2, KROPE_LANE), (6, 5, 0),
                  (7, 6, 0), (8, 7, 0), (9, 8, 0), (10, 9, 0))
RELAYOUT_ROWS = 128


def _relayout_w_in_kernel(w_ref, o_ref):
    small = PROJ_OFFSETS[2]
    o_ref[:, small:small + LANES] = jnp.zeros((o_ref.shape[0], LANES), BF16)
    for src, dst, lane in RELAYOUT_MOVES:
        lo = PROJ_OFFSETS[dst] + lane
        o_ref[:, lo:lo + IN_WIDTHS[src]] = w_ref[0, :, IN_OFFSETS[src]:IN_OFFSETS[src] + IN_WIDTHS[src]].astype(BF16)


def _relayout_w_in(w_in_all, layer):
    _, d, n_in = w_in_all.shape
    n_out = sum(PROJ_WIDTHS)
    return pl.pallas_call(
        _relayout_w_in_kernel,
        out_shape=jax.ShapeDtypeStruct((d, n_out), BF16),
        grid=(d // RELAYOUT_ROWS,),
        in_specs=[pl.BlockSpec((1, RELAYOUT_ROWS, n_in), lambda r: (layer, r, 0))],
        out_specs=pl.BlockSpec((RELAYOUT_ROWS, n_out), lambda r: (r, 0)),
        compiler_params=_params("arbitrary"),
        name="w_in_relayout",
    )(w_in_all)


def _stream_specs(stream, t_lo, ctx_tiles):
    d = stream[0].shape[-1]
    if len(stream) == 1:
        return [pl.BlockSpec((1, TOKEN_TILE, d), lambda b, t: (b, t + t_lo, 0))]
    return [pl.BlockSpec((1, TOKEN_TILE, d), lambda b, t: (b, jnp.minimum(t + t_lo, ctx_tiles - 1), 0)),
            pl.BlockSpec((1, TOKEN_TILE, d), lambda b, t: (b, jnp.maximum(t + t_lo - ctx_tiles, 0), 0))]


def _halo_specs(stream, ctx_tiles):
    arr = stream[-1]
    d = arr.shape[-1]
    shift = 0 if len(stream) == 1 else ctx_tiles
    per_tile = TOKEN_TILE // SUBLANES
    nblk = arr.shape[1] // SUBLANES
    prev = lambda b, t: (b, jnp.clip((t - shift) * per_tile - 1, 0, nblk - 1), 0)
    nxt = lambda b, t: (b, jnp.clip((t - shift + 1) * per_tile, 0, nblk - 1), 0)
    return [pl.BlockSpec((1, SUBLANES, d), prev), pl.BlockSpec((1, SUBLANES, d), nxt)]


def _stream_tile(refs, t, ctx_tiles):
    if len(refs) == 1:
        return refs[0][0]
    return jnp.where(t < ctx_tiles, refs[0][0], refs[1][0])


def _inproj_kernel(*refs, n_stream, ctx_tiles, nt):
    x_refs = refs[:n_stream]
    (prev_ref, next_ref, mod_ref, w_ref, convw_ref, convb_ref, alogt_ref, dtbt_ref,
     cos_ref, sa_ref, sb_ref) = refs[n_stream:n_stream + 11]
    (zs_ref, xs_ref, xst_ref, bc_ref, colp_ref, rowp_ref, small_ref, cq_ref, ckv_ref, mgs_ref,
     q_ref, kbf_ref, kt_ref, v_ref, rgs_ref) = refs[n_stream + 11:]
    t = pl.program_id(1)
    mod = mod_ref[0, 0]
    sh = mod[:, :D_MODEL]
    sc = mod[:, D_MODEL:2 * D_MODEL]
    modulate = lambda v: (v * (1.0 + sc) + sh).astype(BF16)
    h = modulate(_stream_tile(x_refs, t, ctx_tiles))
    h_ext = jnp.concatenate([modulate(prev_ref[0]), h, modulate(next_ref[0])], axis=0)

    def proj(g, lo=0, width=None, lhs=None):
        off = PROJ_OFFSETS[g] + lo
        width = PROJ_WIDTHS[g] - lo if width is None else width
        return _dot(h if lhs is None else lhs, w_ref[:, off:off + width])

    first = jnp.logical_or(t == 0, t == ctx_tiles)
    last = jnp.logical_or(t == ctx_tiles - 1, t == nt - 1)
    row = lax.broadcasted_iota(jnp.int32, (TOKEN_TILE + 2 * SUBLANES, 1), 0)
    keep = jnp.where(jnp.logical_or(jnp.logical_and(row < SUBLANES, first),
                                    jnp.logical_and(row >= SUBLANES + TOKEN_TILE, last)), 0.0, 1.0)
    unit_w = 2 * LANES
    ext_parts = {}

    def ext_unit(e):
        ext_parts[e] = proj(1, e * unit_w, unit_w, lhs=h_ext) * keep

    def conv_strip(j, anchor):
        pad = SSD_CONV // 2
        sl = slice(j * LANES, (j + 1) * LANES)
        esl = slice((j % 2) * LANES, (j % 2 + 1) * LANES)
        ext = ext_parts[j // 2]
        acc = convb_ref[:, sl] + anchor
        for k in range(SSD_CONV):
            lo = SUBLANES - pad + k
            acc = acc + convw_ref[k:k + 1, sl] * ext[lo:lo + TOKEN_TILE, esl]
        u = _silu(acc)
        if j < SSD_WIDTH // LANES:
            xs_ref[0, :, sl] = u.astype(BF16)
            xst_ref[0, sl, :] = u.T.astype(BF16)
        else:
            bc_ref[0, :, j * LANES - SSD_WIDTH:(j + 1) * LANES - SSD_WIDTH] = u

    def gate_unit(g, ref, lo, width):
        def run():
            val = proj(g, lo, width)
            ref[0, :, lo:lo + width] = _silu(val)
            return val
        return run

    def plain_unit(g, ref, cast=None):
        def run():
            val = proj(g)
            ref[0] = val if cast is None else val.astype(cast)
            return val
        return run

    def ret_q_unit():
        val = proj(6)
        q_ref[0] = _rope3(val, cos_ref[...], sa_ref[...], sb_ref[...], RET_QK_DIM // 2)
        return val

    def ret_k_unit():
        val = proj(7)
        kk = _rope3(val, cos_ref[...], sa_ref[...], sb_ref[...], RET_QK_DIM // 2) * (RET_QK_DIM ** -0.5)
        kbf_ref[0] = kk.astype(BF16)
        kt_ref[0] = kk.T
        return val

    units = [gate_unit(0, zs_ref, lo, unit_w) for lo in range(0, SSD_WIDTH, unit_w)]
    units += [plain_unit(3, cq_ref), plain_unit(4, ckv_ref)]
    units += [gate_unit(5, mgs_ref, lo, unit_w) for lo in range(0, MLA_WIDTH, unit_w)]
    units += [ret_q_unit, ret_k_unit, plain_unit(8, v_ref, BF16)]
    units += [gate_unit(9, rgs_ref, lo, unit_w) for lo in range(0, RET_WIDTH, unit_w)]
    small = proj(2)
    small_ref[0] = small
    n_strips = SSD_CONV_DIM // LANES
    ext_unit(0)
    anchor = jnp.zeros((1, LANES), F32)
    for j in range(n_strips):
        if j % 2 == 0 and j // 2 + 1 < n_strips // 2:
            ext_unit(j // 2 + 1)
        nxt = units.pop(0)()[0:1, :LANES] * 0.0 if units else anchor
        conv_strip(j, anchor)
        anchor = nxt
    for unit in units:
        unit()

    dt_raw_t = small.T[DT_LANE:DT_LANE + SSD_HEADS, :]
    pad_rows = jnp.zeros((CHUNK - SSD_HEADS, CHUNK), F32)
    for d in range(2):
        _, mask_t = _tri_masks(d)
        tri_t = jnp.where(mask_t, 1.0, 0.0)
        dt_all = _softplus(dt_raw_t + dtbt_ref[d])
        la_all = dt_all * -jnp.exp(alogt_ref[d])
        base = d * ROW_PACK
        for j in range(TILE_CHUNKS):
            cs = slice(j * CHUNK, (j + 1) * CHUNK)
            dt_row, la_row = dt_all[:, cs], la_all[:, cs]
            cum_row = _dot_exact(la_row, tri_t)
            colp_ref[0, d, cs, :] = jnp.concatenate([cum_row, pad_rows], axis=0).T
            tot_row = jnp.sum(la_row, axis=1, keepdims=True)
            rowp_ref[0, j, base + ROW_CUM:base + ROW_CUM + SSD_HEADS, :] = cum_row
            rowp_ref[0, j, base + ROW_DT:base + ROW_DT + SSD_HEADS, :] = dt_row
            rowp_ref[0, j, base + ROW_WEND:base + ROW_WEND + SSD_HEADS, :] = dt_row * jnp.exp(tot_row - cum_row)
            rowp_ref[0, j, base + ROW_ETOT:base + ROW_ETOT + SSD_HEADS, :] = jnp.broadcast_to(
                jnp.exp(tot_row), (SSD_HEADS, CHUNK))


def _inproj(stream, mod_all, w, conv_w, conv_b, alogt, dtbt, cos_t, sin_a, sin_b, ctx_tiles):
    bsz, d = stream[0].shape[0], stream[0].shape[-1]
    s = sum(a.shape[1] for a in stream)
    nt = s // TOKEN_TILE
    nc = s // CHUNK
    tok = lambda wd: pl.BlockSpec((1, TOKEN_TILE, wd), lambda b, t: (b, t, 0))
    chan = lambda wd: pl.BlockSpec((1, wd, TOKEN_TILE), lambda b, t: (b, 0, t))
    full = lambda a: pl.BlockSpec(a.shape, lambda b, t: (0,) * a.ndim)
    tab = pl.BlockSpec((TOKEN_TILE, RET_QK_WIDTH), lambda b, t: (t, 0))
    bcw = SSD_CONV_DIM - SSD_WIDTH
    outs = [((bsz, s, SSD_WIDTH), F32, tok(SSD_WIDTH)),
            ((bsz, s, SSD_WIDTH), BF16, tok(SSD_WIDTH)),
            ((bsz, SSD_WIDTH, s), BF16, chan(SSD_WIDTH)),
            ((bsz, s, bcw), F32, tok(bcw)),
            ((bsz, 2, s, LANES), F32, pl.BlockSpec((1, 2, TOKEN_TILE, LANES), lambda b, t: (b, 0, t, 0))),
            ((bsz, nc, 2 * ROW_PACK, CHUNK), F32,
             pl.BlockSpec((1, TILE_CHUNKS, 2 * ROW_PACK, CHUNK), lambda b, t: (b, t, 0, 0))),
            ((bsz, s, LANES), F32, tok(LANES)),
            ((bsz, s, MLA_Q_LORA), F32, tok(MLA_Q_LORA)),
            ((bsz, s, MLA_KV_LORA), F32, tok(MLA_KV_LORA)),
            ((bsz, s, MLA_WIDTH), F32, tok(MLA_WIDTH)),
            ((bsz, s, RET_QK_WIDTH), F32, tok(RET_QK_WIDTH)),
            ((bsz, s, RET_QK_WIDTH), BF16, tok(RET_QK_WIDTH)),
            ((bsz, RET_QK_WIDTH, s), F32, chan(RET_QK_WIDTH)),
            ((bsz, s, RET_WIDTH), BF16, tok(RET_WIDTH)),
            ((bsz, s, RET_WIDTH), F32, tok(RET_WIDTH))]
    kern = functools.partial(_inproj_kernel, n_stream=len(stream), ctx_tiles=ctx_tiles, nt=nt)
    return pl.pallas_call(
        kern,
        out_shape=[jax.ShapeDtypeStruct(shape, dt) for shape, dt, _ in outs],
        grid=(bsz, nt),
        in_specs=_stream_specs(stream, 0, ctx_tiles) + _halo_specs(stream, ctx_tiles) + [
            pl.BlockSpec((1, 1, 1, 3 * d), lambda b, t: (b, jnp.minimum(t, ctx_tiles) // ctx_tiles, 0, 0)),
            pl.BlockSpec(w.shape, lambda b, t: (0, 0), pipeline_mode=pl.Buffered(1)),
            full(conv_w), full(conv_b), full(alogt), full(dtbt), tab, tab, tab],
        out_specs=[spec for _, _, spec in outs],
        compiler_params=_params("parallel", "arbitrary"),
        name="in_proj",
    )(*stream, stream[-1], stream[-1], mod_all, w, conv_w, conv_b, alogt, dtbt, cos_t, sin_a, sin_b)


def _ret_consts_kernel(rate_ref, dec_ref, ecum_ref, wend_ref, etot_ref):
    d = pl.program_id(0)
    mask, _ = _tri_masks(d)
    ii = lax.broadcasted_iota(jnp.int32, (CHUNK, CHUNK), 0)
    jj = lax.broadcasted_iota(jnp.int32, (CHUNK, CHUNK), 1)
    dist = jnp.abs(ii - jj).astype(F32)
    pos_i = lax.broadcasted_iota(jnp.int32, (CHUNK, RET_QK_DIM), 0)
    steps_col = jnp.where(d == 0, pos_i + 1, CHUNK - pos_i).astype(F32)
    pos_j = lax.broadcasted_iota(jnp.int32, (RET_QK_DIM, CHUNK), 1)
    left_row = jnp.where(d == 0, CHUNK - 1 - pos_j, pos_j).astype(F32)
    for hd in range(RET_HEADS):
        la = -jnp.exp(rate_ref[0, :, hd:hd + 1])
        dec_ref[0, hd] = jnp.exp(jnp.where(mask, dist * la, NEG_BIG))
        ecum_ref[0, :, hd * RET_QK_DIM:(hd + 1) * RET_QK_DIM] = jnp.exp(steps_col * la)
        wend_ref[0, hd * RET_QK_DIM:(hd + 1) * RET_QK_DIM, :] = jnp.exp(left_row * la)
        etot_ref[0, hd] = jnp.broadcast_to(jnp.exp(CHUNK * la), (1, LANES))


def _ret_consts(rates):
    return pl.pallas_call(
        _ret_consts_kernel,
        out_shape=[jax.ShapeDtypeStruct((2, RET_HEADS, CHUNK, CHUNK), F32),
                   jax.ShapeDtypeStruct((2, CHUNK, RET_QK_WIDTH), F32),
                   jax.ShapeDtypeStruct((2, RET_QK_WIDTH, CHUNK), F32),
                   jax.ShapeDtypeStruct((2, RET_HEADS, 1, LANES), F32)],
        grid=(2,),
        in_specs=[pl.BlockSpec((1, 1, LANES), lambda d: (d, 0, 0))],
        out_specs=[pl.BlockSpec((1, RET_HEADS, CHUNK, CHUNK), lambda d: (d, 0, 0, 0)),
                   pl.BlockSpec((1, CHUNK, RET_QK_WIDTH), lambda d: (d, 0, 0)),
                   pl.BlockSpec((1, RET_QK_WIDTH, CHUNK), lambda d: (d, 0, 0)),
                   pl.BlockSpec((1, RET_HEADS, 1, LANES), lambda d: (d, 0, 0, 0))],
        compiler_params=_params("arbitrary"),
        name="ret_consts",
    )(rates)


SCAN_BATCH = 4


def _mixer_scan_kernel(*refs, direction):
    (xs_ref, xst_ref, bc_ref, colp_ref, rowp_ref, q_ref, kbf_ref, kt_ref, v_ref,
     dec_ref, ecum_ref, wend_ref, etot_ref) = refs[:13]
    if direction == 0:
        yf_ref, yr_ref, ht_ref, hr_ref = refs[13:]
    else:
        (yf_ref, yr_ref, z_ref, g_ref, dwide_ref, normw_ref,
         ssd_out_ref, ret_out_ref, ht_ref, hr_ref) = refs[13:]
    d = direction
    c = pl.program_id(1)
    chains = range(SCAN_BATCH)

    @pl.when(c == 0)
    def _():
        ht_ref[...] = jnp.zeros_like(ht_ref)
        hr_ref[...] = jnp.zeros_like(hr_ref)

    mask, _ = _tri_masks(d)
    lane = lax.broadcasted_iota(jnp.int32, (CHUNK, LANES), 1)
    sub = lax.broadcasted_iota(jnp.int32, (CHUNK, LANES), 0)
    lane_lo = lane < SSD_HEAD_DIM
    sub_lo = sub < SSD_HEAD_DIM
    heads_per_group = SSD_HEADS // SSD_GROUPS

    ret = []
    for bb in chains:
        q = q_ref[bb]
        ret.append(dict(q_bf=q.astype(BF16), k_bf=kbf_ref[bb], qe_bf=(q * ecum_ref[0]).astype(BF16),
                        ktw_bf=(kt_ref[bb] * wend_ref[0]).astype(BF16), v_bf=v_ref[bb], lhs={}, ys=[]))

    def ret_scores(bb, hd):
        r = ret[bb]
        qs = slice(hd * RET_QK_DIM, (hd + 1) * RET_QK_DIM)
        sc = _dot_nt(r["q_bf"][:, qs], r["k_bf"][:, qs]) * dec_ref[0, hd]
        r["lhs"][hd] = jnp.concatenate([sc.astype(BF16), r["qe_bf"][:, qs]], axis=1)

    def ret_update(bb, hd):
        r = ret[bb]
        qs = slice(hd * RET_QK_DIM, (hd + 1) * RET_QK_DIM)
        v_h = r["v_bf"][:, hd * RET_V_DIM:(hd + 1) * RET_V_DIM]
        h_h = hr_ref[bb, hd]
        rhs = jnp.concatenate([v_h, h_h.astype(BF16)], axis=0)
        r["ys"].append(_dot(r["lhs"][hd], rhs))
        hr_ref[bb, hd] = etot_ref[0, hd] * h_h + _dot(r["ktw_bf"][qs, :], v_h)

    ssd = []
    for bb in chains:
        rowp = rowp_ref[bb, 0]
        ssd.append(dict(xs_bf=xs_ref[bb], bc=bc_ref[bb], cum_col=colp_ref[bb, 0],
                        cum_row=rowp[ROW_CUM:ROW_CUM + SSD_HEADS], dt_row=rowp[ROW_DT:ROW_DT + SSD_HEADS],
                        wend_row=rowp[ROW_WEND:ROW_WEND + SSD_HEADS],
                        etot_row=rowp[ROW_ETOT:ROW_ETOT + SSD_HEADS], gmats={}, ys=[]))

    def ssd_pair(bb, k):
        s = ssd[bb]
        g = (2 * k) // heads_per_group
        if g not in s["gmats"]:
            b_g = s["bc"][:, g * SSD_STATE:(g + 1) * SSD_STATE].astype(BF16)
            c_g = s["bc"][:, (SSD_GROUPS + g) * SSD_STATE:(SSD_GROUPS + g + 1) * SSD_STATE]
            s["gmats"][g] = (_dot_nt(c_g.astype(BF16), b_g), c_g, b_g)
        scores, c_g, b_g = s["gmats"][g]
        ha, hb = 2 * k, 2 * k + 1
        xs_pair = s["xs_bf"][:, k * LANES:(k + 1) * LANES]
        ht_pair = ht_ref[bb, k]
        rhs = jnp.concatenate([xs_pair, ht_pair.T.astype(BF16)], axis=0)
        y_heads = []
        for hd in (ha, hb):
            cb = jnp.broadcast_to(s["cum_col"][:, hd:hd + 1], (CHUNK, CHUNK))
            seg = cb - s["cum_row"][hd:hd + 1, :]
            dec = jnp.exp(jnp.where(mask, seg, NEG_BIG))
            m = scores * dec * s["dt_row"][hd:hd + 1, :]
            lhs = jnp.concatenate([m, c_g * jnp.exp(cb)], axis=1).astype(BF16)
            y_heads.append(_dot(lhs, rhs))
        s["ys"].append(jnp.where(lane_lo, y_heads[0], y_heads[1]))
        w_pair = jnp.where(sub_lo, s["wend_row"][ha:ha + 1, :], s["wend_row"][hb:hb + 1, :])
        e_pair = jnp.where(sub_lo, s["etot_row"][ha:ha + 1, :], s["etot_row"][hb:hb + 1, :])
        xw = (xst_ref[bb, k * LANES:(k + 1) * LANES, :].astype(F32) * w_pair).astype(BF16)
        ht_ref[bb, k] = e_pair * ht_pair + _dot(xw, b_g)

    for k in range(SSD_HEADS // 2):
        for bb in chains:
            ssd_pair(bb, k)
        for bb in chains:
            if k % 2 == 0:
                ret_scores(bb, k // 2)
            else:
                ret_update(bb, k // 2)
    y = [jnp.concatenate(ssd[bb]["ys"], axis=1) for bb in chains]
    yr = [jnp.concatenate(ret[bb]["ys"], axis=1) for bb in chains]

    for bb in chains:
        if direction == 0:
            yf_ref[bb] = y[bb].astype(yf_ref.dtype)
            yr_ref[bb] = yr[bb].astype(yr_ref.dtype)
            continue
        tot = yf_ref[bb].astype(F32) + y[bb] + ssd[bb]["xs_bf"].astype(F32) * dwide_ref[...]
        gated = tot * z_ref[bb]
        ms = jnp.mean(gated * gated, axis=-1, keepdims=True)
        ssd_out_ref[bb] = gated * lax.rsqrt(ms + RMS_EPS) * normw_ref[...]
        rtot = yr_ref[bb].astype(F32) + yr[bb]
        outs = []
        for hd in range(RET_HEADS):
            t_h = rtot[:, hd * RET_V_DIM:(hd + 1) * RET_V_DIM]
            mu = jnp.mean(t_h, axis=-1, keepdims=True)
            var = jnp.mean(jnp.square(t_h - mu), axis=-1, keepdims=True)
            outs.append((t_h - mu) * lax.rsqrt(var + LN_EPS))
        ret_out_ref[bb] = jnp.concatenate(outs, axis=1) * g_ref[bb]


def _mixer_scan(prep, z, rg, dwide, normw, rconsts, nc_ctx):
    xs_bf, xst_bf, bc, colp, rowp, rq, rk_bf, rkt, rv_bf = prep
    dec, ecum, wend, etot = rconsts
    bsz, s, _ = z.shape
    nc = s // CHUNK
    nb = SCAN_BATCH
    scratch = [pltpu.VMEM((nb, SSD_HEADS // 2, LANES, SSD_STATE), F32),
               pltpu.VMEM((nb, RET_HEADS, RET_QK_DIM, RET_V_DIM), F32)]

    def call(d, extra_in, extra_specs, out_shape, out_specs):
        cix = lambda c: _chunk_index(d, c, nc_ctx, nc)
        tokc = lambda wd: pl.BlockSpec((nb, CHUNK, wd), lambda b, c: (b, cix(c), 0))
        chan = lambda wd: pl.BlockSpec((nb, wd, CHUNK), lambda b, c: (b, 0, cix(c)))
        per_dir = lambda a: pl.BlockSpec((1,) + a.shape[1:], lambda b, c: (d,) + (0,) * (a.ndim - 1))
        common = [tokc(SSD_WIDTH), chan(SSD_WIDTH), tokc(bc.shape[-1]),
                  pl.BlockSpec((nb, 1, CHUNK, LANES), lambda b, c: (b, d, cix(c), 0)),
                  pl.BlockSpec((nb, 1, ROW_PACK, CHUNK), lambda b, c: (b, cix(c), d, 0)),
                  tokc(RET_QK_WIDTH), tokc(RET_QK_WIDTH), chan(RET_QK_WIDTH), tokc(RET_WIDTH),
                  per_dir(dec), per_dir(ecum), per_dir(wend), per_dir(etot)]
        return pl.pallas_call(
            functools.partial(_mixer_scan_kernel, direction=d),
            out_shape=out_shape,
            grid=(bsz // nb, nc),
            in_specs=common + extra_specs(tokc),
            out_specs=out_specs(tokc),
            scratch_shapes=scratch,
            compiler_params=_params("parallel", "arbitrary"),
            name="mixer_scan_fwd" if d == 0 else "mixer_scan_bwd",
        )(xs_bf, xst_bf, bc, colp, rowp, rq, rk_bf, rkt, rv_bf, dec, ecum, wend, etot, *extra_in)

    row1 = lambda wd: pl.BlockSpec((1, wd), lambda b, c: (0, 0))
    yf, yr = call(0, [], lambda tokc: [],
                  [jax.ShapeDtypeStruct((bsz, s, SSD_WIDTH), BF16), jax.ShapeDtypeStruct((bsz, s, RET_WIDTH), BF16)],
                  lambda tokc: [tokc(SSD_WIDTH), tokc(RET_WIDTH)])
    return call(1, [yf, yr, z, rg, dwide, normw],
                lambda tokc: [tokc(SSD_WIDTH), tokc(RET_WIDTH), tokc(SSD_WIDTH), tokc(RET_WIDTH),
                              row1(SSD_WIDTH), row1(SSD_WIDTH)],
                [jax.ShapeDtypeStruct((bsz, s, SSD_WIDTH), F32), jax.ShapeDtypeStruct((bsz, s, RET_WIDTH), F32)],
                lambda tokc: [tokc(SSD_WIDTH), tokc(RET_WIDTH)])


def _rms(v, g):
    return v * lax.rsqrt(jnp.mean(v * v, axis=-1, keepdims=True) + RMS_EPS) * g


def _mla_proj_kernel(cq_ref, ckv_ref, small_ref, cos_ref, sa_ref, sb_ref, qn_ref, kvn_ref,
                     wq_ref, wk_ref, wv_ref, q_ref, k_ref, v_ref):
    cqn = _rms(cq_ref[0], qn_ref[...]).astype(BF16)
    ckvn = _rms(ckv_ref[0], kvn_ref[...]).astype(BF16)
    q = _dot(cqn, wq_ref[...])
    kn = _dot(ckvn, wk_ref[...])
    vlane = lax.broadcasted_iota(jnp.int32, (1, MLA_PAIRS * MLA_V_SLOT), 1)
    ones = jnp.where(vlane % MLA_V_SLOT >= 2 * MLA_V_DIM, 1.0, 0.0)
    v_ref[0] = (_dot(ckvn, wv_ref[...]) + ones).astype(BF16)
    cos_t, sin_a, sin_b = cos_ref[...], sa_ref[...], sb_ref[...]
    lane = lax.broadcasted_iota(jnp.int32, small_ref.shape[1:], 1)
    in_rope = jnp.logical_and(lane >= KROPE_LANE, lane < KROPE_LANE + MLA_ROPE_DIM)
    kr = _rope3(jnp.where(in_rope, small_ref[0], 0.0), cos_t, sin_a, sin_b, MLA_ROPE_DIM // 2)
    for hd in range(MLA_HEADS):
        sl = slice(hd * MLA_HEAD_SLOT, (hd + 1) * MLA_HEAD_SLOT)
        q_ref[0, :, sl] = (_rope3(q[:, sl], cos_t, sin_a, sin_b, MLA_ROPE_DIM // 2) * MLA_Q_SCALE).astype(BF16)
        k_ref[0, :, sl] = (kn[:, sl] + kr).astype(BF16)


def _mla_proj(cq, ckv, small, cos_t, sin_a, sin_b, qn, kvn, wq, wk, wv):
    bsz, s, _ = cq.shape
    nt = s // TOKEN_TILE
    hw = MLA_HEADS * MLA_HEAD_SLOT
    vw = MLA_PAIRS * MLA_V_SLOT
    tok = lambda wd: pl.BlockSpec((1, TOKEN_TILE, wd), lambda b, t: (b, t, 0))
    tab = pl.BlockSpec((TOKEN_TILE, LANES), lambda b, t: (t, 0))
    full = lambda a: pl.BlockSpec(a.shape, lambda b, t: (0,) * a.ndim)
    return pl.pallas_call(
        _mla_proj_kernel,
        out_shape=[jax.ShapeDtypeStruct((bsz, s, hw), BF16),
                   jax.ShapeDtypeStruct((bsz, s, hw), BF16),
                   jax.ShapeDtypeStruct((bsz, s, vw), BF16)],
        grid=(bsz, nt),
        in_specs=[tok(MLA_Q_LORA), tok(MLA_KV_LORA), tok(LANES), tab, tab, tab,
                  full(qn), full(kvn), full(wq), full(wk), full(wv)],
        out_specs=[tok(hw), tok(hw), tok(vw)],
        compiler_params=_params("parallel", "arbitrary"),
        name="mla_proj",
    )(cq, ckv, small, cos_t, sin_a, sin_b, qn, kvn, wq, wk, wv)


ATTN_KEY_CHUNK = 256


def _attn_step(q_ref, k_ref, v_ref, g_ref, o_ref, slot_a, slot_b, n_keys):
    heads = range(2)
    hsl = [slice(hd * MLA_HEAD_SLOT, (hd + 1) * MLA_HEAD_SLOT) for hd in heads]
    if slot_a is not None:
        q = [q_ref[0, :, hsl[hd]] for hd in heads]
        mcol = [None, None]
    if slot_b is not None:
        mrow = [slot_b[1][hd] for hd in heads]
        pv = [None, None]
    for c in range(n_keys // ATTN_KEY_CHUNK):
        ks = slice(c * ATTN_KEY_CHUNK, (c + 1) * ATTN_KEY_CHUNK)
        for hd in heads:
            if slot_a is not None:
                s = _dot_nt(q[hd], k_ref[0, ks, hsl[hd]])
                slot_a[0][hd, :, ks] = s
                cm = s[:, :LANES]
                for w in range(1, ATTN_KEY_CHUNK // LANES):
                    cm = jnp.maximum(cm, s[:, w * LANES:(w + 1) * LANES])
                mcol[hd] = cm if c == 0 else jnp.maximum(mcol[hd], cm)
            if slot_b is not None:
                p = jnp.exp2(slot_b[0][hd, :, ks] - mrow[hd]).astype(BF16)
                part = _dot(p, v_ref[0, ks, :])
                pv[hd] = part if c == 0 else pv[hd] + part
    if slot_a is not None:
        for hd in heads:
            slot_a[1][hd] = jnp.max(mcol[hd], axis=-1, keepdims=True)
    if slot_b is not None:
        outs = [pv[hd][:, :LANES] / pv[hd][:, LANES:] for hd in heads]
        lane = lax.broadcasted_iota(jnp.int32, outs[0].shape, 1)
        o = jnp.where(lane < MLA_V_DIM, outs[0], outs[1])
        o_ref[0] = o * g_ref[0]


def _mla_attn_ctx_kernel(q_ref, k_ref, v_ref, g_ref, o_ref, s_ref, m_ref):
    n_keys = k_ref.shape[1]
    _attn_step(q_ref, k_ref, v_ref, g_ref, o_ref, (s_ref, m_ref), None, n_keys)
    _attn_step(q_ref, k_ref, v_ref, g_ref, o_ref, None, (s_ref, m_ref), n_keys)


def _mla_attn_ctx(q, k, v, gate, ctx_tiles):
    bsz, s, _ = q.shape
    tq = TOKEN_TILE
    n_ctx = ctx_tiles * tq
    pair_w = 2 * MLA_HEAD_SLOT
    return pl.pallas_call(
        _mla_attn_ctx_kernel,
        out_shape=jax.ShapeDtypeStruct((bsz, s, MLA_WIDTH), F32),
        grid=(bsz, MLA_PAIRS, ctx_tiles),
        in_specs=[pl.BlockSpec((1, tq, pair_w), lambda b, j, t: (b, t, j)),
                  pl.BlockSpec((1, n_ctx, pair_w), lambda b, j, t: (b, 0, j)),
                  pl.BlockSpec((1, n_ctx, MLA_V_SLOT), lambda b, j, t: (b, 0, j)),
                  pl.BlockSpec((1, tq, LANES), lambda b, j, t: (b, t, j))],
        out_specs=pl.BlockSpec((1, tq, LANES), lambda b, j, t: (b, t, j)),
        scratch_shapes=[pltpu.VMEM((2, tq, n_ctx), F32), pltpu.VMEM((2, tq, 1), F32)],
        compiler_params=_params("parallel", "arbitrary", "arbitrary"),
        name="mla_attn_ctx",
    )(q, k, v, gate)


def _mla_attn_lat_kernel(*refs, n_tiles, aliased):
    if aliased:
        refs = refs[1:]
    q_ref, k_ref, v_ref, g_ref, o_ref, s0, m0, s1, m1 = refs
    n_keys = k_ref.shape[1]
    slots = ((s0, m0), (s1, m1))
    n = pl.program_id(0)

    @pl.when(n == 0)
    def _():
        _attn_step(q_ref, k_ref, v_ref, g_ref, o_ref, slots[0], None, n_keys)

    for par in range(2):
        @pl.when(jnp.logical_and(jnp.logical_and(n > 0, n < n_tiles), n % 2 == par))
        def _():
            _attn_step(q_ref, k_ref, v_ref, g_ref, o_ref, slots[par], slots[1 - par], n_keys)

    @pl.when(n == n_tiles)
    def _():
        _attn_step(q_ref, k_ref, v_ref, g_ref, o_ref, None, slots[(n_tiles - 1) % 2], n_keys)


def _mla_attn_lat(q, k, v, gate, ctx_tiles, ctx_out=None):
    bsz, s, _ = q.shape
    tq = TOKEN_TILE
    nq = s // tq - ctx_tiles
    n_tiles = bsz * MLA_PAIRS * nq
    pair_w = 2 * MLA_HEAD_SLOT
    out_lo = 0 if ctx_out is not None else ctx_tiles

    def decode(i):
        return i // (MLA_PAIRS * nq), (i // nq) % MLA_PAIRS, i % nq + ctx_tiles

    def ahead(n):
        return decode(jnp.minimum(n, n_tiles - 1))

    def behind(n):
        return decode(jnp.maximum(n - 1, 0))

    def q_map(n):
        b, j, t = ahead(n)
        return b, t, j

    def k_map(n):
        b, j, _ = ahead(n)
        return b, 0, j

    def v_map(n):
        b, j, _ = behind(n)
        return b, 0, j

    def g_map(n):
        b, j, t = behind(n)
        return b, t, j

    def o_map(n):
        b, j, t = behind(n)
        return b, t - out_lo, j

    in_specs = [pl.BlockSpec((1, tq, pair_w), q_map),
                pl.BlockSpec((1, s, pair_w), k_map),
                pl.BlockSpec((1, s, MLA_V_SLOT), v_map),
                pl.BlockSpec((1, tq, LANES), g_map)]
    args = [q, k, v, gate]
    aliases = {}
    if ctx_out is not None:
        in_specs = [pl.BlockSpec(memory_space=pl.ANY)] + in_specs
        args = [ctx_out] + args
        aliases = {0: 0}
    kern = functools.partial(_mla_attn_lat_kernel, n_tiles=n_tiles, aliased=ctx_out is not None)
    return pl.pallas_call(
        kern,
        out_shape=jax.ShapeDtypeStruct((bsz, s - out_lo * tq, MLA_WIDTH), F32),
        grid=(n_tiles + 1,),
        in_specs=in_specs,
        out_specs=pl.BlockSpec((1, tq, LANES), o_map),
        scratch_shapes=[pltpu.VMEM((2, tq, s), F32), pltpu.VMEM((2, tq, 1), F32),
                        pltpu.VMEM((2, tq, s), F32), pltpu.VMEM((2, tq, 1), F32)],
        input_output_aliases=aliases,
        compiler_params=_params("arbitrary"),
        name="mla_attn",
    )(*args)


def _outproj_kernel(*refs, n_stream, ctx_tiles, t_lo, n_per, n_tiles, alpha):
    ssd_ref, mla_ref, ret_ref = refs[:3]
    x_refs = refs[3:3 + n_stream]
    mod_ref, w_ref, g_ref, b_ref, o_ref, acc0_ref, acc1_ref = refs[3 + n_stream:]
    slots = (acc0_ref, acc1_ref)
    n = pl.program_id(0)
    n_blocks = 4
    bw = D_MODEL // n_blocks

    def step(slot_a, slot_b):
        def project(j):
            cols = slice(j * bw, (j + 1) * bw)
            acc = _dot(ssd_ref[0].astype(BF16), w_ref[:SSD_WIDTH, cols])
            acc = acc + _dot(mla_ref[0].astype(BF16), w_ref[SSD_WIDTH:SSD_WIDTH + MLA_WIDTH, cols])
            acc = acc + _dot(ret_ref[0].astype(BF16), w_ref[SSD_WIDTH + MLA_WIDTH:, cols])
            slot_a[:, cols] = acc

        passes = []
        if slot_b is not None:
            state = {}

            def residual():
                t = (n - 1) % n_per + t_lo
                gate = mod_ref[0, 0][:, 2 * D_MODEL:]
                r = alpha * _stream_tile(x_refs, t, ctx_tiles) + gate * slot_b[...]
                state["r"] = r
                state["mu"] = jnp.mean(r, axis=-1, keepdims=True)

            def variance():
                state["c"] = state["r"] - state["mu"]
                state["var"] = jnp.mean(jnp.square(state["c"]), axis=-1, keepdims=True)

            def normalise():
                o_ref[0] = state["c"] * lax.rsqrt(state["var"] + LN_EPS) * g_ref[...] + b_ref[...]

            passes = [residual, variance, normalise]
        for j in range(n_blocks):
            if slot_a is not None:
                project(j)
            if j < len(passes):
                passes[j]()

    @pl.when(n == 0)
    def _():
        step(slots[0], None)

    for par in range(2):
        @pl.when(jnp.logical_and(jnp.logical_and(n > 0, n < n_tiles), n % 2 == par))
        def _():
            step(slots[par], slots[1 - par])

    @pl.when(n == n_tiles)
    def _():
        step(None, slots[(n_tiles - 1) % 2])


def _outproj(ssd_o, mla_o, mla_lo, ret_o, stream, mod_all, w, ln_g, ln_b, t_lo, ctx_tiles, alpha):
    bsz, s, _ = ssd_o.shape
    d = stream[0].shape[-1]
    n_per = s // TOKEN_TILE - t_lo
    n_tiles = bsz * n_per

    def ahead(n):
        i = jnp.minimum(n, n_tiles - 1)
        return i // n_per, i % n_per + t_lo

    def behind(n):
        i = jnp.maximum(n - 1, 0)
        return i // n_per, i % n_per + t_lo

    def tok(wd, lo=0):
        def index(n):
            b, t = ahead(n)
            return b, t - lo, 0
        return pl.BlockSpec((1, TOKEN_TILE, wd), index)

    def stream_spec(pick):
        def index(n):
            b, t = behind(n)
            return b, pick(t), 0
        return pl.BlockSpec((1, TOKEN_TILE, d), index)

    if len(stream) == 1:
        x_specs = [stream_spec(lambda t: t)]
    else:
        x_specs = [stream_spec(lambda t: jnp.minimum(t, ctx_tiles - 1)),
                   stream_spec(lambda t: jnp.maximum(t - ctx_tiles, 0))]

    def mod_index(n):
        b, t = behind(n)
        return b, jnp.minimum(t, ctx_tiles) // ctx_tiles, 0, 0

    def out_index(n):
        b, t = behind(n)
        return b, t - t_lo, 0

    const = lambda shape: pl.BlockSpec(shape, lambda n: (0,) * len(shape))
    kern = functools.partial(_outproj_kernel, n_stream=len(stream), ctx_tiles=ctx_tiles, t_lo=t_lo,
                             n_per=n_per, n_tiles=n_tiles, alpha=alpha)
    return pl.pallas_call(
        kern,
        out_shape=jax.ShapeDtypeStruct((bsz, n_per * TOKEN_TILE, d), F32),
        grid=(n_tiles + 1,),
        in_specs=[tok(SSD_WIDTH), tok(MLA_WIDTH, mla_lo), tok(RET_WIDTH)] + x_specs + [
            pl.BlockSpec((1, 1, 1, 3 * d), mod_index), const(w.shape), const((1, d)), const((1, d))],
        out_specs=pl.BlockSpec((1, TOKEN_TILE, d), out_index),
        scratch_shapes=[pltpu.VMEM((TOKEN_TILE, d), F32), pltpu.VMEM((TOKEN_TILE, d), F32)],
        compiler_params=_params("arbitrary"),
        name="out_proj",
    )(ssd_o, mla_o, ret_o, *stream, mod_all, w, ln_g.reshape(1, d), ln_b.reshape(1, d))


def _rope_tables(n_ctx, n_lat, rot_dim, width, lane0, period):
    t = np.arange(n_lat)
    row = (t // GRID_W).astype(np.float32)
    col = (t % GRID_W).astype(np.float32)
    n_freq = rot_dim // 4
    inv = (ROPE_THETA ** (-np.arange(n_freq, dtype=np.float32) / n_freq)).astype(np.float32)
    ang = np.concatenate([row[:, None] * inv, col[:, None] * inv], -1).astype(np.float32)
    cos, sin = np.cos(ang), np.sin(ang)
    half = rot_dim // 2
    lane = np.arange(width) % period - lane0
    in_rot = (lane >= 0) & (lane < rot_dim)
    idx = np.where(in_rot, lane % half, 0)
    first = in_rot & (lane < half)
    second = in_rot & (lane >= half)
    cos_t = np.where(in_rot[None, :], cos[:, idx], 1.0)
    sin_a = np.where(first[None, :], -sin[:, idx], 0.0)
    sin_b = np.where(second[None, :], sin[:, idx], 0.0)
    ctx_pad = lambda a, v: np.concatenate([np.full((n_ctx, width), v), a], axis=0).astype(np.float32)
    return ctx_pad(cos_t, 1.0), ctx_pad(sin_a, 0.0), ctx_pad(sin_b, 0.0)


def _mla_weights(w_uq, w_ukv):
    qd = MLA_NOPE_DIM + MLA_ROPE_DIM
    wq = w_uq.reshape(MLA_Q_LORA, MLA_HEADS, qd)
    wq = jnp.pad(wq, ((0, 0), (0, 0), (0, MLA_HEAD_SLOT - qd))).reshape(MLA_Q_LORA, MLA_HEADS * MLA_HEAD_SLOT)
    wkv = w_ukv.reshape(MLA_KV_LORA, MLA_HEADS, MLA_NOPE_DIM + MLA_V_DIM)
    wk = jnp.pad(wkv[..., :MLA_NOPE_DIM], ((0, 0), (0, 0), (0, MLA_HEAD_SLOT - MLA_NOPE_DIM)))
    wk = wk.reshape(MLA_KV_LORA, MLA_HEADS * MLA_HEAD_SLOT)
    wv = wkv[..., MLA_NOPE_DIM:].reshape(MLA_KV_LORA, MLA_PAIRS, 2 * MLA_V_DIM)
    wv = jnp.pad(wv, ((0, 0), (0, 0), (0, MLA_V_SLOT - 2 * MLA_V_DIM))).reshape(MLA_KV_LORA, MLA_PAIRS * MLA_V_SLOT)
    return wq.astype(BF16), wk.astype(BF16), wv.astype(BF16)


def _lane_pad(v):
    return jnp.pad(v, (0, LANES - v.shape[0])).reshape(1, LANES)


def kernel(x, c, ctx, c_ctx, w_ada, b_ada, w_in, ssd_conv_w, ssd_conv_b, ssd_a_log_f, ssd_a_log_b,
           ssd_dt_bias_f, ssd_dt_bias_b, ssd_d, ssd_norm_w, mla_q_norm, mla_w_uq, mla_kv_norm, mla_w_ukv,
           ret_log_rate_f, ret_log_rate_b, w_out, ln_g, ln_b):
    bsz, n_lat, d = x.shape
    n_ctx = ctx.shape[1]
    depth = w_in.shape[0]
    assert d == D_MODEL and n_ctx == TOKEN_TILE and n_lat % TOKEN_TILE == 0 and bsz % SCAN_BATCH == 0
    alpha = (2 * depth) ** 0.25
    s = n_ctx + n_lat
    nc_ctx = n_ctx // CHUNK
    ctx_tiles = n_ctx // TOKEN_TILE
    nt = s // TOKEN_TILE

    cos_m, sa_m, sb_m = _rope_tables(n_ctx, n_lat, MLA_ROPE_DIM, LANES, MLA_ROPE_LANE, LANES)
    cos_r, sa_r, sb_r = _rope_tables(n_ctx, n_lat, RET_QK_DIM, RET_QK_WIDTH, 0, RET_QK_DIM)

    rows = 16
    cc = jnp.zeros((rows, d), F32).at[:bsz].set(c).at[bsz].set(c_ctx)
    stream = (ctx, x)

    for i in range(depth):
        last = i == depth - 1
        mod = _ada(cc, w_ada, b_ada, i)
        mod_all = jnp.stack([jnp.broadcast_to(mod[bsz], (bsz, 3 * d)), mod[:bsz]], axis=1)
        mod_all = mod_all.reshape(bsz, 2, 1, 3 * d)
        both = lambda f, b: jnp.stack([f, b], axis=0)
        head_rows = lambda v: jnp.broadcast_to(v[:, None], (SSD_HEADS, TOKEN_TILE))
        alogt = both(head_rows(ssd_a_log_f[i]), head_rows(ssd_a_log_b[i]))
        dtbt = both(head_rows(ssd_dt_bias_f[i]), head_rows(ssd_dt_bias_b[i]))
        (zs, xs_bf, xst_bf, bc, colp, rowp, small, cq, ckv, mgs, rq, rk_bf, rkt, rv_bf, rgs) = _inproj(
            stream, mod_all, _relayout_w_in(w_in, i), ssd_conv_w[i], ssd_conv_b[i].reshape(1, -1), alogt, dtbt,
            cos_r, sa_r, sb_r, ctx_tiles)

        dwide = jnp.repeat(ssd_d[i], SSD_HEAD_DIM).reshape(1, SSD_WIDTH)
        rates = both(_lane_pad(ret_log_rate_f[i]), _lane_pad(ret_log_rate_b[i]))
        prep = (xs_bf, xst_bf, bc, colp, rowp, rq, rk_bf, rkt, rv_bf)
        ssd_o, ret_o = _mixer_scan(prep, zs, rgs, dwide, ssd_norm_w[i].reshape(1, -1), _ret_consts(rates), nc_ctx)

        wq, wk, wv = _mla_weights(mla_w_uq[i], mla_w_ukv[i])
        q, k, v = _mla_proj(cq, ckv, small, cos_m, sa_m, sb_m, mla_q_norm[i].reshape(1, -1),
                            mla_kv_norm[i].reshape(1, -1), wq, wk, wv)
        mla_lo = ctx_tiles if last else 0
        mla_o = _mla_attn_lat(q, k, v, mgs, ctx_tiles, None if last else _mla_attn_ctx(q, k, v, mgs, ctx_tiles))

        out = _outproj(ssd_o, mla_o, mla_lo, ret_o, stream, mod_all, w_out[i].astype(BF16), ln_g[i], ln_b[i],
                       ctx_tiles if last else 0, ctx_tiles, alpha)
        stream = (out,)
    return stream[0]
```
